```python
import jax
import jax.numpy as jnp
from jax import lax
import numpy as np

D_MODEL = 2048
BATCH = 16
SEQ = 2048
DEPTH = 4

N_GROUPS = 4
GROUP_WIDTH = D_MODEL // N_GROUPS
HEAD_DIM = 64
N_HEADS = GROUP_WIDTH // HEAD_DIM
D_FF = ((8 * D_MODEL + 3 * 256 - 1) // (3 * 256)) * 256

RET_CHUNK = 128
HGRN_CHUNK = 32
MLSTM_CHUNK = 64
CONV_WIDTH = 4
ROPE_BASE = 10000.0
RWKV_W_LORA = 64
RWKV_A_LORA = 64
RWKV_V_LORA = 32
RWKV_G_LORA = 128
RWKV_DECAY_SCALE = 0.6065306597126334
RWKV_LN_EPS = 64e-5
NORM_EPS = 1e-6

RET_SPLIT = (GROUP_WIDTH,) * 4
HGRN_SPLIT = (GROUP_WIDTH,) * 4
MLSTM_SPLIT = (GROUP_WIDTH,) * 4 + (N_HEADS, N_HEADS)
RWKV_SPLIT = (GROUP_WIDTH,) * 3 + (RWKV_W_LORA, RWKV_A_LORA, RWKV_G_LORA)
GROUP_COLS = (sum(RET_SPLIT), sum(HGRN_SPLIT), sum(MLSTM_SPLIT), sum(RWKV_SPLIT))
N_IN = sum(GROUP_COLS)

kernel_name = 'hybrid_parallel_groups_trunk'


def split_cols(t, sizes):
    return jnp.split(t, [int(i) for i in np.cumsum(sizes)[:-1]], axis=-1)


def rms_norm(x, w):
    xf = x.astype(jnp.float32)
    y = xf * lax.rsqrt(jnp.mean(xf * xf, axis=-1, keepdims=True) + NORM_EPS)
    return (y * w.astype(jnp.float32)).astype(x.dtype)


def to_heads(t):
    return t.reshape(t.shape[0], t.shape[1], N_HEADS, HEAD_DIM).transpose(0, 2, 1, 3)


def from_heads(t):
    return t.transpose(0, 2, 1, 3)


def head_rms(t, w=None):
    y = t * lax.rsqrt(jnp.mean(t * t, axis=-1, keepdims=True) + NORM_EPS)
    if w is not None:
        y = y * w.reshape(N_HEADS, HEAD_DIM)
    return y.reshape(t.shape[0], t.shape[1], -1)


def to_chunks(t, c):
    b, h, s = t.shape[:3]
    return jnp.moveaxis(t.reshape(b, h, s // c, c, *t.shape[3:]), 2, 0)


def from_chunks(t):
    t = jnp.moveaxis(t, 0, 2)
    return t.reshape(t.shape[0], t.shape[1], -1, t.shape[-1])


def rotary(t, pos):
    half = HEAD_DIM // 2
    inv_freq = ROPE_BASE ** (-jnp.arange(half, dtype=jnp.float32) / half)
    ang = pos[:, None] * inv_freq[None, :]
    cos, sin = jnp.cos(ang), jnp.sin(ang)
    t1, t2 = t[..., :half], t[..., half:]
    return jnp.concatenate([t1 * cos - t2 * sin, t1 * sin + t2 * cos], axis=-1)


def token_shift(t):
    return jnp.pad(t, ((0, 0), (1, 0), (0, 0)))[:, :-1]


def causal_dwconv(t, w, b):
    seq = t.shape[1]
    tp = jnp.pad(t, ((0, 0), (CONV_WIDTH - 1, 0), (0, 0)))
    out = b + tp[:, 0:seq] * w[0]
    for j in range(1, CONV_WIDTH):
        out = out + tp[:, j:j + seq] * w[j]
    return out


def chunked_retention(q, k, v):
    c = RET_CHUNK
    b, h, s, d = q.shape
    log_gamma = jnp.log1p(-jnp.exp2(-5.0 - jnp.arange(h, dtype=jnp.float32)))
    idx = jnp.arange(c, dtype=jnp.float32)
    rel = idx[:, None] - idx[None, :]
    intra = jnp.where(rel >= 0, jnp.exp(log_gamma[:, None, None] * jnp.maximum(rel, 0.0)), 0.0)
    q_dec = jnp.exp(log_gamma[:, None] * (idx + 1.0))[:, :, None]
    k_dec = jnp.exp(log_gamma[:, None] * (c - 1.0 - idx))[:, :, None]
    chunk_dec = jnp.exp(log_gamma * c)[:, None, None]

    def step(state, inp):
        qc, kc, vc = inp
        scores = jnp.einsum('bhtd,bhsd->bhts', qc, kc) * intra
        out = (jnp.einsum('bhts,bhse->bhte', scores, vc)
               + jnp.einsum('bhtd,bhde->bhte', qc * q_dec, state))
        state = chunk_dec * state + jnp.einsum('bhsd,bhse->bhde', kc * k_dec, vc)
        return state, out

    _, o = lax.scan(step, jnp.zeros((b, h, d, d), q.dtype),
                    (to_chunks(q, c), to_chunks(k, c), to_chunks(v, c)))
    return from_chunks(o)


def retention_group(p):
    q, k, v, g = split_cols(p, RET_SPLIT)
    pos = jnp.arange(p.shape[1], dtype=jnp.float32)
    qh = rotary(to_heads(q), pos)
    kh = rotary(to_heads(k), pos) * HEAD_DIM ** -0.5
    o = chunked_retention(qh, kh, to_heads(v))
    return head_rms(from_heads(o)) * jax.nn.silu(g)


def chunked_hgrn2(q, k, v, log_f):
    c = HGRN_CHUNK
    b, h, s, dk = q.shape
    dv = v.shape[-1]
    causal = jnp.tril(jnp.ones((c, c), dtype=bool))[:, :, None]

    def step(state, inp):
        qc, kc, vc, gc = inp
        cum = jnp.cumsum(gc, axis=2)
        pair = jnp.where(causal,
                         jnp.exp(jnp.minimum(cum[:, :, :, None, :] - cum[:, :, None, :, :], 0.0)), 0.0)
        scores = jnp.einsum('bhtd,bhsd,bhtsd->bhts', qc, kc, pair)
        out = (jnp.einsum('bhts,bhse->bhte', scores, vc)
               + jnp.einsum('bhtd,bhde->bhte', qc * jnp.exp(cum), state))
        cum_end = cum[:, :, -1:, :]
        state = (jnp.exp(cum_end)[:, :, 0, :, None] * state
                 + jnp.einsum('bhsd,bhse->bhde', kc * jnp.exp(cum_end - cum), vc))
        return state, out

    _, o = lax.scan(step, jnp.zeros((b, h, dk, dv), q.dtype),
                    (to_chunks(q, c), to_chunks(k, c), to_chunks(v, c), to_chunks(log_f, c)))
    return from_chunks(o)


def hgrn2_group(p, lb, norm_w):
    q, f, i, g = split_cols(p, HGRN_SPLIT)
    log_f = jnp.logaddexp(jnp.log(lb), jnp.log1p(-lb) + jax.nn.log_sigmoid(f))
    k = (1.0 - lb) * jax.nn.sigmoid(-f)
    o = chunked_hgrn2(to_heads(jax.nn.silu(q)) * HEAD_DIM ** -0.5, to_heads(k), to_heads(i), to_heads(log_f))
    return head_rms(from_heads(o), norm_w) * jax.nn.sigmoid(g)


def chunked_mlstm(q, k, v, log_i, log_f):
    c = MLSTM_CHUNK
    b, h, s, dk = q.shape
    dv = v.shape[-1]
    causal = jnp.tril(jnp.ones((c, c), dtype=bool))

    def step(carry, inp):
        cmem, nvec, m = carry
        qc, kc, vc, ic, fc = inp
        cum = jnp.cumsum(fc, axis=-1)
        logd = jnp.where(causal, cum[..., :, None] - cum[..., None, :] + ic[..., None, :], -jnp.inf)
        log_inter = cum + m[..., None]
        m_t = jnp.maximum(jnp.max(logd, axis=-1), log_inter)
        scores = jnp.einsum('bhtd,bhsd->bhts', qc, kc) * jnp.exp(logd - m_t[..., None])
        w_inter = jnp.exp(log_inter - m_t)
        num = (jnp.einsum('bhts,bhse->bhte', scores, vc)
               + w_inter[..., None] * jnp.einsum('bhtd,bhde->bhte', qc, cmem))
        den = jnp.sum(scores, axis=-1) + w_inter * jnp.einsum('bhtd,bhd->bht', qc, nvec)
        h_out = num / jnp.maximum(jnp.abs(den), jnp.exp(-m_t))[..., None]
        cum_end = cum[..., -1]
        log_w = cum_end[..., None] - cum + ic
        m_new = jnp.maximum(cum_end + m, jnp.max(log_w, axis=-1))
        decay = jnp.exp(cum_end + m - m_new)
        kw = kc * jnp.exp(log_w - m_new[..., None])[..., None]
        cmem = decay[..., None, None] * cmem + jnp.einsum('bhsd,bhse->bhde', kw, vc)
        nvec = decay[..., None] * nvec + jnp.sum(kw, axis=2)
        return (cmem, nvec, m_new), h_out

    init = (jnp.zeros((b, h, dk, dv), q.dtype), jnp.zeros((b, h, dk), q.dtype), jnp.zeros((b, h), q.dtype))
    _, o = lax.scan(step, init, (to_chunks(q, c), to_chunks(k, c), to_chunks(v, c),
                                 to_chunks(log_i, c), to_chunks(log_f, c)))
    return from_chunks(o)


def mlstm_group(p, conv_w, conv_b, i_bias, f_bias, norm_w):
    q, k, v, o, ig, fg = split_cols(p, MLSTM_SPLIT)
    qk = jax.nn.silu(causal_dwconv(jnp.concatenate([q, k], axis=-1), conv_w, conv_b))
    q, k = jnp.split(qk, 2, axis=-1)
    log_i = jnp.swapaxes(ig + i_bias, 1, 2)
    log_f = jnp.swapaxes(jax.nn.log_sigmoid(fg + f_bias), 1, 2)
    hm = chunked_mlstm(to_heads(q), to_heads(k) * HEAD_DIM ** -0.5, to_heads(v), log_i, log_f)
    return jax.nn.sigmoid(o) * head_rms(from_heads(hm), norm_w)


def rwkv7_scan(r, log_w, k, v, kk, a):
    b, s, h, d = r.shape
    xs = tuple(jnp.swapaxes(t, 0, 1) for t in (r, log_w, k, v, kk, a))

    def step(state, inp):
        rt, lwt, kt, vt, kkt, at = inp
        state = (state * jnp.exp(lwt)[:, :, None, :]
                 - jnp.einsum('bhvk,bhk->bhv', state, kkt)[..., None] * (at * kkt)[:, :, None, :]
                 + vt[..., None] * kt[:, :, None, :])
        return state, jnp.einsum('bhvk,bhk->bhv', state, rt)

    _, y = lax.scan(step, jnp.zeros((b, h, d, d), r.dtype), xs)
    return jnp.swapaxes(y, 0, 1)


def rwkv7_group(p, v_first, mu, w0, w_up, a0, a_up, g_up, k_k, k_a, r_k, ln_w, ln_b, vmix):
    b, s, _ = p.shape
    p = p + (token_shift(p) - p) * mu
    r, k, v, wd, ad, gd = split_cols(p, RWKV_SPLIT)
    log_w = -RWKV_DECAY_SCALE * jax.nn.sigmoid(w0 + jnp.tanh(wd) @ w_up)
    a = jax.nn.sigmoid(a0 + ad @ a_up)
    g = jax.nn.sigmoid(gd) @ g_up
    if vmix is None:
        v_first = v
    else:
        v0, v_down, v_up = vmix
        v = v + (v_first - v) * jax.nn.sigmoid(v0 + (v @ v_down) @ v_up)

    def hs(t):
        return t.reshape(b, s, N_HEADS, HEAD_DIM)

    kk = hs(k * k_k)
    kk = kk / jnp.maximum(jnp.sqrt(jnp.sum(kk * kk, axis=-1, keepdims=True)), 1e-12)
    k = k * (1.0 + (a - 1.0) * k_a)
    y = rwkv7_scan(hs(r), hs(log_w), hs(k), hs(v), kk, hs(a))
    mean = jnp.mean(y, axis=-1, keepdims=True)
    var = jnp.mean(jnp.square(y - mean), axis=-1, keepdims=True)
    y = ((y - mean) * lax.rsqrt(var + RWKV_LN_EPS)).reshape(b, s, -1) * ln_w + ln_b
    bonus = (jnp.sum(hs(r * k * r_k), axis=-1, keepdims=True) * hs(v)).reshape(b, s, -1)
    return (y + bonus) * g, v_first


def setup_inputs(seed: int = 0) -> dict:
    key = jax.random.key(seed)
    keys = jax.random.split(key, 31)
    counter = iter(range(31))
    G, H, L = GROUP_WIDTH, N_HEADS, DEPTH

    def nrm(shape, scale):
        return scale * jax.random.normal(keys[next(counter)], shape, jnp.float32)

    def gain(shape):
        return 1.0 + nrm(shape, 0.05)

    return {
        'x': nrm((BATCH, SEQ, D_MODEL), 1.0),
        'w_in': nrm((L, D_MODEL, N_IN), D_MODEL ** -0.5),
        'w_out': nrm((L, D_MODEL, D_MODEL), D_MODEL ** -0.5),
        'norm_pre_mix': gain((L, D_MODEL)),
        'norm_post_mix': gain((L, D_MODEL)),
        'norm_pre_ffn': gain((L, D_MODEL)),
        'norm_post_ffn': gain((L, D_MODEL)),
        'w_ffn_gate': nrm((L, D_MODEL, D_FF), D_MODEL ** -0.5),
        'w_ffn_up': nrm((L, D_MODEL, D_FF), D_MODEL ** -0.5),
        'w_ffn_down': nrm((L, D_FF, D_MODEL), D_FF ** -0.5),
        'hgrn_lb_logits': nrm((L, G), 0.5),
        'hgrn_norm_w': gain((L, G)),
        'mlstm_conv_w': nrm((L, CONV_WIDTH, 2 * G), CONV_WIDTH ** -0.5),
        'mlstm_conv_b': nrm((L, 2 * G), 0.02),
        'mlstm_i_bias': nrm((L, H), 0.1) - 1.0,
        'mlstm_f_bias': nrm((L, H), 0.5) + 3.0,
        'mlstm_norm_w': gain((L, G)),
        'rwkv_mu': jax.random.uniform(keys[next(counter)], (L, GROUP_COLS[3]), jnp.float32),
        'rwkv_w0': nrm((L, G), 0.5),
        'rwkv_w_up': nrm((L, RWKV_W_LORA, G), 0.5 * RWKV_W_LORA ** -0.5),
        'rwkv_a0': nrm((L, G), 0.5),
        'rwkv_a_up': nrm((L, RWKV_A_LORA, G), 0.5 * RWKV_A_LORA ** -0.5),
        'rwkv_g_up': nrm((L, RWKV_G_LORA, G), RWKV_G_LORA ** -0.5),
        'rwkv_k_k': gain((L, G)),
        'rwkv_k_a': gain((L, G)),
        'rwkv_r_k': nrm((L, G), 0.1),
        'rwkv_ln_w': gain((L, G)),
        'rwkv_ln_b': nrm((L, G), 0.02),
        'rwkv_v0': nrm((L - 1, G), 0.5),
        'rwkv_v_down': nrm((L - 1, G, RWKV_V_LORA), G ** -0.5),
        'rwkv_v_up': nrm((L - 1, RWKV_V_LORA, G), 0.5 * RWKV_V_LORA ** -0.5),
    }


def reference(x, w_in, w_out, norm_pre_mix, norm_post_mix, norm_pre_ffn, norm_post_ffn,
              w_ffn_gate, w_ffn_up, w_ffn_down, hgrn_lb_logits, hgrn_norm_w,
              mlstm_conv_w, mlstm_conv_b, mlstm_i_bias, mlstm_f_bias, mlstm_norm_w,
              rwkv_mu, rwkv_w0, rwkv_w_up, rwkv_a0, rwkv_a_up, rwkv_g_up,
              rwkv_k_k, rwkv_k_a, rwkv_r_k, rwkv_ln_w, rwkv_ln_b,
              rwkv_v0, rwkv_v_down, rwkv_v_up):
    lb_cum = jnp.cumsum(jax.nn.softmax(hgrn_lb_logits.astype(jnp.float32), axis=0), axis=0)
    lower_bounds = lb_cum - lb_cum[0]
    v_first = None
    h = x
    for l in range(DEPTH):
        u = rms_norm(h, norm_pre_mix[l])
        p = (u @ w_in[l]).astype(jnp.float32)
        p_ret, p_hgrn, p_ml, p_rw = split_cols(p, GROUP_COLS)
        o_ret = retention_group(p_ret)
        o_hgrn = hgrn2_group(p_hgrn, lower_bounds[l], hgrn_norm_w[l])
        o_ml = mlstm_group(p_ml, mlstm_conv_w[l], mlstm_conv_b[l], mlstm_i_bias[l],
                           mlstm_f_bias[l], mlstm_norm_w[l])
        vmix = None if l == 0 else (rwkv_v0[l - 1], rwkv_v_down[l - 1], rwkv_v_up[l - 1])
        o_rw, v_first = rwkv7_group(p_rw, v_first, rwkv_mu[l], rwkv_w0[l], rwkv_w_up[l],
                                    rwkv_a0[l], rwkv_a_up[l], rwkv_g_up[l], rwkv_k_k[l],
                                    rwkv_k_a[l], rwkv_r_k[l], rwkv_ln_w[l], rwkv_ln_b[l], vmix)
        mixed = jnp.concatenate([o_ret, o_hgrn, o_ml, o_rw], axis=-1).astype(h.dtype) @ w_out[l]
        h = h + rms_norm(mixed, norm_post_mix[l])
        u = rms_norm(h, norm_pre_ffn[l])
        ffn = (jax.nn.silu(u @ w_ffn_gate[l]) * (u @ w_ffn_up[l])) @ w_ffn_down[l]
        h = h + rms_norm(ffn, norm_post_ffn[l])
    return h
```

```python
import functools

import numpy as np
import jax
import jax.numpy as jnp
from jax import lax
from jax.experimental import pallas as pl
from jax.experimental.pallas import tpu as pltpu

F32 = jnp.float32
BF16 = jnp.bfloat16

HEAD_DIM = 64
N_HEADS = 8
GROUP = N_HEADS * HEAD_DIM
LANES = 128
N_PAIRS = GROUP // LANES
NORM_EPS = 1e-6
ROPE_BASE = 10000.0
RWKV_DECAY_SCALE = 0.6065306597126334
RWKV_LN_EPS = 64e-5
RWKV_LORA_COLS = 256
CONV_WIDTH = 4
NEG_BIG = -1e30

RET_C = 128
HGRN_TS = 128
HGRN_C = 16
MLSTM_C = 128
RWKV_C = 64

COL_RET = 0
COL_HGRN = 4 * GROUP
COL_ML = 8 * GROUP
COL_RW = 12 * GROUP
COL_RW_LORA = 15 * GROUP
COL_ML_GATES = 15 * GROUP + RWKV_LORA_COLS
N_IN_PAD = 16 * GROUP

VMEM_LIMIT = 56 * 1024 * 1024


def _cparams(n_axes):
    return pltpu.CompilerParams(dimension_semantics=("arbitrary",) * n_axes,
                                vmem_limit_bytes=VMEM_LIMIT)


def _dot(a, b):
    return jnp.dot(a.astype(BF16), b.astype(BF16), preferred_element_type=F32)


def _dot_nt(a, b):
    return lax.dot_general(a.astype(BF16), b.astype(BF16), (((1,), (1,)), ((), ())),
                           preferred_element_type=F32)


def _dot_tn(a, b):
    return lax.dot_general(a.astype(BF16), b.astype(BF16), (((0,), (0,)), ((), ())),
                           preferred_element_type=F32)


def _split3(x):
    hi = x.astype(BF16)
    r1 = x - hi.astype(F32)
    mid = r1.astype(BF16)
    lo = (r1 - mid.astype(F32)).astype(BF16)
    return hi, mid, lo


def _dot_x3(x, sel):
    hi, mid, lo = _split3(x)
    d = lambda t: jnp.dot(t, sel, preferred_element_type=F32)
    return d(hi) + d(mid) + d(lo)


def _dot_3x(sel, x):
    hi, mid, lo = _split3(x)
    d = lambda t: jnp.dot(sel, t, preferred_element_type=F32)
    return d(hi) + d(mid) + d(lo)


def _bsum(x, ones_bd):
    hi = x.astype(BF16)
    lo = (x - hi.astype(F32)).astype(BF16)
    return (jnp.dot(hi, ones_bd, preferred_element_type=F32)
            + jnp.dot(lo, ones_bd, preferred_element_type=F32))


def _sigmoid(x):
    return 1.0 / (1.0 + jnp.exp(-x))


def _silu(x):
    return x * _sigmoid(x)


def _log_sigmoid(x):
    return jnp.minimum(x, 0.0) - jnp.log1p(jnp.exp(-jnp.abs(x)))


def _rms_rows(x, w):
    return x * lax.rsqrt(jnp.mean(x * x, axis=-1, keepdims=True) + NORM_EPS) * w


def _shift_rows(x, prev8, k):
    rolled = pltpu.roll(x, k, axis=0)
    row8 = lax.broadcasted_iota(jnp.int32, (8, x.shape[1]), 0)
    top = jnp.where(row8 < k, pltpu.roll(prev8, k, axis=0), rolled[0:8])
    return jnp.concatenate([top, rolled[8:]], axis=0)


def _rms_kernel(x_ref, w_ref, u_ref):
    u_ref[...] = _rms_rows(x_ref[...], w_ref[...]).astype(BF16)


def _prenorm(x2d, w, tm=512):
    t, d = x2d.shape
    return pl.pallas_call(
        _rms_kernel,
        grid=(t // tm,),
        in_specs=[pl.BlockSpec((tm, d), lambda i: (i, 0)),
                  pl.BlockSpec((1, d), lambda i: (0, 0))],
        out_specs=pl.BlockSpec((tm, d), lambda i: (i, 0)),
        out_shape=jax.ShapeDtypeStruct((t, d), BF16),
        compiler_params=_cparams(1),
        name="prenorm",
    )(x2d, w.reshape(1, d))


def _mm_kernel(u_ref, w_ref, o_ref):
    o_ref[...] = jnp.dot(u_ref[...], w_ref[...], preferred_element_type=F32)


def _in_proj(u, w, tm=1024, tn=1024):
    t, d = u.shape
    n = w.shape[1]
    tm = min(tm, t)
    return pl.pallas_call(
        _mm_kernel,
        grid=(t // tm, n // tn),
        in_specs=[pl.BlockSpec((tm, d), lambda i, j: (i, 0)),
                  pl.BlockSpec((d, tn), lambda i, j: (0, j))],
        out_specs=pl.BlockSpec((tm, tn), lambda i, j: (i, j)),
        out_shape=jax.ShapeDtypeStruct((t, n), F32),
        compiler_params=_cparams(2),
        name="in_proj",
    )(u, w)


def _outproj_kernel(o0_ref, o1_ref, o2_ref, o3_ref, w_ref, h_ref, wpost_ref, wpre_ref,
                    hout_ref, u_ref):
    acc = jnp.dot(o0_ref[...], w_ref[0 * GROUP:1 * GROUP, :], preferred_element_type=F32)
    acc += jnp.dot(o1_ref[...], w_ref[1 * GROUP:2 * GROUP, :], preferred_element_type=F32)
    acc += jnp.dot(o2_ref[...], w_ref[2 * GROUP:3 * GROUP, :], preferred_element_type=F32)
    acc += jnp.dot(o3_ref[...], w_ref[3 * GROUP:4 * GROUP, :], preferred_element_type=F32)
    h = h_ref[...] + _rms_rows(acc, wpost_ref[...])
    hout_ref[...] = h
    u_ref[...] = _rms_rows(h, wpre_ref[...]).astype(BF16)


def _out_proj(outs, w, h, w_post, w_pre, tm=512):
    t, d = h.shape
    tm = min(tm, t)
    row = lambda i: (i, 0)
    fixed = lambda i: (0, 0)
    return pl.pallas_call(
        _outproj_kernel,
        grid=(t // tm,),
        in_specs=[pl.BlockSpec((tm, GROUP), row)] * 4 + [
            pl.BlockSpec((d, d), fixed),
            pl.BlockSpec((tm, d), row),
            pl.BlockSpec((1, d), fixed),
            pl.BlockSpec((1, d), fixed)],
        out_specs=[pl.BlockSpec((tm, d), row), pl.BlockSpec((tm, d), row)],
        out_shape=[jax.ShapeDtypeStruct((t, d), F32), jax.ShapeDtypeStruct((t, d), BF16)],
        compiler_params=_cparams(1),
        name="out_proj",
    )(*outs, w, h, w_post.reshape(1, d), w_pre.reshape(1, d))


def _ffn_kernel(u_ref, h_ref, wg_ref, wu_ref, wd_ref, wpost_ref, wpre_ref, hout_ref, u_next_ref,
                acc_ref):
    j = pl.program_id(1)
    u = u_ref[...]
    g = jnp.dot(u, wg_ref[...], preferred_element_type=F32)
    up = jnp.dot(u, wu_ref[...], preferred_element_type=F32)
    part = jnp.dot((_silu(g) * up).astype(BF16), wd_ref[...], preferred_element_type=F32)

    @pl.when(j == 0)
    def _():
        acc_ref[...] = part

    @pl.when(j > 0)
    def _():
        acc_ref[...] += part

    @pl.when(j == pl.num_programs(1) - 1)
    def _():
        h = h_ref[...] + _rms_rows(acc_ref[...], wpost_ref[...])
        hout_ref[...] = h
        u_next_ref[...] = _rms_rows(h, wpre_ref[...]).astype(BF16)


def _ffn(u, h, wg, wu, wd, w_post, w_pre_next, tm=512, tf=512):
    t, d = h.shape
    f = wg.shape[1]
    tm = min(tm, t)
    row = lambda i, j: (i, 0)
    fixed = lambda i, j: (0, 0)
    return pl.pallas_call(
        _ffn_kernel,
        grid=(t // tm, f // tf),
        in_specs=[pl.BlockSpec((tm, d), row),
                  pl.BlockSpec((tm, d), row),
                  pl.BlockSpec((d, tf), lambda i, j: (0, j)),
                  pl.BlockSpec((d, tf), lambda i, j: (0, j)),
                  pl.BlockSpec((tf, d), lambda i, j: (j, 0)),
                  pl.BlockSpec((1, d), fixed),
                  pl.BlockSpec((1, d), fixed)],
        out_specs=[pl.BlockSpec((tm, d), row), pl.BlockSpec((tm, d), row)],
        out_shape=[jax.ShapeDtypeStruct((t, d), F32), jax.ShapeDtypeStruct((t, d), BF16)],
        scratch_shapes=[pltpu.VMEM((tm, d), F32)],
        compiler_params=_cparams(2),
        name="ffn",
    )(u, h, wg, wu, wd, w_post.reshape(1, d), w_pre_next.reshape(1, d))


def _block_ones():
    i = np.arange(LANES)
    return (i[:, None] // HEAD_DIM == i[None, :] // HEAD_DIM).astype(np.float32)


def _tri_incl(c, block=None):
    i = np.arange(c)
    m = i[None, :] <= i[:, None]
    if block is not None:
        m = m & (i[None, :] // block == i[:, None] // block)
    return m.astype(np.float32)


def _rope_tables(seq):
    half = HEAD_DIM // 2
    lane = np.arange(LANES)
    inv_freq = ROPE_BASE ** (-jnp.arange(half, dtype=F32) / half)
    ang = jnp.arange(seq, dtype=F32)[:, None] * inv_freq[None, :]
    cos, sin = jnp.cos(ang), jnp.sin(ang)
    fidx = (lane % HEAD_DIM) % half
    sign = np.where((lane % HEAD_DIM) < half, -1.0, 1.0).astype(np.float32)
    return cos[:, fidx], sin[:, fidx] * sign


def _retention_tables():
    c = RET_C
    log_gamma = jnp.log1p(-jnp.exp2(-5.0 - jnp.arange(N_HEADS, dtype=F32)))
    idx = jnp.arange(c, dtype=F32)
    rel = idx[:, None] - idx[None, :]
    intra = jnp.where(rel >= 0, jnp.exp(log_gamma[:, None, None] * jnp.maximum(rel, 0.0)), 0.0)
    q_dec = jnp.exp(log_gamma[:, None] * (idx + 1.0))
    k_dec = jnp.exp(log_gamma[:, None] * (c - 1.0 - idx))
    chunk_dec = jnp.exp(log_gamma * c)
    per_lane = lambda t: jnp.repeat(t.T, HEAD_DIM, axis=1)
    bd = jnp.asarray(_block_ones())
    lane_head = np.arange(LANES) // HEAD_DIM
    sdec = jnp.stack([bd * chunk_dec[2 * j + lane_head][None, :] for j in range(N_PAIRS)])
    return intra, per_lane(q_dec), per_lane(k_dec), sdec


def _ret_kernel(p_ref, cos_ref, sin_ref, intra_ref, qdec_ref, kdec_ref, sdec_ref, bd_ref,
                ones_ref, o_ref, st_ref):
    c = RET_C

    @pl.when(pl.program_id(1) == 0)
    def _():
        st_ref[...] = jnp.zeros_like(st_ref)

    cos = cos_ref[...]
    sin = sin_ref[...]
    lane = lax.broadcasted_iota(jnp.int32, (c, LANES), 1)
    first_half = (lane & (HEAD_DIM - 1)) < HEAD_DIM // 2
    lo = lane < HEAD_DIM
    ones_bd = ones_ref[...]
    bd = bd_ref[...]

    def rot(t):
        swapped = jnp.where(first_half, pltpu.roll(t, LANES - HEAD_DIM // 2, axis=1),
                            pltpu.roll(t, HEAD_DIM // 2, axis=1))
        return t * cos + swapped * sin

    for j in range(N_PAIRS):
        sl = slice(j * LANES, (j + 1) * LANES)
        q = rot(p_ref[0, :, j * LANES:(j + 1) * LANES])
        k = rot(p_ref[0, :, GROUP + j * LANES:GROUP + (j + 1) * LANES]) * HEAD_DIM ** -0.5
        v = p_ref[0, :, 2 * GROUP + j * LANES:2 * GROUP + (j + 1) * LANES]
        g = p_ref[0, :, 3 * GROUP + j * LANES:3 * GROUP + (j + 1) * LANES]
        st = st_ref[j]
        kb = k.astype(BF16)
        vb = v.astype(BF16)
        s_a = _dot_nt(jnp.where(lo, q, 0.0), kb) * intra_ref[2 * j]
        s_b = _dot_nt(jnp.where(lo, 0.0, q), kb) * intra_ref[2 * j + 1]
        out = jnp.where(lo, _dot(s_a, vb), _dot(s_b, vb)) + _dot(q * qdec_ref[:, sl], st)
        st_ref[j] = sdec_ref[j] * st + bd * _dot_tn(k * kdec_ref[:, sl], vb)
        ms = _bsum(out * out, ones_bd) * (1.0 / HEAD_DIM)
        o_ref[0, :, sl] = (out * lax.rsqrt(ms + NORM_EPS) * _silu(g)).astype(BF16)


def _retention(p3, tables):
    b, s, _ = p3.shape
    c = RET_C
    cos, sin, intra, qdec, kdec, sdec, bd, ones_bd = tables
    fixed2 = lambda i, t: (0, 0)
    fixed3 = lambda i, t: (0, 0, 0)
    return pl.pallas_call(
        _ret_kernel,
        grid=(b, s // c),
        in_specs=[pl.BlockSpec((1, c, 4 * GROUP), lambda i, t: (i, t, COL_RET // (4 * GROUP))),
                  pl.BlockSpec((c, LANES), lambda i, t: (t, 0)),
                  pl.BlockSpec((c, LANES), lambda i, t: (t, 0)),
                  pl.BlockSpec((N_HEADS, c, c), fixed3),
                  pl.BlockSpec((c, GROUP), fixed2),
                  pl.BlockSpec((c, GROUP), fixed2),
                  pl.BlockSpec((N_PAIRS, LANES, LANES), fixed3),
                  pl.BlockSpec((LANES, LANES), fixed2),
                  pl.BlockSpec((LANES, LANES), fixed2)],
        out_specs=pl.BlockSpec((1, c, GROUP), lambda i, t: (i, t, 0)),
        out_shape=jax.ShapeDtypeStruct((b, s, GROUP), BF16),
        scratch_shapes=[pltpu.VMEM((N_PAIRS, LANES, LANES), F32)],
        compiler_params=_cparams(2),
        name="retention",
    )(p3, cos, sin, intra, qdec, kdec, sdec, bd, ones_bd)


def _hgrn_kernel(p_ref, loglb_ref, log1mlb_ref, onemlb_ref, normw_ref, tri_ref, bd_ref, ones_ref,
                 o_ref, q_s, k_s, c_s, st_ref):
    ts, c = HGRN_TS, HGRN_C

    @pl.when(pl.program_id(1) == 0)
    def _():
        st_ref[...] = jnp.zeros_like(st_ref)

    pf = p_ref[0, :, GROUP:2 * GROUP]
    q_s[...] = _silu(p_ref[0, :, 0:GROUP]) * HEAD_DIM ** -0.5
    k_s[...] = onemlb_ref[...] * _sigmoid(-pf)
    a = loglb_ref[...]
    bb = log1mlb_ref[...] + _log_sigmoid(pf)
    log_f = jnp.maximum(a, bb) + jnp.log1p(jnp.exp(-jnp.abs(a - bb)))
    c_s[...] = _dot_3x(tri_ref[...], log_f)

    ones_bd = ones_ref[...]
    bd = bd_ref[...]
    srow = lax.broadcasted_iota(jnp.int32, (c, LANES), 0)

    def chunk(ci, carry):
        r0 = pl.multiple_of(ci * c, c)
        rows = pl.ds(r0, c)
        for j in range(N_PAIRS):
            sl = slice(j * LANES, (j + 1) * LANES)
            qc = q_s[rows, sl]
            kc = k_s[rows, sl]
            cc = c_s[rows, sl]
            vc = p_ref[0, rows, 2 * GROUP + j * LANES:2 * GROUP + (j + 1) * LANES]
            gc = p_ref[0, rows, 3 * GROUP + j * LANES:3 * GROUP + (j + 1) * LANES]
            zs = []
            for t in range(c):
                w = jnp.where(srow <= t, jnp.exp(jnp.minimum(cc[t:t + 1, :] - cc, 0.0)), 0.0)
                zs.append(qc[t:t + 1, :] * kc * w)
            scores = _dot(jnp.concatenate(zs, axis=0), ones_bd)
            intra = jnp.concatenate(
                [jnp.sum(scores[t * c:(t + 1) * c] * vc, axis=0, keepdims=True) for t in range(c)],
                axis=0)
            st_t = st_ref[j]
            c_end = cc[c - 1:c, :]
            out = intra + _dot_nt(qc * jnp.exp(cc), st_t)
            st_ref[j] = st_t * jnp.exp(c_end) + bd * _dot_tn(vc, kc * jnp.exp(c_end - cc))
            ms = _bsum(out * out, ones_bd) * (1.0 / HEAD_DIM)
            y = out * lax.rsqrt(ms + NORM_EPS) * normw_ref[:, sl]
            o_ref[0, rows, sl] = (y * _sigmoid(gc)).astype(BF16)
        return carry

    lax.fori_loop(0, ts // c, chunk, 0)


def _hgrn(p3, lb, norm_w, consts):
    b, s, _ = p3.shape
    ts = HGRN_TS
    tri, bd, ones_bd = consts
    fixed2 = lambda i, t: (0, 0)
    row = lambda v: v.reshape(1, GROUP)
    return pl.pallas_call(
        _hgrn_kernel,
        grid=(b, s // ts),
        in_specs=[pl.BlockSpec((1, ts, 4 * GROUP), lambda i, t: (i, t, COL_HGRN // (4 * GROUP))),
                  pl.BlockSpec((1, GROUP), fixed2),
                  pl.BlockSpec((1, GROUP), fixed2),
                  pl.BlockSpec((1, GROUP), fixed2),
                  pl.BlockSpec((1, GROUP), fixed2),
                  pl.BlockSpec((ts, ts), fixed2),
                  pl.BlockSpec((LANES, LANES), fixed2),
                  pl.BlockSpec((LANES, LANES), fixed2)],
        out_specs=pl.BlockSpec((1, ts, GROUP), lambda i, t: (i, t, 0)),
        out_shape=jax.ShapeDtypeStruct((b, s, GROUP), BF16),
        scratch_shapes=[pltpu.VMEM((ts, GROUP), F32)] * 3
        + [pltpu.VMEM((N_PAIRS, LANES, LANES), F32)],
        compiler_params=_cparams(2),
        name="hgrn2",
    )(p3, row(jnp.log(lb)), row(jnp.log1p(-lb)), row(1.0 - lb), row(norm_w), tri, bd, ones_bd)


def _mlstm_tables():
    c = MLSTM_C
    lane = np.arange(LANES)
    esel = np.zeros((LANES, N_HEADS * c), np.float32)
    for h in range(N_HEADS):
        esel[N_HEADS + h, h * c:(h + 1) * c] = 1.0
    epf = np.zeros((LANES, GROUP), np.float32)
    epi = np.zeros((LANES, GROUP), np.float32)
    for h in range(N_HEADS):
        epf[N_HEADS + h, h * HEAD_DIM:(h + 1) * HEAD_DIM] = 1.0
        epi[h, h * HEAD_DIM:(h + 1) * HEAD_DIM] = 1.0
    del lane
    return (jnp.asarray(_tri_incl(c), BF16), jnp.asarray(esel, BF16), jnp.asarray(epf, BF16),
            jnp.asarray(epi, BF16))


def _mlstm_kernel(pm_ref, pg_ref, convw_ref, convb_ref, gbias_ref, normw_ref, tri_ref, esel_ref,
                  epf_ref, epi_ref, bd_ref, ones_ref, o_ref,
                  prev_ref, cm_ref, n_ref, m_ref, mgl_ref):
    c = MLSTM_C

    @pl.when(pl.program_id(1) == 0)
    def _():
        prev_ref[...] = jnp.zeros_like(prev_ref)
        cm_ref[...] = jnp.zeros_like(cm_ref)
        n_ref[...] = jnp.zeros_like(n_ref)
        m_ref[...] = jnp.zeros_like(m_ref)
        mgl_ref[...] = jnp.zeros_like(mgl_ref)

    ones_bd = ones_ref[...]
    bd = bd_ref[...]

    x = pm_ref[0, :, 0:2 * GROUP]
    prev = prev_ref[...]
    acc = convb_ref[...] + x * convw_ref[CONV_WIDTH - 1:CONV_WIDTH, :]
    for sh in range(1, CONV_WIDTH):
        acc = acc + _shift_rows(x, prev, sh) * convw_ref[CONV_WIDTH - 1 - sh:CONV_WIDTH - sh, :]
    prev_ref[...] = x[c - 8:c]
    qk = _silu(acc)
    q_all = qk[:, 0:GROUP]
    k_all = qk[:, GROUP:2 * GROUP] * HEAD_DIM ** -0.5

    lane = lax.broadcasted_iota(jnp.int32, (c, LANES), 1)
    gb = pg_ref[0] + gbias_ref[...]
    gl = jnp.where(lane < N_HEADS, gb, jnp.where(lane < 2 * N_HEADS, _log_sigmoid(gb), 0.0))
    cum = _dot_3x(tri_ref[...], gl)
    mgl = mgl_ref[...]
    xm = cum + mgl
    colb = _dot_x3(xm, esel_ref[...])
    xm_t = xm.T
    gl_t = gl.T

    row = lax.broadcasted_iota(jnp.int32, (c, c), 0)
    col = lax.broadcasted_iota(jnp.int32, (c, c), 1)
    causal = col <= row
    dmat, w_inter, e_neg_m = [], [], []
    for h in range(N_HEADS):
        log_inter = colb[:, h * c:(h + 1) * c]
        logd = jnp.where(
            causal, log_inter + (gl_t[h:h + 1, :] - xm_t[N_HEADS + h:N_HEADS + h + 1, :]), NEG_BIG)
        m_t = jnp.maximum(jnp.max(logd, axis=1, keepdims=True), log_inter)
        dmat.append(jnp.exp(logd - m_t))
        w_inter.append(jnp.exp(log_inter - m_t))
        e_neg_m.append(jnp.exp(-m_t))

    cumcol = _dot_x3(cum, epf_ref[...])
    icol = _dot_x3(gl, epi_ref[...])
    c_end = cumcol[c - 1:c, :]
    m_old = m_ref[...]
    log_w = c_end - cumcol + icol
    m_new = jnp.maximum(c_end + m_old, jnp.max(log_w, axis=0, keepdims=True))
    decay = jnp.exp(c_end + m_old - m_new)
    kw_all = k_all * jnp.exp(log_w - m_new)
    n_old = n_ref[...]
    n_ref[...] = decay * n_old + jnp.sum(kw_all, axis=0, keepdims=True)
    m_ref[...] = m_new
    c_end_gl = cum[c - 1:c, :]
    log_w_gl = c_end_gl - cum + pltpu.roll(gl, N_HEADS, axis=1)
    m_new_gl = jnp.maximum(c_end_gl + mgl, jnp.max(log_w_gl, axis=0, keepdims=True))
    lane1 = lax.broadcasted_iota(jnp.int32, (1, LANES), 1)
    mgl_ref[...] = jnp.where((lane1 >= N_HEADS) & (lane1 < 2 * N_HEADS), m_new_gl, 0.0)

    lo = lane < HEAD_DIM
    for j in range(N_PAIRS):
        sl = slice(j * LANES, (j + 1) * LANES)
        qj = q_all[:, sl]
        kj = k_all[:, sl]
        vj = pm_ref[0, :, 2 * GROUP + j * LANES:2 * GROUP + (j + 1) * LANES]
        oj = pm_ref[0, :, 3 * GROUP + j * LANES:3 * GROUP + (j + 1) * LANES]
        kb = kj.astype(BF16)
        vb = vj.astype(BF16)
        s_a = _dot_nt(jnp.where(lo, qj, 0.0), kb) * dmat[2 * j]
        s_b = _dot_nt(jnp.where(lo, 0.0, qj), kb) * dmat[2 * j + 1]
        cm = cm_ref[j]
        wi = jnp.where(lo, w_inter[2 * j], w_inter[2 * j + 1])
        num = jnp.where(lo, _dot(s_a, vb), _dot(s_b, vb)) + wi * _dot(qj, cm)
        den = (jnp.where(lo, jnp.sum(s_a, axis=1, keepdims=True),
                         jnp.sum(s_b, axis=1, keepdims=True))
               + wi * _bsum(qj * n_old[:, sl], ones_bd))
        floor = jnp.where(lo, e_neg_m[2 * j], e_neg_m[2 * j + 1])
        hh = num / jnp.maximum(jnp.abs(den), floor)
        cm_ref[j] = cm * decay[:, sl] + bd * _dot_tn(kw_all[:, sl], vb)
        ms = _bsum(hh * hh, ones_bd) * (1.0 / HEAD_DIM)
        y = hh * lax.rsqrt(ms + NORM_EPS) * normw_ref[:, sl]
        o_ref[0, :, sl] = (_sigmoid(oj) * y).astype(BF16)


def _mlstm(p3, conv_w, conv_b, i_bias, f_bias, norm_w, consts):
    b, s, _ = p3.shape
    c = MLSTM_C
    tri, esel, epf, epi, bd, ones_bd = consts
    gbias = jnp.zeros((1, LANES), F32).at[0, 0:N_HEADS].set(i_bias).at[0, N_HEADS:2 * N_HEADS].set(f_bias)
    fixed2 = lambda i, t: (0, 0)
    return pl.pallas_call(
        _mlstm_kernel,
        grid=(b, s // c),
        in_specs=[pl.BlockSpec((1, c, 4 * GROUP), lambda i, t: (i, t, COL_ML // (4 * GROUP))),
                  pl.BlockSpec((1, c, LANES), lambda i, t: (i, t, COL_ML_GATES // LANES)),
                  pl.BlockSpec((CONV_WIDTH, 2 * GROUP), fixed2),
                  pl.BlockSpec((1, 2 * GROUP), fixed2),
                  pl.BlockSpec((1, LANES), fixed2),
                  pl.BlockSpec((1, GROUP), fixed2),
                  pl.BlockSpec((c, c), fixed2),
                  pl.BlockSpec((LANES, N_HEADS * c), fixed2),
                  pl.BlockSpec((LANES, GROUP), fixed2),
                  pl.BlockSpec((LANES, GROUP), fixed2),
                  pl.BlockSpec((LANES, LANES), fixed2),
                  pl.BlockSpec((LANES, LANES), fixed2)],
        out_specs=pl.BlockSpec((1, c, GROUP), lambda i, t: (i, t, 0)),
        out_shape=jax.ShapeDtypeStruct((b, s, GROUP), BF16),
        scratch_shapes=[pltpu.VMEM((8, 2 * GROUP), F32),
                        pltpu.VMEM((N_PAIRS, LANES, LANES), F32),
                        pltpu.VMEM((1, GROUP), F32),
                        pltpu.VMEM((1, GROUP), F32),
                        pltpu.VMEM((1, LANES), F32)],
        compiler_params=_cparams(2),
        name="mlstm",
    )(p3, p3, conv_w, conv_b.reshape(1, -1), gbias, norm_w.reshape(1, GROUP), tri, esel, epf, epi,
      bd, ones_bd)


def _stack2(x, lo):
    return jnp.concatenate([jnp.where(lo, x, 0.0), jnp.where(lo, 0.0, x)], axis=0)


def _rwkv_kernel(first_layer, *refs):
    c = RWKV_C
    if first_layer:
        (prkv_ref, plora_ref, mu_rkv_ref, mu_lora_ref, w0_ref, wup_ref, a0_ref, aup_ref, gup_ref,
         kk_ref, ka_ref, rk_ref, lnw_ref, lnb_ref, tri_ref, ones_ref,
         o_ref, vfirst_out_ref, prev_rkv_ref, prev_lora_ref, st_ref) = refs
    else:
        (prkv_ref, plora_ref, vfirst_ref, mu_rkv_ref, mu_lora_ref, w0_ref, wup_ref, a0_ref, aup_ref,
         gup_ref, kk_ref, ka_ref, rk_ref, lnw_ref, lnb_ref, v0_ref, vdown_ref, vup_ref, tri_ref,
         ones_ref, o_ref, prev_rkv_ref, prev_lora_ref, st_ref) = refs

    @pl.when(pl.program_id(1) == 0)
    def _():
        prev_rkv_ref[...] = jnp.zeros_like(prev_rkv_ref)
        prev_lora_ref[...] = jnp.zeros_like(prev_lora_ref)
        st_ref[...] = jnp.zeros_like(st_ref)

    ones_bd = ones_ref[...]

    def token_mix(x_ref, prev_ref, mu_ref):
        x = x_ref[0]
        shifted = _shift_rows(x, prev_ref[...], 1)
        prev_ref[...] = x[c - 8:c]
        return x + (shifted - x) * mu_ref[...]

    rkv = token_mix(prkv_ref, prev_rkv_ref, mu_rkv_ref)
    lora = token_mix(plora_ref, prev_lora_ref, mu_lora_ref)
    r = rkv[:, 0:GROUP]
    k = rkv[:, GROUP:2 * GROUP]
    v = rkv[:, 2 * GROUP:3 * GROUP]
    wa = lora[:, 0:LANES]
    log_w = -RWKV_DECAY_SCALE * _sigmoid(w0_ref[...] + _dot(jnp.tanh(wa), wup_ref[...]))
    a = _sigmoid(a0_ref[...] + _dot(wa, aup_ref[...]))
    g = _dot(_sigmoid(lora[:, LANES:2 * LANES]), gup_ref[...])
    if first_layer:
        vfirst_out_ref[0] = v
    else:
        mix = _sigmoid(v0_ref[...] + _dot(_dot(v, vdown_ref[...]), vup_ref[...]))
        v = v + (vfirst_ref[0] - v) * mix

    kk_raw = k * kk_ref[...]
    k2 = k * (1.0 + (a - 1.0) * ka_ref[...])
    cum = _dot_3x(tri_ref[...], log_w)
    c_end = cum[c - 1:c, :]
    e_pos = jnp.exp(cum)
    e_neg = jnp.exp(-cum)
    e_end = jnp.exp(c_end - cum)
    rk_term = r * k2 * rk_ref[...]

    lane = lax.broadcasted_iota(jnp.int32, (c, LANES), 1)
    lo = lane < HEAD_DIM
    row = lax.broadcasted_iota(jnp.int32, (2 * c, 2 * c), 0)
    col = lax.broadcasted_iota(jnp.int32, (2 * c, 2 * c), 1)
    strict = col < row
    incl = col <= row
    eye = (col == row).astype(F32)

    for j in range(N_PAIRS):
        sl = slice(j * LANES, (j + 1) * LANES)
        kkr = kk_raw[:, sl]
        norm = jnp.sqrt(_bsum(kkr * kkr, ones_bd))
        kk = kkr / jnp.maximum(norm, 1e-12)
        bvec = a[:, sl] * kk
        r_t = _stack2(r[:, sl] * e_pos[:, sl], lo)
        kk_t = _stack2(kk * jnp.exp(cum[:, sl] - log_w[:, sl]), lo)
        k_h = _stack2(k2[:, sl] * e_neg[:, sl], lo)
        b_h = _stack2(bvec * e_neg[:, sl], lo)
        k_e = _stack2(k2[:, sl] * e_end[:, sl], lo)
        b_e = _stack2(bvec * e_end[:, sl], lo)
        v2 = _stack2(v[:, sl], lo)
        ht = st_ref[j]

        lmat = jnp.where(strict, _dot_nt(kk_t, b_h), 0.0)
        a_kk = jnp.where(strict, _dot_nt(kk_t, k_h), 0.0)
        a_rk = jnp.where(incl, _dot_nt(r_t, k_h), 0.0)
        a_rb = jnp.where(incl, _dot_nt(r_t, b_h), 0.0)
        tinv = eye - lmat
        lpow = lmat
        for _ in range(5):
            lpow = _dot(lpow, lpow)
            tinv = tinv + _dot(tinv, lpow)
        rhs = _dot_nt(kk_t, ht) + _dot(a_kk, v2)
        u2 = -_dot(tinv, rhs)
        y2 = _dot_nt(r_t, ht) + _dot(a_rk, v2) + _dot(a_rb, u2)
        y = y2[0:c] + y2[c:2 * c]
        st_ref[j] = ht * jnp.exp(c_end[:, sl]) + _dot_tn(v2, k_e) + _dot_tn(u2, b_e)

        mean = _bsum(y, ones_bd) * (1.0 / HEAD_DIM)
        yc = y - mean
        var = _bsum(yc * yc, ones_bd) * (1.0 / HEAD_DIM)
        yn = yc * lax.rsqrt(var + RWKV_LN_EPS) * lnw_ref[:, sl] + lnb_ref[:, sl]
        bonus = _bsum(rk_term[:, sl], ones_bd) * v[:, sl]
        o_ref[0, :, sl] = ((yn + bonus) * g[:, sl]).astype(BF16)


def _rwkv(p3, v_first, prm, consts):
    b, s, _ = p3.shape
    c = RWKV_C
    tri, ones_bd = consts
    first_layer = v_first is None
    fixed2 = lambda i, t: (0, 0)
    row = lambda n: pl.BlockSpec((1, n), fixed2)
    full = lambda a: pl.BlockSpec(a.shape, fixed2)
    seq_spec = pl.BlockSpec((1, c, GROUP), lambda i, t: (i, t, 0))
    in_specs = [pl.BlockSpec((1, c, 3 * GROUP), lambda i, t: (i, t, COL_RW // (3 * GROUP))),
                pl.BlockSpec((1, c, RWKV_LORA_COLS), lambda i, t: (i, t, COL_RW_LORA // RWKV_LORA_COLS))]
    args = [p3, p3]
    if not first_layer:
        in_specs.append(seq_spec)
        args.append(v_first)
    in_specs += [row(3 * GROUP), row(RWKV_LORA_COLS), row(GROUP), full(prm["w_up"]), row(GROUP),
                 full(prm["a_up"]), full(prm["g_up"])] + [row(GROUP)] * 5
    args += [prm["mu_rkv"], prm["mu_lora"], prm["w0"], prm["w_up"], prm["a0"], prm["a_up"],
             prm["g_up"], prm["k_k"], prm["k_a"], prm["r_k"], prm["ln_w"], prm["ln_b"]]
    if not first_layer:
        in_specs += [row(GROUP), full(prm["v_down"]), full(prm["v_up"])]
        args += [prm["v0"], prm["v_down"], prm["v_up"]]
    in_specs += [pl.BlockSpec((c, c), fixed2), pl.BlockSpec((LANES, LANES), fixed2)]
    args += [tri, ones_bd]
    o_shape = jax.ShapeDtypeStruct((b, s, GROUP), BF16)
    if first_layer:
        out_specs = [seq_spec, seq_spec]
        out_shape = [o_shape, jax.ShapeDtypeStruct((b, s, GROUP), F32)]
    else:
        out_specs = seq_spec
        out_shape = o_shape
    res = pl.pallas_call(
        functools.partial(_rwkv_kernel, first_layer),
        grid=(b, s // c),
        in_specs=in_specs,
        out_specs=out_specs,
        out_shape=out_shape,
        scratch_shapes=[pltpu.VMEM((8, 3 * GROUP), F32),
                        pltpu.VMEM((8, RWKV_LORA_COLS), F32),
                        pltpu.VMEM((N_PAIRS, LANES, LANES), F32)],
        compiler_params=_cparams(2),
        name="rwkv7",
    )(*args)
    if first_layer:
        return res[0], res[1]
    return res, v_first


def _pad_cols(w, n):
    return jnp.pad(w, ((0, 0), (0, n - w.shape[1])))


def _pad_rows(w, n):
    return jnp.pad(w, ((0, n - w.shape[0]), (0, 0)))


def _layout_w_in(w):
    g = GROUP
    ret, hg = w[:, 0:4 * g], w[:, 4 * g:8 * g]
    ml = w[:, 8 * g:12 * g + 2 * N_HEADS]
    rw = w[:, 12 * g + 2 * N_HEADS:]
    ml_main, ml_gates = ml[:, 0:4 * g], ml[:, 4 * g:]
    out = jnp.concatenate([ret, hg, ml_main, rw, ml_gates], axis=1)
    return _pad_cols(out, N_IN_PAD).astype(BF16)


def kernel(x, w_in, w_out, norm_pre_mix, norm_post_mix, norm_pre_ffn, norm_post_ffn, w_ffn_gate, w_ffn_up, w_ffn_down, hgrn_lb_logits, hgrn_norm_w, mlstm_conv_w, mlstm_conv_b, mlstm_i_bias, mlstm_f_bias, mlstm_norm_w, rwkv_mu, rwkv_w0, rwkv_w_up, rwkv_a0, rwkv_a_up, rwkv_g_up, rwkv_k_k, rwkv_k_a, rwkv_r_k, rwkv_ln_w, rwkv_ln_b, rwkv_v0, rwkv_v_down, rwkv_v_up):
    b, s, d = x.shape
    depth = w_in.shape[0]
    t = b * s
    g = GROUP

    lb_cum = jnp.cumsum(jax.nn.softmax(hgrn_lb_logits.astype(F32), axis=0), axis=0)
    lower_bounds = lb_cum - lb_cum[0]

    bd_f32 = jnp.asarray(_block_ones(), F32)
    ones_bd = jnp.asarray(_block_ones(), BF16)
    cos, sin = _rope_tables(s)
    ret_tables = (cos, sin) + _retention_tables() + (bd_f32, ones_bd)
    hgrn_consts = (jnp.asarray(_tri_incl(HGRN_TS, HGRN_C), BF16), bd_f32, ones_bd)
    mlstm_consts = _mlstm_tables() + (bd_f32, ones_bd)
    rwkv_consts = (jnp.asarray(_tri_incl(RWKV_C), BF16), ones_bd)

    h = x.reshape(t, d)
    u = _prenorm(h, norm_pre_mix[0])
    v_first = None
    for l in range(depth):
        p3 = _in_proj(u, _layout_w_in(w_in[l])).reshape(b, s, N_IN_PAD)

        o_ret = _retention(p3, ret_tables)
        o_hgrn = _hgrn(p3, lower_bounds[l], hgrn_norm_w[l], hgrn_consts)
        o_ml = _mlstm(p3, mlstm_conv_w[l], mlstm_conv_b[l], mlstm_i_bias[l], mlstm_f_bias[l],
                      mlstm_norm_w[l], mlstm_consts)
        mu = rwkv_mu[l]
        prm = {
            "mu_rkv": mu[0:3 * g].reshape(1, -1), "mu_lora": mu[3 * g:].reshape(1, -1),
            "w0": rwkv_w0[l].reshape(1, g), "a0": rwkv_a0[l].reshape(1, g),
            "w_up": _pad_rows(rwkv_w_up[l], LANES).astype(BF16),
            "a_up": jnp.concatenate([jnp.zeros_like(rwkv_w_up[l]), rwkv_a_up[l]], axis=0).astype(BF16),
            "g_up": rwkv_g_up[l].astype(BF16),
            "k_k": rwkv_k_k[l].reshape(1, g), "k_a": rwkv_k_a[l].reshape(1, g),
            "r_k": rwkv_r_k[l].reshape(1, g), "ln_w": rwkv_ln_w[l].reshape(1, g),
            "ln_b": rwkv_ln_b[l].reshape(1, g),
        }
        if l > 0:
            prm["v0"] = rwkv_v0[l - 1].reshape(1, g)
            prm["v_down"] = _pad_cols(rwkv_v_down[l - 1], LANES).astype(BF16)
            prm["v_up"] = _pad_rows(rwkv_v_up[l - 1], LANES).astype(BF16)
        o_rw, v_first = _rwkv(p3, v_first, prm, rwkv_consts)

        outs = [o.reshape(t, g) for o in (o_ret, o_hgrn, o_ml, o_rw)]
        h, u = _out_proj(outs, w_out[l].astype(BF16), h, norm_post_mix[l], norm_pre_ffn[l])
        w_pre_next = norm_pre_mix[l + 1] if l + 1 < depth else norm_pre_mix[l]
        h, u = _ffn(u, h, w_ffn_gate[l].astype(BF16), w_ffn_up[l].astype(BF16),
                    w_ffn_down[l].astype(BF16), norm_post_ffn[l], w_pre_next)
    return h.reshape(b, s, d)
```

```python
import functools

import numpy as np
import jax
import jax.numpy as jnp
from jax import lax
from jax.experimental import pallas as pl
from jax.experimental.pallas import tpu as pltpu

F32 = jnp.float32
BF16 = jnp.bfloat16

HEAD_DIM = 64
N_HEADS = 8
GROUP = N_HEADS * HEAD_DIM
LANES = 128
N_PAIRS = GROUP // LANES
NORM_EPS = 1e-6
ROPE_BASE = 10000.0
RWKV_DECAY_SCALE = 0.6065306597126334
RWKV_LN_EPS = 64e-5
RWKV_LORA_COLS = 256
CONV_WIDTH = 4
NEG_BIG = -1e30

RET_C = 128
HGRN_TS = 128
HGRN_C = 16
MLSTM_C = 128
RWKV_C = 64
RWKV_TS = 128

COL_RET = 0
COL_HGRN = 4 * GROUP
COL_ML = 8 * GROUP
COL_RW = 12 * GROUP
COL_RW_LORA = 15 * GROUP
COL_ML_GATES = 15 * GROUP + RWKV_LORA_COLS
N_IN_PAD = 16 * GROUP

VMEM_LIMIT = 56 * 1024 * 1024


def _cparams(n_axes):
    return pltpu.CompilerParams(dimension_semantics=("arbitrary",) * n_axes,
                                vmem_limit_bytes=VMEM_LIMIT)


def _dot(a, b):
    return jnp.dot(a.astype(BF16), b.astype(BF16), preferred_element_type=F32)


def _dot_nt(a, b):
    return lax.dot_general(a.astype(BF16), b.astype(BF16), (((1,), (1,)), ((), ())),
                           preferred_element_type=F32)


def _dot_tn(a, b):
    return lax.dot_general(a.astype(BF16), b.astype(BF16), (((0,), (0,)), ((), ())),
                           preferred_element_type=F32)


def _split3(x):
    hi = x.astype(BF16)
    r1 = x - hi.astype(F32)
    mid = r1.astype(BF16)
    lo = (r1 - mid.astype(F32)).astype(BF16)
    return hi, mid, lo


def _dot_x3(x, sel):
    hi, mid, lo = _split3(x)
    d = lambda t: jnp.dot(t, sel, preferred_element_type=F32)
    return d(hi) + d(mid) + d(lo)


def _dot_3x(sel, x):
    hi, mid, lo = _split3(x)
    d = lambda t: jnp.dot(sel, t, preferred_element_type=F32)
    return d(hi) + d(mid) + d(lo)


def _bsum(x, ones_bd):
    hi = x.astype(BF16)
    lo = (x - hi.astype(F32)).astype(BF16)
    return (jnp.dot(hi, ones_bd, preferred_element_type=F32)
            + jnp.dot(lo, ones_bd, preferred_element_type=F32))


def _sigmoid(x):
    return 1.0 / (1.0 + jnp.exp(-x))


def _silu(x):
    return x * _sigmoid(x)


def _log_sigmoid(x):
    return jnp.minimum(x, 0.0) - jnp.log1p(jnp.exp(-jnp.abs(x)))


def _rms_rows(x, w):
    return x * lax.rsqrt(jnp.mean(x * x, axis=-1, keepdims=True) + NORM_EPS) * w


def _shift_rows(x, prev8, k):
    rolled = pltpu.roll(x, k, axis=0)
    row8 = lax.broadcasted_iota(jnp.int32, (8, x.shape[1]), 0)
    top = jnp.where(row8 < k, pltpu.roll(prev8, k, axis=0), rolled[0:8])
    return jnp.concatenate([top, rolled[8:]], axis=0)


def _rms_kernel(x_ref, w_ref, u_ref):
    u_ref[...] = _rms_rows(x_ref[...], w_ref[...]).astype(BF16)


def _prenorm(x2d, w, tm=512):
    t, d = x2d.shape
    return pl.pallas_call(
        _rms_kernel,
        grid=(t // tm,),
        in_specs=[pl.BlockSpec((tm, d), lambda i: (i, 0)),
                  pl.BlockSpec((1, d), lambda i: (0, 0))],
        out_specs=pl.BlockSpec((tm, d), lambda i: (i, 0)),
        out_shape=jax.ShapeDtypeStruct((t, d), BF16),
        compiler_params=_cparams(1),
        name="prenorm",
    )(x2d, w.reshape(1, d))


def _mm_kernel(u_ref, w_ref, o_ref):
    o_ref[...] = jnp.dot(u_ref[...], w_ref[...], preferred_element_type=F32)


def _in_proj(u, w, tm=1024, tn=1024):
    t, d = u.shape
    n = w.shape[1]
    tm = min(tm, t)
    return pl.pallas_call(
        _mm_kernel,
        grid=(t // tm, n // tn),
        in_specs=[pl.BlockSpec((tm, d), lambda i, j: (i, 0)),
                  pl.BlockSpec((d, tn), lambda i, j: (0, j))],
        out_specs=pl.BlockSpec((tm, tn), lambda i, j: (i, j)),
        out_shape=jax.ShapeDtypeStruct((t, n), F32),
        compiler_params=_cparams(2),
        name="in_proj",
    )(u, w)


def _outproj_kernel(o0_ref, o1_ref, o2_ref, o3_ref, w_ref, h_ref, wpost_ref, wpre_ref,
                    hout_ref, u_ref):
    acc = jnp.dot(o0_ref[...], w_ref[0 * GROUP:1 * GROUP, :], preferred_element_type=F32)
    acc += jnp.dot(o1_ref[...], w_ref[1 * GROUP:2 * GROUP, :], preferred_element_type=F32)
    acc += jnp.dot(o2_ref[...], w_ref[2 * GROUP:3 * GROUP, :], preferred_element_type=F32)
    acc += jnp.dot(o3_ref[...], w_ref[3 * GROUP:4 * GROUP, :], preferred_element_type=F32)
    h = h_ref[...] + _rms_rows(acc, wpost_ref[...])
    hout_ref[...] = h
    u_ref[...] = _rms_rows(h, wpre_ref[...]).astype(BF16)


def _out_proj(outs, w, h, w_post, w_pre, tm=512):
    t, d = h.shape
    tm = min(tm, t)
    row = lambda i: (i, 0)
    fixed = lambda i: (0, 0)
    return pl.pallas_call(
        _outproj_kernel,
        grid=(t // tm,),
        in_specs=[pl.BlockSpec((tm, GROUP), row)] * 4 + [
            pl.BlockSpec((d, d), fixed),
            pl.BlockSpec((tm, d), row),
            pl.BlockSpec((1, d), fixed),
            pl.BlockSpec((1, d), fixed)],
        out_specs=[pl.BlockSpec((tm, d), row), pl.BlockSpec((tm, d), row)],
        out_shape=[jax.ShapeDtypeStruct((t, d), F32), jax.ShapeDtypeStruct((t, d), BF16)],
        compiler_params=_cparams(1),
        name="out_proj",
    )(*outs, w, h, w_post.reshape(1, d), w_pre.reshape(1, d))


def _ffn_kernel(u_ref, h_ref, wg_ref, wu_ref, wd_ref, wpost_ref, wpre_ref, hout_ref, u_next_ref,
                acc_ref):
    j = pl.program_id(1)
    u = u_ref[...]
    g = jnp.dot(u, wg_ref[...], preferred_element_type=F32)
    up = jnp.dot(u, wu_ref[...], preferred_element_type=F32)

    @pl.when(j == 0)
    def _():
        acc_ref[...] = jnp.zeros_like(acc_ref)

    acc_ref[...] += jnp.dot((_silu(g) * up).astype(BF16), wd_ref[...], preferred_element_type=F32)

    @pl.when(j == pl.num_programs(1) - 1)
    def _():
        h = h_ref[...] + _rms_rows(acc_ref[...], wpost_ref[...])
        hout_ref[...] = h
        u_next_ref[...] = _rms_rows(h, wpre_ref[...]).astype(BF16)


def _ffn(u, h, wg, wu, wd, w_post, w_pre_next, tm=512, tf=512):
    t, d = h.shape
    f = wg.shape[1]
    tm = min(tm, t)
    row = lambda i, j: (i, 0)
    fixed = lambda i, j: (0, 0)
    return pl.pallas_call(
        _ffn_kernel,
        grid=(t // tm, f // tf),
        in_specs=[pl.BlockSpec((tm, d), row),
                  pl.BlockSpec((tm, d), row),
                  pl.BlockSpec((d, tf), lambda i, j: (0, j)),
                  pl.BlockSpec((d, tf), lambda i, j: (0, j)),
                  pl.BlockSpec((tf, d), lambda i, j: (j, 0)),
                  pl.BlockSpec((1, d), fixed),
                  pl.BlockSpec((1, d), fixed)],
        out_specs=[pl.BlockSpec((tm, d), row), pl.BlockSpec((tm, d), row)],
        out_shape=[jax.ShapeDtypeStruct((t, d), F32), jax.ShapeDtypeStruct((t, d), BF16)],
        scratch_shapes=[pltpu.VMEM((tm, d), F32)],
        compiler_params=_cparams(2),
        name="ffn",
    )(u, h, wg, wu, wd, w_post.reshape(1, d), w_pre_next.reshape(1, d))


def _block_ones():
    i = np.arange(LANES)
    return (i[:, None] // HEAD_DIM == i[None, :] // HEAD_DIM).astype(np.float32)


def _tri_incl(c, block=None):
    i = np.arange(c)
    m = i[None, :] <= i[:, None]
    if block is not None:
        m = m & (i[None, :] // block == i[:, None] // block)
    return m.astype(np.float32)


def _rope_tables(seq):
    half = HEAD_DIM // 2
    lane = np.arange(LANES)
    inv_freq = ROPE_BASE ** (-jnp.arange(half, dtype=F32) / half)
    ang = jnp.arange(seq, dtype=F32)[:, None] * inv_freq[None, :]
    cos, sin = jnp.cos(ang), jnp.sin(ang)
    fidx = (lane % HEAD_DIM) % half
    sign = np.where((lane % HEAD_DIM) < half, -1.0, 1.0).astype(np.float32)
    return cos[:, fidx], sin[:, fidx] * sign


def _retention_tables():
    c = RET_C
    log_gamma = jnp.log1p(-jnp.exp2(-5.0 - jnp.arange(N_HEADS, dtype=F32)))
    idx = jnp.arange(c, dtype=F32)
    rel = idx[:, None] - idx[None, :]
    intra = jnp.where(rel >= 0, jnp.exp(log_gamma[:, None, None] * jnp.maximum(rel, 0.0)), 0.0)
    q_dec = jnp.exp(log_gamma[:, None] * (idx + 1.0))
    k_dec = jnp.exp(log_gamma[:, None] * (c - 1.0 - idx))
    chunk_dec = jnp.exp(log_gamma * c)
    per_lane = lambda t: jnp.repeat(t.T, HEAD_DIM, axis=1)
    bd = jnp.asarray(_block_ones())
    lane_head = np.arange(LANES) // HEAD_DIM
    sdec = jnp.stack([bd * chunk_dec[2 * j + lane_head][None, :] for j in range(N_PAIRS)])
    return intra, per_lane(q_dec), per_lane(k_dec), sdec


def _ret_kernel(p_ref, cos_ref, sin_ref, intra_ref, qdec_ref, kdec_ref, sdec_ref, bd_ref,
                ones_ref, o_ref, st_ref):
    c = RET_C

    @pl.when(pl.program_id(1) == 0)
    def _():
        st_ref[...] = jnp.zeros_like(st_ref)

    cos = cos_ref[...]
    sin = sin_ref[...]
    lane = lax.broadcasted_iota(jnp.int32, (c, LANES), 1)
    first_half = (lane & (HEAD_DIM - 1)) < HEAD_DIM // 2
    lo = lane < HEAD_DIM
    ones_bd = ones_ref[...]
    bd = bd_ref[...]

    def rot(t):
        swapped = jnp.where(first_half, pltpu.roll(t, LANES - HEAD_DIM // 2, axis=1),
                            pltpu.roll(t, HEAD_DIM // 2, axis=1))
        return t * cos + swapped * sin

    for j in range(N_PAIRS):
        sl = slice(j * LANES, (j + 1) * LANES)
        q = rot(p_ref[0, :, j * LANES:(j + 1) * LANES])
        k = rot(p_ref[0, :, GROUP + j * LANES:GROUP + (j + 1) * LANES]) * HEAD_DIM ** -0.5
        v = p_ref[0, :, 2 * GROUP + j * LANES:2 * GROUP + (j + 1) * LANES]
        g = p_ref[0, :, 3 * GROUP + j * LANES:3 * GROUP + (j + 1) * LANES]
        st = st_ref[j]
        kb = k.astype(BF16)
        vb = v.astype(BF16)
        s_a = _dot_nt(jnp.where(lo, q, 0.0), kb) * intra_ref[2 * j]
        s_b = _dot_nt(jnp.where(lo, 0.0, q), kb) * intra_ref[2 * j + 1]
        out = jnp.where(lo, _dot(s_a, vb), _dot(s_b, vb)) + _dot(q * qdec_ref[:, sl], st)
        st_ref[j] = sdec_ref[j] * st + bd * _dot_tn(k * kdec_ref[:, sl], vb)
        ms = _bsum(out * out, ones_bd) * (1.0 / HEAD_DIM)
        o_ref[0, :, sl] = (out * lax.rsqrt(ms + NORM_EPS) * _silu(g)).astype(BF16)


def _retention(p3, tables):
    b, s, _ = p3.shape
    c = RET_C
    cos, sin, intra, qdec, kdec, sdec, bd, ones_bd = tables
    fixed2 = lambda i, t: (0, 0)
    fixed3 = lambda i, t: (0, 0, 0)
    return pl.pallas_call(
        _ret_kernel,
        grid=(b, s // c),
        in_specs=[pl.BlockSpec((1, c, 4 * GROUP), lambda i, t: (i, t, COL_RET // (4 * GROUP))),
                  pl.BlockSpec((c, LANES), lambda i, t: (t, 0)),
                  pl.BlockSpec((c, LANES), lambda i, t: (t, 0)),
                  pl.BlockSpec((N_HEADS, c, c), fixed3),
                  pl.BlockSpec((c, GROUP), fixed2),
                  pl.BlockSpec((c, GROUP), fixed2),
                  pl.BlockSpec((N_PAIRS, LANES, LANES), fixed3),
                  pl.BlockSpec((LANES, LANES), fixed2),
                  pl.BlockSpec((LANES, LANES), fixed2)],
        out_specs=pl.BlockSpec((1, c, GROUP), lambda i, t: (i, t, 0)),
        out_shape=jax.ShapeDtypeStruct((b, s, GROUP), BF16),
        scratch_shapes=[pltpu.VMEM((N_PAIRS, LANES, LANES), F32)],
        compiler_params=_cparams(2),
        name="retention",
    )(p3, cos, sin, intra, qdec, kdec, sdec, bd, ones_bd)


def _level_ref(cum, h):
    ts, n = cum.shape
    if h >= 8:
        return jnp.concatenate(
            [jnp.broadcast_to(cum[b0 + h - 1:b0 + h, :], (2 * h, n)) for b0 in range(0, ts, 2 * h)],
            axis=0)
    cum3 = cum.reshape(ts // 8, 8, n)
    brow = lambda i: jnp.broadcast_to(cum3[:, i:i + 1, :], cum3.shape).reshape(ts, n)
    sub = lax.broadcasted_iota(jnp.int32, (ts, n), 0) & 7
    if h == 4:
        return brow(3)
    if h == 2:
        return jnp.where(sub < 4, brow(1), brow(5))
    return jnp.where(sub < 2, brow(0), jnp.where(sub < 4, brow(2), jnp.where(sub < 6, brow(4), brow(6))))


def _hgrn_kernel(p_ref, loglb_ref, log1mlb_ref, onemlb_ref, normw_ref, tri_ref, bd_ref, ones_ref,
                 o_ref, st_ref):
    ts = HGRN_TS

    @pl.when(pl.program_id(1) == 0)
    def _():
        st_ref[...] = jnp.zeros_like(st_ref)

    pf = p_ref[0, :, GROUP:2 * GROUP]
    q = _silu(p_ref[0, :, 0:GROUP]) * HEAD_DIM ** -0.5
    k = onemlb_ref[...] * _sigmoid(-pf)
    a = loglb_ref[...]
    bb = log1mlb_ref[...] + _log_sigmoid(pf)
    log_f = jnp.maximum(a, bb) + jnp.log1p(jnp.exp(-jnp.abs(a - bb)))
    cum = _dot_3x(tri_ref[...], log_f)

    ones_bd = ones_ref[...]
    bd = bd_ref[...]
    lane = lax.broadcasted_iota(jnp.int32, (ts, LANES), 1)
    lo = lane < HEAD_DIM
    rowi = lax.broadcasted_iota(jnp.int32, (ts, GROUP), 0)
    row2 = lax.broadcasted_iota(jnp.int32, (2 * ts, ts), 0) & (ts - 1)
    col2 = lax.broadcasted_iota(jnp.int32, (2 * ts, ts), 1)
    diff2 = jnp.where(col2 < row2, row2 ^ col2, 0)

    scores = [jnp.zeros((2 * ts, ts), F32) for _ in range(N_PAIRS)]
    h = ts // 2
    while h >= 1:
        refc = _level_ref(cum, h)
        upper = (rowi & h) != 0
        q_l = jnp.where(upper, q * jnp.exp(jnp.minimum(cum - refc, 0.0)), 0.0)
        k_l = jnp.where(upper, 0.0, k * jnp.exp(jnp.minimum(refc - cum, 0.0)))
        level = (diff2 >= h) & (diff2 < 2 * h)
        for j in range(N_PAIRS):
            sl = slice(j * LANES, (j + 1) * LANES)
            sc = _dot_nt(_stack2(q_l[:, sl], lo), k_l[:, sl])
            scores[j] = scores[j] + jnp.where(level, sc, 0.0)
        h //= 2

    q_in = q * jnp.exp(cum)
    c_end = cum[ts - 1:ts, :]
    k_out = k * jnp.exp(c_end - cum)
    diag = q * k
    for j in range(N_PAIRS):
        sl = slice(j * LANES, (j + 1) * LANES)
        vj = p_ref[0, :, 2 * GROUP + j * LANES:2 * GROUP + (j + 1) * LANES]
        gj = p_ref[0, :, 3 * GROUP + j * LANES:3 * GROUP + (j + 1) * LANES]
        pv = _dot(scores[j], vj)
        st_t = st_ref[j]
        out = (jnp.where(lo, pv[0:ts], pv[ts:2 * ts]) + _bsum(diag[:, sl], ones_bd) * vj
               + _dot_nt(q_in[:, sl], st_t))
        st_ref[j] = st_t * jnp.exp(c_end[:, sl]) + bd * _dot_tn(vj, k_out[:, sl])
        ms = _bsum(out * out, ones_bd) * (1.0 / HEAD_DIM)
        y = out * lax.rsqrt(ms + NORM_EPS) * normw_ref[:, sl]
        o_ref[0, :, sl] = (y * _sigmoid(gj)).astype(BF16)


def _hgrn_kernel_v1(p_ref, loglb_ref, log1mlb_ref, onemlb_ref, normw_ref, tri_ref, bd_ref, ones_ref,
                    o_ref, q_s, k_s, c_s, st_ref):
    ts, c = HGRN_TS, HGRN_C

    @pl.when(pl.program_id(1) == 0)
    def _():
        st_ref[...] = jnp.zeros_like(st_ref)

    pf = p_ref[0, :, GROUP:2 * GROUP]
    q_s[...] = _silu(p_ref[0, :, 0:GROUP]) * HEAD_DIM ** -0.5
    k_s[...] = onemlb_ref[...] * _sigmoid(-pf)
    a = loglb_ref[...]
    bb = log1mlb_ref[...] + _log_sigmoid(pf)
    log_f = jnp.maximum(a, bb) + jnp.log1p(jnp.exp(-jnp.abs(a - bb)))
    c_s[...] = _dot_3x(tri_ref[...], log_f)

    ones_bd = ones_ref[...]
    bd = bd_ref[...]
    srow = lax.broadcasted_iota(jnp.int32, (c, LANES), 0)

    def chunk(ci, carry):
        r0 = pl.multiple_of(ci * c, c)
        rows = pl.ds(r0, c)
        for j in range(N_PAIRS):
            sl = slice(j * LANES, (j + 1) * LANES)
            qc = q_s[rows, sl]
            kc = k_s[rows, sl]
            cc = c_s[rows, sl]
            vc = p_ref[0, rows, 2 * GROUP + j * LANES:2 * GROUP + (j + 1) * LANES]
            gc = p_ref[0, rows, 3 * GROUP + j * LANES:3 * GROUP + (j + 1) * LANES]
            zs = []
            for t in range(c):
                w = jnp.where(srow <= t, jnp.exp(jnp.minimum(cc[t:t + 1, :] - cc, 0.0)), 0.0)
                zs.append(qc[t:t + 1, :] * kc * w)
            scores = _dot(jnp.concatenate(zs, axis=0), ones_bd)
            intra = jnp.concatenate(
                [jnp.sum(scores[t * c:(t + 1) * c] * vc, axis=0, keepdims=True) for t in range(c)],
                axis=0)
            st_t = st_ref[j]
            c_end = cc[c - 1:c, :]
            out = intra + _dot_nt(qc * jnp.exp(cc), st_t)
            st_ref[j] = st_t * jnp.exp(c_end) + bd * _dot_tn(vc, kc * jnp.exp(c_end - cc))
            ms = _bsum(out * out, ones_bd) * (1.0 / HEAD_DIM)
            y = out * lax.rsqrt(ms + NORM_EPS) * normw_ref[:, sl]
            o_ref[0, rows, sl] = (y * _sigmoid(gc)).astype(BF16)
        return carry

    lax.fori_loop(0, ts // c, chunk, 0)


def _hgrn(p3, lb, norm_w, consts):
    b, s, _ = p3.shape
    ts = HGRN_TS
    tri, bd, ones_bd = consts
    fixed2 = lambda i, t: (0, 0)
    row = lambda v: v.reshape(1, GROUP)
    return pl.pallas_call(
        _hgrn_kernel,
        grid=(b, s // ts),
        in_specs=[pl.BlockSpec((1, ts, 4 * GROUP), lambda i, t: (i, t, COL_HGRN // (4 * GROUP))),
                  pl.BlockSpec((1, GROUP), fixed2),
                  pl.BlockSpec((1, GROUP), fixed2),
                  pl.BlockSpec((1, GROUP), fixed2),
                  pl.BlockSpec((1, GROUP), fixed2),
                  pl.BlockSpec((ts, ts), fixed2),
                  pl.BlockSpec((LANES, LANES), fixed2),
                  pl.BlockSpec((LANES, LANES), fixed2)],
        out_specs=pl.BlockSpec((1, ts, GROUP), lambda i, t: (i, t, 0)),
        out_shape=jax.ShapeDtypeStruct((b, s, GROUP), BF16),
        scratch_shapes=[pltpu.VMEM((N_PAIRS, LANES, LANES), F32)],
        compiler_params=_cparams(2),
        name="hgrn2",
    )(p3, row(jnp.log(lb)), row(jnp.log1p(-lb)), row(1.0 - lb), row(norm_w), tri, bd, ones_bd)


def _mlstm_tables():
    c = MLSTM_C
    lane = np.arange(LANES)
    esel = np.zeros((LANES, N_HEADS * c), np.float32)
    for h in range(N_HEADS):
        esel[N_HEADS + h, h * c:(h + 1) * c] = 1.0
    epf = np.zeros((LANES, GROUP), np.float32)
    epi = np.zeros((LANES, GROUP), np.float32)
    for h in range(N_HEADS):
        epf[N_HEADS + h, h * HEAD_DIM:(h + 1) * HEAD_DIM] = 1.0
        epi[h, h * HEAD_DIM:(h + 1) * HEAD_DIM] = 1.0
    del lane
    return (jnp.asarray(_tri_incl(c), BF16), jnp.asarray(esel, BF16), jnp.asarray(epf, BF16),
            jnp.asarray(epi, BF16))


def _mlstm_kernel(pm_ref, pg_ref, convw_ref, convb_ref, gbias_ref, normw_ref, tri_ref, esel_ref,
                  epf_ref, epi_ref, bd_ref, ones_ref, o_ref,
                  prev_ref, cm_ref, n_ref, m_ref, mgl_ref):
    c = MLSTM_C

    @pl.when(pl.program_id(1) == 0)
    def _():
        prev_ref[...] = jnp.zeros_like(prev_ref)
        cm_ref[...] = jnp.zeros_like(cm_ref)
        n_ref[...] = jnp.zeros_like(n_ref)
        m_ref[...] = jnp.zeros_like(m_ref)
        mgl_ref[...] = jnp.zeros_like(mgl_ref)

    ones_bd = ones_ref[...]
    bd = bd_ref[...]

    x = pm_ref[0, :, 0:2 * GROUP]
    prev = prev_ref[...]
    acc = convb_ref[...] + x * convw_ref[CONV_WIDTH - 1:CONV_WIDTH, :]
    for sh in range(1, CONV_WIDTH):
        acc = acc + _shift_rows(x, prev, sh) * convw_ref[CONV_WIDTH - 1 - sh:CONV_WIDTH - sh, :]
    prev_ref[...] = x[c - 8:c]
    qk = _silu(acc)
    q_all = qk[:, 0:GROUP]
    k_all = qk[:, GROUP:2 * GROUP] * HEAD_DIM ** -0.5

    lane = lax.broadcasted_iota(jnp.int32, (c, LANES), 1)
    gb = pg_ref[0] + gbias_ref[...]
    gl = jnp.where(lane < N_HEADS, gb, jnp.where(lane < 2 * N_HEADS, _log_sigmoid(gb), 0.0))
    cum = _dot_3x(tri_ref[...], gl)
    mgl = mgl_ref[...]
    xm = cum + mgl
    colb = _dot_x3(xm, esel_ref[...])
    xm_t = xm.T
    gl_t = gl.T

    row = lax.broadcasted_iota(jnp.int32, (c, c), 0)
    col = lax.broadcasted_iota(jnp.int32, (c, c), 1)
    causal = col <= row
    dmat, w_inter, e_neg_m = [], [], []
    for h in range(N_HEADS):
        log_inter = colb[:, h * c:(h + 1) * c]
        logd = jnp.where(
            causal, log_inter + (gl_t[h:h + 1, :] - xm_t[N_HEADS + h:N_HEADS + h + 1, :]), NEG_BIG)
        m_t = jnp.maximum(jnp.max(logd, axis=1, keepdims=True), log_inter)
        dmat.append(jnp.exp(logd - m_t))
        w_inter.append(jnp.exp(log_inter - m_t))
        e_neg_m.append(jnp.exp(-m_t))

    cumcol = _dot_x3(cum, epf_ref[...])
    icol = _dot_x3(gl, epi_ref[...])
    c_end = cumcol[c - 1:c, :]
    m_old = m_ref[...]
    log_w = c_end - cumcol + icol
    m_new = jnp.maximum(c_end + m_old, jnp.max(log_w, axis=0, keepdims=True))
    decay = jnp.exp(c_end + m_old - m_new)
    kw_all = k_all * jnp.exp(log_w - m_new)
    n_old = n_ref[...]
    n_ref[...] = decay * n_old + jnp.sum(kw_all, axis=0, keepdims=True)
    m_ref[...] = m_new
    c_end_gl = cum[c - 1:c, :]
    log_w_gl = c_end_gl - cum + pltpu.roll(gl, N_HEADS, axis=1)
    m_new_gl = jnp.maximum(c_end_gl + mgl, jnp.max(log_w_gl, axis=0, keepdims=True))
    lane1 = lax.broadcasted_iota(jnp.int32, (1, LANES), 1)
    mgl_ref[...] = jnp.where((lane1 >= N_HEADS) & (lane1 < 2 * N_HEADS), m_new_gl, 0.0)

    lo = lane < HEAD_DIM
    for j in range(N_PAIRS):
        sl = slice(j * LANES, (j + 1) * LANES)
        qj = q_all[:, sl]
        kj = k_all[:, sl]
        vj = pm_ref[0, :, 2 * GROUP + j * LANES:2 * GROUP + (j + 1) * LANES]
        oj = pm_ref[0, :, 3 * GROUP + j * LANES:3 * GROUP + (j + 1) * LANES]
        kb = kj.astype(BF16)
        vb = vj.astype(BF16)
        s_a = _dot_nt(jnp.where(lo, qj, 0.0), kb) * dmat[2 * j]
        s_b = _dot_nt(jnp.where(lo, 0.0, qj), kb) * dmat[2 * j + 1]
        cm = cm_ref[j]
        wi = jnp.where(lo, w_inter[2 * j], w_inter[2 * j + 1])
        num = jnp.where(lo, _dot(s_a, vb), _dot(s_b, vb)) + wi * _dot(qj, cm)
        den = (jnp.where(lo, jnp.sum(s_a, axis=1, keepdims=True),
                         jnp.sum(s_b, axis=1, keepdims=True))
               + wi * _bsum(qj * n_old[:, sl], ones_bd))
        floor = jnp.where(lo, e_neg_m[2 * j], e_neg_m[2 * j + 1])
        hh = num / jnp.maximum(jnp.abs(den), floor)
        cm_ref[j] = cm * decay[:, sl] + bd * _dot_tn(kw_all[:, sl], vb)
        ms = _bsum(hh * hh, ones_bd) * (1.0 / HEAD_DIM)
        y = hh * lax.rsqrt(ms + NORM_EPS) * normw_ref[:, sl]
        o_ref[0, :, sl] = (_sigmoid(oj) * y).astype(BF16)


def _mlstm(p3, conv_w, conv_b, i_bias, f_bias, norm_w, consts):
    b, s, _ = p3.shape
    c = MLSTM_C
    tri, esel, epf, epi, bd, ones_bd = consts
    gbias = jnp.zeros((1, LANES), F32).at[0, 0:N_HEADS].set(i_bias).at[0, N_HEADS:2 * N_HEADS].set(f_bias)
    fixed2 = lambda i, t: (0, 0)
    return pl.pallas_call(
        _mlstm_kernel,
        grid=(b, s // c),
        in_specs=[pl.BlockSpec((1, c, 4 * GROUP), lambda i, t: (i, t, COL_ML // (4 * GROUP))),
                  pl.BlockSpec((1, c, LANES), lambda i, t: (i, t, COL_ML_GATES // LANES)),
                  pl.BlockSpec((CONV_WIDTH, 2 * GROUP), fixed2),
                  pl.BlockSpec((1, 2 * GROUP), fixed2),
                  pl.BlockSpec((1, LANES), fixed2),
                  pl.BlockSpec((1, GROUP), fixed2),
                  pl.BlockSpec((c, c), fixed2),
                  pl.BlockSpec((LANES, N_HEADS * c), fixed2),
                  pl.BlockSpec((LANES, GROUP), fixed2),
                  pl.BlockSpec((LANES, GROUP), fixed2),
                  pl.BlockSpec((LANES, LANES), fixed2),
                  pl.BlockSpec((LANES, LANES), fixed2)],
        out_specs=pl.BlockSpec((1, c, GROUP), lambda i, t: (i, t, 0)),
        out_shape=jax.ShapeDtypeStruct((b, s, GROUP), BF16),
        scratch_shapes=[pltpu.VMEM((8, 2 * GROUP), F32),
                        pltpu.VMEM((N_PAIRS, LANES, LANES), F32),
                        pltpu.VMEM((1, GROUP), F32),
                        pltpu.VMEM((1, GROUP), F32),
                        pltpu.VMEM((1, LANES), F32)],
        compiler_params=_cparams(2),
        name="mlstm",
    )(p3, p3, conv_w, conv_b.reshape(1, -1), gbias, norm_w.reshape(1, GROUP), tri, esel, epf, epi,
      bd, ones_bd)


def _stack2(x, lo):
    return jnp.concatenate([jnp.where(lo, x, 0.0), jnp.where(lo, 0.0, x)], axis=0)


def _rwkv_kernel(first_layer, *refs):
    ts, c = RWKV_TS, RWKV_C
    n_ch = ts // c
    if first_layer:
        (prkv_ref, plora_ref, mu_rkv_ref, mu_lora_ref, w0_ref, wup_ref, a0_ref, aup_ref, gup_ref,
         kk_ref, ka_ref, rk_ref, lnw_ref, lnb_ref, tri_ref, ones_ref,
         o_ref, vfirst_out_ref, prev_rkv_ref, prev_lora_ref, st_ref) = refs
    else:
        (prkv_ref, plora_ref, vfirst_ref, mu_rkv_ref, mu_lora_ref, w0_ref, wup_ref, a0_ref, aup_ref,
         gup_ref, kk_ref, ka_ref, rk_ref, lnw_ref, lnb_ref, v0_ref, vdown_ref, vup_ref, tri_ref,
         ones_ref, o_ref, prev_rkv_ref, prev_lora_ref, st_ref) = refs

    @pl.when(pl.program_id(1) == 0)
    def _():
        prev_rkv_ref[...] = jnp.zeros_like(prev_rkv_ref)
        prev_lora_ref[...] = jnp.zeros_like(prev_lora_ref)
        st_ref[...] = jnp.zeros_like(st_ref)

    ones_bd = ones_ref[...]

    def token_mix(x_ref, prev_ref, mu_ref):
        x = x_ref[0]
        shifted = _shift_rows(x, prev_ref[...], 1)
        prev_ref[...] = x[ts - 8:ts]
        return x + (shifted - x) * mu_ref[...]

    rkv = token_mix(prkv_ref, prev_rkv_ref, mu_rkv_ref)
    lora = token_mix(plora_ref, prev_lora_ref, mu_lora_ref)
    r = rkv[:, 0:GROUP]
    k = rkv[:, GROUP:2 * GROUP]
    v = rkv[:, 2 * GROUP:3 * GROUP]
    wa = lora[:, 0:LANES]
    log_w = -RWKV_DECAY_SCALE * _sigmoid(w0_ref[...] + _dot(jnp.tanh(wa), wup_ref[...]))
    a = _sigmoid(a0_ref[...] + _dot(wa, aup_ref[...]))
    g = _dot(_sigmoid(lora[:, LANES:2 * LANES]), gup_ref[...])
    if first_layer:
        vfirst_out_ref[0] = v
    else:
        mix = _sigmoid(v0_ref[...] + _dot(_dot(v, vdown_ref[...]), vup_ref[...]))
        v = v + (vfirst_ref[0] - v) * mix

    kk_raw = k * kk_ref[...]
    k2 = k * (1.0 + (a - 1.0) * ka_ref[...])
    cum = _dot_3x(tri_ref[...], log_w)
    c_end = [cum[(ci + 1) * c - 1:(ci + 1) * c, :] for ci in range(n_ch)]
    c_end_rows = jnp.concatenate([jnp.broadcast_to(e, (c, GROUP)) for e in c_end], axis=0)
    r_g = r * jnp.exp(cum)
    e_prev = jnp.exp(cum - log_w)
    e_neg = jnp.exp(-cum)
    e_end = jnp.exp(c_end_rows - cum)
    rk_term = r * k2 * rk_ref[...]

    lane = lax.broadcasted_iota(jnp.int32, (c, LANES), 1)
    lo = lane < HEAD_DIM
    row = lax.broadcasted_iota(jnp.int32, (2 * c, 2 * c), 0)
    col = lax.broadcasted_iota(jnp.int32, (2 * c, 2 * c), 1)
    strict = col < row
    incl = col <= row
    eye = (col == row).astype(F32)

    kk_all, b_all = [], []
    for j in range(N_PAIRS):
        sl = slice(j * LANES, (j + 1) * LANES)
        kkr = kk_raw[:, sl]
        kk = kkr / jnp.maximum(jnp.sqrt(_bsum(kkr * kkr, ones_bd)), 1e-12)
        kk_all.append(kk)
        b_all.append(a[:, sl] * kk)

    insts = [(ci, j) for ci in range(n_ch) for j in range(N_PAIRS)]
    st2 = {}
    for ci, j in insts:
        rs = slice(ci * c, (ci + 1) * c)
        sl = slice(j * LANES, (j + 1) * LANES)
        st2[ci, j] = dict(
            r_t=_stack2(r_g[rs, sl], lo),
            kk_t=_stack2(kk_all[j][rs] * e_prev[rs, sl], lo),
            k_h=_stack2(k2[rs, sl] * e_neg[rs, sl], lo),
            b_h=_stack2(b_all[j][rs] * e_neg[rs, sl], lo),
            k_e=_stack2(k2[rs, sl] * e_end[rs, sl], lo),
            b_e=_stack2(b_all[j][rs] * e_end[rs, sl], lo),
            v2=_stack2(v[rs, sl], lo))
    for key in insts:
        d = st2[key]
        prod = _dot_nt(jnp.concatenate([d["kk_t"], d["r_t"]], axis=0),
                       jnp.concatenate([d["b_h"], d["k_h"]], axis=0))
        d["lmat"] = jnp.where(strict, prod[0:2 * c, 0:2 * c], 0.0)
        d["a_kk"] = jnp.where(strict, prod[0:2 * c, 2 * c:4 * c], 0.0)
        d["a_rb"] = jnp.where(incl, prod[2 * c:4 * c, 0:2 * c], 0.0)
        d["a_rk"] = jnp.where(incl, prod[2 * c:4 * c, 2 * c:4 * c], 0.0)
    for key in insts:
        d = st2[key]
        d["s"] = eye - d["lmat"]
        d["p"] = _dot(d["lmat"], d["lmat"])
    for it in range(4):
        for key in insts:
            d = st2[key]
            both = _dot(jnp.concatenate([d["p"], d["s"]], axis=0), d["p"])
            d["p"] = both[0:2 * c]
            d["s"] = d["s"] + both[2 * c:4 * c]
    for key in insts:
        d = st2[key]
        d["s"] = d["s"] + _dot(d["s"], d["p"])
        d["akv"] = _dot(d["a_kk"], d["v2"])
    for key in insts:
        d = st2[key]
        tw = _dot(d["s"], jnp.concatenate([d["kk_t"], d["akv"]], axis=1))
        d["w"] = -tw[:, 0:LANES]
        d["u0"] = -tw[:, LANES:2 * LANES]
    for key in insts:
        d = st2[key]
        zero = jnp.zeros((2 * c, LANES), F32)
        rhs = jnp.concatenate([jnp.concatenate([d["v2"], zero], axis=1),
                               jnp.concatenate([d["u0"], d["w"]], axis=1)], axis=0)
        yq = _dot(jnp.concatenate([d["a_rk"], d["a_rb"]], axis=1), rhs)
        d["y0"] = yq[:, 0:LANES]
        d["q"] = d["r_t"] + yq[:, LANES:2 * LANES]
        d["m"] = _dot_tn(d["b_e"], d["w"])
        d["n_t"] = _dot_tn(jnp.concatenate([d["v2"], d["u0"]], axis=0),
                           jnp.concatenate([d["k_e"], d["b_e"]], axis=0))

    hts = [st_ref[j] for j in range(N_PAIRS)]
    ys = [[] for _ in range(N_PAIRS)]
    for ci in range(n_ch):
        for j in range(N_PAIRS):
            sl = slice(j * LANES, (j + 1) * LANES)
            d = st2[ci, j]
            ht = hts[j]
            y2 = _dot_nt(d["q"], ht) + d["y0"]
            hts[j] = ht * jnp.exp(c_end[ci][:, sl]) + _dot_nt(ht, d["m"]) + d["n_t"]
            ys[j].append(y2[0:c] + y2[c:2 * c])
    for j in range(N_PAIRS):
        st_ref[j] = hts[j]
    y_pairs = [jnp.concatenate(ys[j], axis=0) for j in range(N_PAIRS)]
    means = [_dot(y, ones_bd) * (1.0 / HEAD_DIM) for y in y_pairs]
    ycs = [y - m for y, m in zip(y_pairs, means)]
    vars_ = [_dot(yc * yc, ones_bd) * (1.0 / HEAD_DIM) for yc in ycs]
    for j in range(N_PAIRS):
        sl = slice(j * LANES, (j + 1) * LANES)
        yn = ycs[j] * lax.rsqrt(vars_[j] + RWKV_LN_EPS) * lnw_ref[:, sl] + lnb_ref[:, sl]
        bonus = _bsum(rk_term[:, sl], ones_bd) * v[:, sl]
        o_ref[0, :, sl] = ((yn + bonus) * g[:, sl]).astype(BF16)


def _rwkv_kernel_v1(first_layer, *refs):
    c = RWKV_C
    if first_layer:
        (prkv_ref, plora_ref, mu_rkv_ref, mu_lora_ref, w0_ref, wup_ref, a0_ref, aup_ref, gup_ref,
         kk_ref, ka_ref, rk_ref, lnw_ref, lnb_ref, tri_ref, ones_ref,
         o_ref, vfirst_out_ref, prev_rkv_ref, prev_lora_ref, st_ref) = refs
    else:
        (prkv_ref, plora_ref, vfirst_ref, mu_rkv_ref, mu_lora_ref, w0_ref, wup_ref, a0_ref, aup_ref,
         gup_ref, kk_ref, ka_ref, rk_ref, lnw_ref, lnb_ref, v0_ref, vdown_ref, vup_ref, tri_ref,
         ones_ref, o_ref, prev_rkv_ref, prev_lora_ref, st_ref) = refs

    @pl.when(pl.program_id(1) == 0)
    def _():
        prev_rkv_ref[...] = jnp.zeros_like(prev_rkv_ref)
        prev_lora_ref[...] = jnp.zeros_like(prev_lora_ref)
        st_ref[...] = jnp.zeros_like(st_ref)

    ones_bd = ones_ref[...]

    def token_mix(x_ref, prev_ref, mu_ref):
        x = x_ref[0]
        shifted = _shift_rows(x, prev_ref[...], 1)
        prev_ref[...] = x[c - 8:c]
        return x + (shifted - x) * mu_ref[...]

    rkv = token_mix(prkv_ref, prev_rkv_ref, mu_rkv_ref)
    lora = token_mix(plora_ref, prev_lora_ref, mu_lora_ref)
    r = rkv[:, 0:GROUP]
    k = rkv[:, GROUP:2 * GROUP]
    v = rkv[:, 2 * GROUP:3 * GROUP]
    wa = lora[:, 0:LANES]
    log_w = -RWKV_DECAY_SCALE * _sigmoid(w0_ref[...] + _dot(jnp.tanh(wa), wup_ref[...]))
    a = _sigmoid(a0_ref[...] + _dot(wa, aup_ref[...]))
    g = _dot(_sigmoid(lora[:, LANES:2 * LANES]), gup_ref[...])
    if first_layer:
        vfirst_out_ref[0] = v
    else:
        mix = _sigmoid(v0_ref[...] + _dot(_dot(v, vdown_ref[...]), vup_ref[...]))
        v = v + (vfirst_ref[0] - v) * mix

    kk_raw = k * kk_ref[...]
    k2 = k * (1.0 + (a - 1.0) * ka_ref[...])
    cum = _dot_3x(tri_ref[...], log_w)
    c_end = cum[c - 1:c, :]
    e_pos = jnp.exp(cum)
    e_neg = jnp.exp(-cum)
    e_end = jnp.exp(c_end - cum)
    rk_term = r * k2 * rk_ref[...]

    lane = lax.broadcasted_iota(jnp.int32, (c, LANES), 1)
    lo = lane < HEAD_DIM
    row = lax.broadcasted_iota(jnp.int32, (2 * c, 2 * c), 0)
    col = lax.broadcasted_iota(jnp.int32, (2 * c, 2 * c), 1)
    strict = col < row
    incl = col <= row
    eye = (col == row).astype(F32)

    for j in range(N_PAIRS):
        sl = slice(j * LANES, (j + 1) * LANES)
        kkr = kk_raw[:, sl]
        norm = jnp.sqrt(_bsum(kkr * kkr, ones_bd))
        kk = kkr / jnp.maximum(norm, 1e-12)
        bvec = a[:, sl] * kk
        r_t = _stack2(r[:, sl] * e_pos[:, sl], lo)
        kk_t = _stack2(kk * jnp.exp(cum[:, sl] - log_w[:, sl]), lo)
        k_h = _stack2(k2[:, sl] * e_neg[:, sl], lo)
        b_h = _stack2(bvec * e_neg[:, sl], lo)
        k_e = _stack2(k2[:, sl] * e_end[:, sl], lo)
        b_e = _stack2(bvec * e_end[:, sl], lo)
        v2 = _stack2(v[:, sl], lo)
        ht = st_ref[j]

        lmat = jnp.where(strict, _dot_nt(kk_t, b_h), 0.0)
        a_kk = jnp.where(strict, _dot_nt(kk_t, k_h), 0.0)
        a_rk = jnp.where(incl, _dot_nt(r_t, k_h), 0.0)
        a_rb = jnp.where(incl, _dot_nt(r_t, b_h), 0.0)
        tinv = eye - lmat
        lpow = lmat
        for _ in range(5):
            lpow = _dot(lpow, lpow)
            tinv = tinv + _dot(tinv, lpow)
        rhs = _dot_nt(kk_t, ht) + _dot(a_kk, v2)
        u2 = -_dot(tinv, rhs)
        y2 = _dot_nt(r_t, ht) + _dot(a_rk, v2) + _dot(a_rb, u2)
        y = y2[0:c] + y2[c:2 * c]
        st_ref[j] = ht * jnp.exp(c_end[:, sl]) + _dot_tn(v2, k_e) + _dot_tn(u2, b_e)

        mean = _bsum(y, ones_bd) * (1.0 / HEAD_DIM)
        yc = y - mean
        var = _bsum(yc * yc, ones_bd) * (1.0 / HEAD_DIM)
        yn = yc * lax.rsqrt(var + RWKV_LN_EPS) * lnw_ref[:, sl] + lnb_ref[:, sl]
        bonus = _bsum(rk_term[:, sl], ones_bd) * v[:, sl]
        o_ref[0, :, sl] = ((yn + bonus) * g[:, sl]).astype(BF16)


def _rwkv(p3, v_first, prm, consts):
    b, s, _ = p3.shape
    c = RWKV_TS
    tri, ones_bd = consts
    first_layer = v_first is None
    fixed2 = lambda i, t: (0, 0)
    row = lambda n: pl.BlockSpec((1, n), fixed2)
    full = lambda a: pl.BlockSpec(a.shape, fixed2)
    seq_spec = pl.BlockSpec((1, c, GROUP), lambda i, t: (i, t, 0))
    in_specs = [pl.BlockSpec((1, c, 3 * GROUP), lambda i, t: (i, t, COL_RW // (3 * GROUP))),
                pl.BlockSpec((1, c, RWKV_LORA_COLS), lambda i, t: (i, t, COL_RW_LORA // RWKV_LORA_COLS))]
    args = [p3, p3]
    if not first_layer:
        in_specs.append(seq_spec)
        args.append(v_first)
    in_specs += [row(3 * GROUP), row(RWKV_LORA_COLS), row(GROUP), full(prm["w_up"]), row(GROUP),
                 full(prm["a_up"]), full(prm["g_up"])] + [row(GROUP)] * 5
    args += [prm["mu_rkv"], prm["mu_lora"], prm["w0"], prm["w_up"], prm["a0"], prm["a_up"],
             prm["g_up"], prm["k_k"], prm["k_a"], prm["r_k"], prm["ln_w"], prm["ln_b"]]
    if not first_layer:
        in_specs += [row(GROUP), full(prm["v_down"]), full(prm["v_up"])]
        args += [prm["v0"], prm["v_down"], prm["v_up"]]
    in_specs += [pl.BlockSpec((c, c), fixed2), pl.BlockSpec((LANES, LANES), fixed2)]
    args += [tri, ones_bd]
    o_shape = jax.ShapeDtypeStruct((b, s, GROUP), BF16)
    if first_layer:
        out_specs = [seq_spec, seq_spec]
        out_shape = [o_shape, jax.ShapeDtypeStruct((b, s, GROUP), F32)]
    else:
        out_specs = seq_spec
        out_shape = o_shape
    res = pl.pallas_call(
        functools.partial(_rwkv_kernel, first_layer),
        grid=(b, s // c),
        in_specs=in_specs,
        out_specs=out_specs,
        out_shape=out_shape,
        scratch_shapes=[pltpu.VMEM((8, 3 * GROUP), F32),
                        pltpu.VMEM((8, RWKV_LORA_COLS), F32),
                        pltpu.VMEM((N_PAIRS, LANES, LANES), F32)],
        compiler_params=_cparams(2),
        name="rwkv7",
    )(*args)
    if first_layer:
        return res[0], res[1]
    return res, v_first


def _pad_cols(w, n):
    return jnp.pad(w, ((0, 0), (0, n - w.shape[1])))


def _pad_rows(w, n):
    return jnp.pad(w, ((0, n - w.shape[0]), (0, 0)))


def _layout_w_in(w):
    g = GROUP
    ret, hg = w[:, 0:4 * g], w[:, 4 * g:8 * g]
    ml = w[:, 8 * g:12 * g + 2 * N_HEADS]
    rw = w[:, 12 * g + 2 * N_HEADS:]
    ml_main, ml_gates = ml[:, 0:4 * g], ml[:, 4 * g:]
    out = jnp.concatenate([ret, hg, ml_main, rw, ml_gates], axis=1)
    return _pad_cols(out, N_IN_PAD).astype(BF16)


def kernel(x, w_in, w_out, norm_pre_mix, norm_post_mix, norm_pre_ffn, norm_post_ffn, w_ffn_gate, w_ffn_up, w_ffn_down, hgrn_lb_logits, hgrn_norm_w, mlstm_conv_w, mlstm_conv_b, mlstm_i_bias, mlstm_f_bias, mlstm_norm_w, rwkv_mu, rwkv_w0, rwkv_w_up, rwkv_a0, rwkv_a_up, rwkv_g_up, rwkv_k_k, rwkv_k_a, rwkv_r_k, rwkv_ln_w, rwkv_ln_b, rwkv_v0, rwkv_v_down, rwkv_v_up):
    b, s, d = x.shape
    depth = w_in.shape[0]
    t = b * s
    g = GROUP

    lb_cum = jnp.cumsum(jax.nn.softmax(hgrn_lb_logits.astype(F32), axis=0), axis=0)
    lower_bounds = lb_cum - lb_cum[0]

    bd_f32 = jnp.asarray(_block_ones(), F32)
    ones_bd = jnp.asarray(_block_ones(), BF16)
    cos, sin = _rope_tables(s)
    ret_tables = (cos, sin) + _retention_tables() + (bd_f32, ones_bd)
    hgrn_consts = (jnp.asarray(_tri_incl(HGRN_TS), BF16), bd_f32, ones_bd)
    mlstm_consts = _mlstm_tables() + (bd_f32, ones_bd)
    rwkv_consts = (jnp.asarray(_tri_incl(RWKV_TS, RWKV_C), BF16), ones_bd)

    h = x.reshape(t, d)
    u = _prenorm(h, norm_pre_mix[0])
    v_first = None
    for l in range(depth):
        p3 = _in_proj(u, _layout_w_in(w_in[l])).reshape(b, s, N_IN_PAD)

        o_ret = _retention(p3, ret_tables)
        o_hgrn = _hgrn(p3, lower_bounds[l], hgrn_norm_w[l], hgrn_consts)
        o_ml = _mlstm(p3, mlstm_conv_w[l], mlstm_conv_b[l], mlstm_i_bias[l], mlstm_f_bias[l],
                      mlstm_norm_w[l], mlstm_consts)
        mu = rwkv_mu[l]
        prm = {
            "mu_rkv": mu[0:3 * g].reshape(1, -1), "mu_lora": mu[3 * g:].reshape(1, -1),
            "w0": rwkv_w0[l].reshape(1, g), "a0": rwkv_a0[l].reshape(1, g),
            "w_up": _pad_rows(rwkv_w_up[l], LANES).astype(BF16),
            "a_up": jnp.concatenate([jnp.zeros_like(rwkv_w_up[l]), rwkv_a_up[l]], axis=0).astype(BF16),
            "g_up": rwkv_g_up[l].astype(BF16),
            "k_k": rwkv_k_k[l].reshape(1, g), "k_a": rwkv_k_a[l].reshape(1, g),
            "r_k": rwkv_r_k[l].reshape(1, g), "ln_w": rwkv_ln_w[l].reshape(1, g),
            "ln_b": rwkv_ln_b[l].reshape(1, g),
        }
        if l > 0:
            prm["v0"] = rwkv_v0[l - 1].reshape(1, g)
            prm["v_down"] = _pad_cols(rwkv_v_down[l - 1], LANES).astype(BF16)
            prm["v_up"] = _pad_rows(rwkv_v_up[l - 1], LANES).astype(BF16)
        o_rw, v_first = _rwkv(p3, v_first, prm, rwkv_consts)

        outs = [o.reshape(t, g) for o in (o_ret, o_hgrn, o_ml, o_rw)]
        h, u = _out_proj(outs, w_out[l].astype(BF16), h, norm_post_mix[l], norm_pre_ffn[l])
        w_pre_next = norm_pre_mix[l + 1] if l + 1 < depth else norm_pre_mix[l]
        h, u = _ffn(u, h, w_ffn_gate[l].astype(BF16), w_ffn_up[l].astype(BF16),
                    w_ffn_down[l].astype(BF16), norm_post_ffn[l], w_pre_next)
    return h.reshape(b, s, d)
```

```python
import functools

import numpy as np
import jax
import jax.numpy as jnp
from jax import lax
from jax.experimental import pallas as pl
from jax.experimental.pallas import tpu as pltpu

F32 = jnp.float32
BF16 = jnp.bfloat16

HEAD_DIM = 64
N_HEADS = 8
GROUP = N_HEADS * HEAD_DIM
LANES = 128
N_PAIRS = GROUP // LANES
NORM_EPS = 1e-6
ROPE_BASE = 10000.0
RWKV_DECAY_SCALE = 0.6065306597126334
RWKV_LN_EPS = 64e-5
RWKV_LORA_COLS = 256
CONV_WIDTH = 4
NEG_BIG = -1e30

MIX_TS = 128
RWKV_C = 64
MIXER_GROUPS = ((("rwkv", 0), ("hgrn", 0)), (("mlstm", 0), ("ret", 0)))

COL_RET = 0
COL_HGRN = 4 * GROUP
COL_ML = 8 * GROUP
COL_RW = 12 * GROUP
COL_RW_LORA = 15 * GROUP
COL_ML_GATES = 15 * GROUP + RWKV_LORA_COLS
N_IN_PAD = 16 * GROUP

VMEM_LIMIT = 56 * 1024 * 1024


def _cparams(n_axes):
    return pltpu.CompilerParams(dimension_semantics=("arbitrary",) * n_axes,
                                vmem_limit_bytes=VMEM_LIMIT)


def _dot(a, b):
    return jnp.dot(a.astype(BF16), b.astype(BF16), preferred_element_type=F32)


def _dot_nt(a, b):
    return lax.dot_general(a.astype(BF16), b.astype(BF16), (((1,), (1,)), ((), ())),
                           preferred_element_type=F32)


def _dot_tn(a, b):
    return lax.dot_general(a.astype(BF16), b.astype(BF16), (((0,), (0,)), ((), ())),
                           preferred_element_type=F32)


def _split3(x):
    hi = x.astype(BF16)
    r1 = x - hi.astype(F32)
    mid = r1.astype(BF16)
    lo = (r1 - mid.astype(F32)).astype(BF16)
    return hi, mid, lo


def _dot_x3(x, sel):
    hi, mid, lo = _split3(x)
    d = lambda t: jnp.dot(t, sel, preferred_element_type=F32)
    return d(hi) + d(mid) + d(lo)


def _dot_3x(sel, x):
    hi, mid, lo = _split3(x)
    d = lambda t: jnp.dot(sel, t, preferred_element_type=F32)
    return d(hi) + d(mid) + d(lo)


def _bsum(x, ones_bd):
    hi = x.astype(BF16)
    lo = (x - hi.astype(F32)).astype(BF16)
    return (jnp.dot(hi, ones_bd, preferred_element_type=F32)
            + jnp.dot(lo, ones_bd, preferred_element_type=F32))


def _sigmoid(x):
    return 1.0 / (1.0 + jnp.exp(-x))


def _silu(x):
    return x * _sigmoid(x)


def _log_sigmoid(x):
    return jnp.minimum(x, 0.0) - jnp.log1p(jnp.exp(-jnp.abs(x)))


def _rms_rows(x, w):
    return x * lax.rsqrt(jnp.mean(x * x, axis=-1, keepdims=True) + NORM_EPS) * w


def _shift_rows(x, prev8, k):
    rolled = pltpu.roll(x, k, axis=0)
    row8 = lax.broadcasted_iota(jnp.int32, (8, x.shape[1]), 0)
    top = jnp.where(row8 < k, pltpu.roll(prev8, k, axis=0), rolled[0:8])
    return jnp.concatenate([top, rolled[8:]], axis=0)


def _stack2(x, lo):
    return jnp.concatenate([jnp.where(lo, x, 0.0), jnp.where(lo, 0.0, x)], axis=0)


def _rms_kernel(x_ref, w_ref, u_ref):
    u_ref[...] = _rms_rows(x_ref[...], w_ref[...]).astype(BF16)


def _prenorm(x2d, w, tm=512):
    t, d = x2d.shape
    return pl.pallas_call(
        _rms_kernel,
        grid=(t // tm,),
        in_specs=[pl.BlockSpec((tm, d), lambda i: (i, 0)),
                  pl.BlockSpec((1, d), lambda i: (0, 0))],
        out_specs=pl.BlockSpec((tm, d), lambda i: (i, 0)),
        out_shape=jax.ShapeDtypeStruct((t, d), BF16),
        compiler_params=_cparams(1),
        name="prenorm",
    )(x2d, w.reshape(1, d))


def _mm_kernel(u_ref, w_ref, o_ref):
    o_ref[...] = jnp.dot(u_ref[...], w_ref[...], preferred_element_type=F32)


def _in_proj(u, w, tm=2048, tn=1024):
    t, d = u.shape
    n = w.shape[1]
    tm = min(tm, t)
    return pl.pallas_call(
        _mm_kernel,
        grid=(t // tm, n // tn),
        in_specs=[pl.BlockSpec((tm, d), lambda i, j: (i, 0)),
                  pl.BlockSpec((d, tn), lambda i, j: (0, j))],
        out_specs=pl.BlockSpec((tm, tn), lambda i, j: (i, j)),
        out_shape=jax.ShapeDtypeStruct((t, n), F32),
        compiler_params=_cparams(2),
        name="in_proj",
    )(u, w)


def _outproj_kernel(o0_ref, o1_ref, o2_ref, o3_ref, w_ref, h_ref, wpost_ref, wpre_ref,
                    hout_ref, u_ref):
    acc = jnp.dot(o0_ref[...], w_ref[0 * GROUP:1 * GROUP, :], preferred_element_type=F32)
    acc += jnp.dot(o1_ref[...], w_ref[1 * GROUP:2 * GROUP, :], preferred_element_type=F32)
    acc += jnp.dot(o2_ref[...], w_ref[2 * GROUP:3 * GROUP, :], preferred_element_type=F32)
    acc += jnp.dot(o3_ref[...], w_ref[3 * GROUP:4 * GROUP, :], preferred_element_type=F32)
    h = h_ref[...] + _rms_rows(acc, wpost_ref[...])
    hout_ref[...] = h
    u_ref[...] = _rms_rows(h, wpre_ref[...]).astype(BF16)


def _out_proj(outs, w, h, w_post, w_pre, tm=512):
    t, d = h.shape
    tm = min(tm, t)
    row = lambda i: (i, 0)
    fixed = lambda i: (0, 0)
    return pl.pallas_call(
        _outproj_kernel,
        grid=(t // tm,),
        in_specs=[pl.BlockSpec((tm, GROUP), row)] * 4 + [
            pl.BlockSpec((d, d), fixed),
            pl.BlockSpec((tm, d), row),
            pl.BlockSpec((1, d), fixed),
            pl.BlockSpec((1, d), fixed)],
        out_specs=[pl.BlockSpec((tm, d), row), pl.BlockSpec((tm, d), row)],
        out_shape=[jax.ShapeDtypeStruct((t, d), F32), jax.ShapeDtypeStruct((t, d), BF16)],
        compiler_params=_cparams(1),
        name="out_proj",
    )(*outs, w, h, w_post.reshape(1, d), w_pre.reshape(1, d))


def _ffn_up_kernel(u_ref, wg_ref, wu_ref, a_ref):
    u = u_ref[...]
    g = jnp.dot(u, wg_ref[...], preferred_element_type=F32)
    up = jnp.dot(u, wu_ref[...], preferred_element_type=F32)
    a_ref[...] = (_silu(g) * up).astype(BF16)


def _ffn_down_kernel(a_ref, wd_ref, h_ref, wpost_ref, wpre_ref, hout_ref, u_next_ref):
    kk = pl.program_id(1)

    @pl.when(kk == 0)
    def _():
        hout_ref[...] = jnp.zeros_like(hout_ref)

    hout_ref[...] += jnp.dot(a_ref[...], wd_ref[...], preferred_element_type=F32)

    @pl.when(kk == pl.num_programs(1) - 1)
    def _():
        h = h_ref[...] + _rms_rows(hout_ref[...], wpost_ref[...])
        hout_ref[...] = h
        u_next_ref[...] = _rms_rows(h, wpre_ref[...]).astype(BF16)


def _ffn(u, h, wg, wu, wd, w_post, w_pre_next, tm_up=2048, tf=512, tm_down=512, tk=1408):
    t, d = h.shape
    f = wg.shape[1]
    tm_up, tm_down = min(tm_up, t), min(tm_down, t)
    act = pl.pallas_call(
        _ffn_up_kernel,
        grid=(t // tm_up, f // tf),
        in_specs=[pl.BlockSpec((tm_up, d), lambda i, j: (i, 0)),
                  pl.BlockSpec((d, tf), lambda i, j: (0, j)),
                  pl.BlockSpec((d, tf), lambda i, j: (0, j))],
        out_specs=pl.BlockSpec((tm_up, tf), lambda i, j: (i, j)),
        out_shape=jax.ShapeDtypeStruct((t, f), BF16),
        compiler_params=_cparams(2),
        name="ffn_up",
    )(u, wg, wu)
    row = lambda i, k: (i, 0)
    fixed = lambda i, k: (0, 0)
    return pl.pallas_call(
        _ffn_down_kernel,
        grid=(t // tm_down, f // tk),
        in_specs=[pl.BlockSpec((tm_down, tk), lambda i, k: (i, k)),
                  pl.BlockSpec((tk, d), lambda i, k: (k, 0)),
                  pl.BlockSpec((tm_down, d), row),
                  pl.BlockSpec((1, d), fixed),
                  pl.BlockSpec((1, d), fixed)],
        out_specs=[pl.BlockSpec((tm_down, d), row), pl.BlockSpec((tm_down, d), row)],
        out_shape=[jax.ShapeDtypeStruct((t, d), F32), jax.ShapeDtypeStruct((t, d), BF16)],
        compiler_params=_cparams(2),
        name="ffn_down",
    )(act, wd, h, w_post.reshape(1, d), w_pre_next.reshape(1, d))


def _block_ones():
    i = np.arange(LANES)
    return (i[:, None] // HEAD_DIM == i[None, :] // HEAD_DIM).astype(np.float32)


def _tri_incl(c, block=None):
    i = np.arange(c)
    m = i[None, :] <= i[:, None]
    if block is not None:
        m = m & (i[None, :] // block == i[:, None] // block)
    return m.astype(np.float32)


def _rope_tables(seq):
    half = HEAD_DIM // 2
    lane = np.arange(LANES)
    inv_freq = ROPE_BASE ** (-jnp.arange(half, dtype=F32) / half)
    ang = jnp.arange(seq, dtype=F32)[:, None] * inv_freq[None, :]
    cos, sin = jnp.cos(ang), jnp.sin(ang)
    fidx = (lane % HEAD_DIM) % half
    sign = np.where((lane % HEAD_DIM) < half, -1.0, 1.0).astype(np.float32)
    return cos[:, fidx], sin[:, fidx] * sign


def _retention_tables():
    c = MIX_TS
    log_gamma = jnp.log1p(-jnp.exp2(-5.0 - jnp.arange(N_HEADS, dtype=F32)))
    idx = jnp.arange(c, dtype=F32)
    rel = idx[:, None] - idx[None, :]
    intra = jnp.where(rel >= 0, jnp.exp(log_gamma[:, None, None] * jnp.maximum(rel, 0.0)), 0.0)
    q_dec = jnp.exp(log_gamma[:, None] * (idx + 1.0))
    k_dec = jnp.exp(log_gamma[:, None] * (c - 1.0 - idx))
    chunk_dec = jnp.exp(log_gamma * c)
    per_lane = lambda t: jnp.repeat(t.T, HEAD_DIM, axis=1)
    bd = jnp.asarray(_block_ones())
    lane_head = np.arange(LANES) // HEAD_DIM
    sdec = jnp.stack([bd * chunk_dec[2 * j + lane_head][None, :] for j in range(N_PAIRS)])
    intra2 = intra.reshape(N_PAIRS, 2 * c, c)
    return intra2, per_lane(q_dec), per_lane(k_dec), sdec


def _mlstm_tables():
    c = MIX_TS
    esel = np.zeros((LANES, N_HEADS * c), np.float32)
    epf = np.zeros((LANES, GROUP), np.float32)
    epi = np.zeros((LANES, GROUP), np.float32)
    for h in range(N_HEADS):
        esel[N_HEADS + h, h * c:(h + 1) * c] = 1.0
        epf[N_HEADS + h, h * HEAD_DIM:(h + 1) * HEAD_DIM] = 1.0
        epi[h, h * HEAD_DIM:(h + 1) * HEAD_DIM] = 1.0
    return (jnp.asarray(_tri_incl(c), BF16), jnp.asarray(esel, BF16), jnp.asarray(epf, BF16),
            jnp.asarray(epi, BF16))


def _fixed(ndim):
    return lambda i, t: (0,) * ndim


def _full(a):
    return pl.BlockSpec(a.shape, _fixed(a.ndim))


def _seq_spec(width, col_block=0):
    return pl.BlockSpec((1, MIX_TS, width), lambda i, t: (i, t, col_block))


def _mixer_kernel(bodies, n_in, n_out, n_scr, delays, *refs):
    ins, outs, scrs = [], [], []
    pos = 0
    for group, counts in ((ins, n_in), (outs, n_out), (scrs, n_scr)):
        for n in counts:
            group.append(refs[pos:pos + n])
            pos += n

    @pl.when(pl.program_id(1) == 0)
    def _():
        for scr in scrs:
            for ref in scr:
                ref[...] = jnp.zeros_like(ref)

    active = [body(*i, *o, *s) for body, i, o, s in zip(bodies, ins, outs, scrs)]
    waits = list(delays)
    while active:
        for gen in list(active):
            k = active.index(gen)
            if waits[k] > 0:
                waits[k] -= 1
            elif next(gen, "done") == "done":
                active.pop(k)
                waits.pop(k)


def _run_mixers(specs, b, s, delays=None):
    kern = functools.partial(_mixer_kernel, tuple(sp["body"] for sp in specs),
                             tuple(len(sp["args"]) for sp in specs),
                             tuple(len(sp["out_shape"]) for sp in specs),
                             tuple(len(sp["scratch"]) for sp in specs),
                             tuple(delays) if delays else (0,) * len(specs))
    flat = lambda key: [x for sp in specs for x in sp[key]]
    res = pl.pallas_call(
        kern,
        grid=(b, s // MIX_TS),
        in_specs=flat("in_specs"),
        out_specs=flat("out_specs"),
        out_shape=flat("out_shape"),
        scratch_shapes=flat("scratch"),
        compiler_params=_cparams(2),
        name="mix_" + "_".join(sp["name"] for sp in specs),
    )(*flat("args"))
    outs, pos = {}, 0
    for sp in specs:
        n = len(sp["out_shape"])
        outs[sp["name"]] = res[pos:pos + n]
        pos += n
    return outs


def _out_bf16(b, s):
    return jax.ShapeDtypeStruct((b, s, GROUP), BF16)


def _ret_body(p_ref, cos_ref, sin_ref, intra_ref, qdec_ref, kdec_ref, sdec_ref, bd_ref,
              ones_ref, o_ref, st_ref):
    c = MIX_TS
    cos = cos_ref[...]
    sin = sin_ref[...]
    lane = lax.broadcasted_iota(jnp.int32, (c, LANES), 1)
    first_half = (lane & (HEAD_DIM - 1)) < HEAD_DIM // 2
    lo = lane < HEAD_DIM
    ones_bd = ones_ref[...]
    bd = bd_ref[...]

    def rot(t):
        swapped = jnp.where(first_half, pltpu.roll(t, LANES - HEAD_DIM // 2, axis=1),
                            pltpu.roll(t, HEAD_DIM // 2, axis=1))
        return t * cos + swapped * sin

    pairs = range(N_PAIRS)
    sls = [slice(j * LANES, (j + 1) * LANES) for j in pairs]
    qs = [rot(p_ref[0, :, j * LANES:(j + 1) * LANES]) for j in pairs]
    ks = [rot(p_ref[0, :, GROUP + j * LANES:GROUP + (j + 1) * LANES]) * HEAD_DIM ** -0.5
          for j in pairs]
    vbs = [p_ref[0, :, 2 * GROUP + j * LANES:2 * GROUP + (j + 1) * LANES].astype(BF16)
           for j in pairs]
    sts = [st_ref[j] for j in pairs]
    yield
    scores = [_dot_nt(_stack2(qs[j], lo), ks[j]) * intra_ref[j] for j in pairs]
    pvs = [_dot(scores[j], vbs[j]) for j in pairs]
    inters = [_dot(qs[j] * qdec_ref[:, sls[j]], sts[j]) for j in pairs]
    outs = [jnp.where(lo, pvs[j][0:c], pvs[j][c:2 * c]) + inters[j] for j in pairs]
    yield
    for j in pairs:
        st_ref[j] = sdec_ref[j] * sts[j] + bd * _dot_tn(ks[j] * kdec_ref[:, sls[j]], vbs[j])
    mss = [_bsum(outs[j] * outs[j], ones_bd) * (1.0 / HEAD_DIM) for j in pairs]
    for j in pairs:
        g = p_ref[0, :, 3 * GROUP + j * LANES:3 * GROUP + (j + 1) * LANES]
        o_ref[0, :, sls[j]] = (outs[j] * lax.rsqrt(mss[j] + NORM_EPS) * _silu(g)).astype(BF16)
    yield


def _ret_spec(p3, tables):
    b, s, _ = p3.shape
    c = MIX_TS
    cos, sin, intra, qdec, kdec, sdec, bd, ones_bd = tables
    tab_spec = pl.BlockSpec((c, LANES), lambda i, t: (t, 0))
    return dict(
        name="ret", body=_ret_body,
        args=[p3, cos, sin, intra, qdec, kdec, sdec, bd, ones_bd],
        in_specs=[_seq_spec(4 * GROUP, COL_RET // (4 * GROUP)), tab_spec, tab_spec,
                  _full(intra), _full(qdec), _full(kdec), _full(sdec), _full(bd), _full(ones_bd)],
        out_specs=[_seq_spec(GROUP)], out_shape=[_out_bf16(b, s)],
        scratch=[pltpu.VMEM((N_PAIRS, LANES, LANES), F32)])


def _level_ref(cum, h):
    ts, n = cum.shape
    if h >= 8:
        return jnp.concatenate(
            [jnp.broadcast_to(cum[b0 + h - 1:b0 + h, :], (2 * h, n)) for b0 in range(0, ts, 2 * h)],
            axis=0)
    cum3 = cum.reshape(ts // 8, 8, n)
    brow = lambda i: jnp.broadcast_to(cum3[:, i:i + 1, :], cum3.shape).reshape(ts, n)
    sub = lax.broadcasted_iota(jnp.int32, (ts, n), 0) & 7
    if h == 4:
        return brow(3)
    if h == 2:
        return jnp.where(sub < 4, brow(1), brow(5))
    return jnp.where(sub < 2, brow(0), jnp.where(sub < 4, brow(2), jnp.where(sub < 6, brow(4), brow(6))))


def _hgrn_body(p_ref, loglb_ref, log1mlb_ref, onemlb_ref, normw_ref, tri_ref, bd_ref, ones_ref,
               o_ref, st_ref):
    ts = MIX_TS
    pf = p_ref[0, :, GROUP:2 * GROUP]
    q = _silu(p_ref[0, :, 0:GROUP]) * HEAD_DIM ** -0.5
    e_f = jnp.exp(-jnp.abs(pf))
    k = onemlb_ref[...] * (jnp.where(pf >= 0.0, e_f, 1.0) / (1.0 + e_f))
    a = loglb_ref[...]
    bb = log1mlb_ref[...] + (jnp.minimum(pf, 0.0) - jnp.log1p(e_f))
    log_f = jnp.maximum(a, bb) + jnp.log1p(jnp.exp(-jnp.abs(a - bb)))
    cum = _dot_3x(tri_ref[...], log_f)
    yield

    ones_bd = ones_ref[...]
    bd = bd_ref[...]
    lo = lax.broadcasted_iota(jnp.int32, (ts, LANES), 1) < HEAD_DIM
    row2 = lax.broadcasted_iota(jnp.int32, (2 * ts, ts), 0) & (ts - 1)
    col2 = lax.broadcasted_iota(jnp.int32, (2 * ts, ts), 1)
    diff2 = jnp.where(col2 < row2, row2 ^ col2, 0)

    lo4 = (lax.broadcasted_iota(jnp.int32, (ts, GROUP), 1) & HEAD_DIM) == 0
    q_a = jnp.where(lo4, q, 0.0)
    q_b = q - q_a
    scores = [None] * N_PAIRS
    h = ts // 2
    while h >= 1:
        e = jnp.exp(-jnp.abs(cum - _level_ref(cum, h)))
        qa_l, qb_l, k_l = q_a * e, q_b * e, k * e
        level = (diff2 >> (h.bit_length() - 1)) == 1
        for j in range(N_PAIRS):
            sl = slice(j * LANES, (j + 1) * LANES)
            sc = _dot_nt(jnp.concatenate([qa_l[:, sl], qb_l[:, sl]], axis=0), k_l[:, sl])
            scores[j] = jnp.where(level, sc, 0.0 if scores[j] is None else scores[j])
        h //= 2
        yield

    q_in = q * jnp.exp(cum)
    c_end = cum[ts - 1:ts, :]
    k_out = k * jnp.exp(c_end - cum)
    diag = q * k
    for j in range(N_PAIRS):
        sl = slice(j * LANES, (j + 1) * LANES)
        vj = p_ref[0, :, 2 * GROUP + j * LANES:2 * GROUP + (j + 1) * LANES]
        gj = p_ref[0, :, 3 * GROUP + j * LANES:3 * GROUP + (j + 1) * LANES]
        pv = _dot(scores[j], vj)
        st_t = st_ref[j]
        out = (jnp.where(lo, pv[0:ts], pv[ts:2 * ts]) + _bsum(diag[:, sl], ones_bd) * vj
               + _dot_nt(q_in[:, sl], st_t))
        st_ref[j] = st_t * jnp.exp(c_end[:, sl]) + bd * _dot_tn(vj, k_out[:, sl])
        ms = _bsum(out * out, ones_bd) * (1.0 / HEAD_DIM)
        y = out * lax.rsqrt(ms + NORM_EPS) * normw_ref[:, sl]
        o_ref[0, :, sl] = (y * _sigmoid(gj)).astype(BF16)
        yield


def _hgrn_spec(p3, lb, norm_w, consts):
    b, s, _ = p3.shape
    tri, bd, ones_bd = consts
    row = lambda v: v.reshape(1, GROUP)
    args = [p3, row(jnp.log(lb)), row(jnp.log1p(-lb)), row(1.0 - lb), row(norm_w), tri, bd, ones_bd]
    return dict(
        name="hgrn", body=_hgrn_body, args=args,
        in_specs=[_seq_spec(4 * GROUP, COL_HGRN // (4 * GROUP))] + [_full(a) for a in args[1:]],
        out_specs=[_seq_spec(GROUP)], out_shape=[_out_bf16(b, s)],
        scratch=[pltpu.VMEM((N_PAIRS, LANES, LANES), F32)])


def _mlstm_body(pm_ref, pg_ref, convw_ref, convb_ref, gbias_ref, normw_ref, tri_ref, esel_ref,
                epf_ref, epi_ref, bd_ref, ones_ref, o_ref,
                prev_ref, cm_ref, n_ref, m_ref, mgl_ref):
    c = MIX_TS
    ones_bd = ones_ref[...]
    bd = bd_ref[...]

    x = pm_ref[0, :, 0:2 * GROUP]
    prev = prev_ref[...]
    acc = convb_ref[...] + x * convw_ref[CONV_WIDTH - 1:CONV_WIDTH, :]
    for sh in range(1, CONV_WIDTH):
        acc = acc + _shift_rows(x, prev, sh) * convw_ref[CONV_WIDTH - 1 - sh:CONV_WIDTH - sh, :]
    prev_ref[...] = x[c - 8:c]
    qk = _silu(acc)
    q_all = qk[:, 0:GROUP]
    k_all = qk[:, GROUP:2 * GROUP] * HEAD_DIM ** -0.5
    yield

    lane = lax.broadcasted_iota(jnp.int32, (c, LANES), 1)
    gb = pg_ref[0] + gbias_ref[...]
    gl = jnp.where(lane < N_HEADS, gb, jnp.where(lane < 2 * N_HEADS, _log_sigmoid(gb), 0.0))
    cum = _dot_3x(tri_ref[...], gl)
    mgl = mgl_ref[...]
    xm = cum + mgl
    colb = _dot_x3(xm, esel_ref[...])
    xm_t = xm.T
    gl_t = gl.T

    row = lax.broadcasted_iota(jnp.int32, (c, c), 0)
    col = lax.broadcasted_iota(jnp.int32, (c, c), 1)
    causal = col <= row
    dmat, w_inter, e_neg_m = [], [], []
    for h in range(N_HEADS):
        log_inter = colb[:, h * c:(h + 1) * c]
        logd = jnp.where(
            causal, log_inter + (gl_t[h:h + 1, :] - xm_t[N_HEADS + h:N_HEADS + h + 1, :]), NEG_BIG)
        m_t = jnp.maximum(jnp.max(logd, axis=1, keepdims=True), log_inter)
        dmat.append(jnp.exp(logd - m_t))
        w_inter.append(jnp.exp(log_inter - m_t))
        e_neg_m.append(jnp.exp(-m_t))
    yield

    cumcol = _dot_x3(cum, epf_ref[...])
    icol = _dot_x3(gl, epi_ref[...])
    c_end = cumcol[c - 1:c, :]
    m_old = m_ref[...]
    log_w = c_end - cumcol + icol
    m_new = jnp.maximum(c_end + m_old, jnp.max(log_w, axis=0, keepdims=True))
    decay = jnp.exp(c_end + m_old - m_new)
    kw_all = k_all * jnp.exp(log_w - m_new)
    n_old = n_ref[...]
    n_ref[...] = decay * n_old + jnp.sum(kw_all, axis=0, keepdims=True)
    m_ref[...] = m_new
    c_end_gl = cum[c - 1:c, :]
    log_w_gl = c_end_gl - cum + pltpu.roll(gl, N_HEADS, axis=1)
    m_new_gl = jnp.maximum(c_end_gl + mgl, jnp.max(log_w_gl, axis=0, keepdims=True))
    lane1 = lax.broadcasted_iota(jnp.int32, (1, LANES), 1)
    mgl_ref[...] = jnp.where((lane1 >= N_HEADS) & (lane1 < 2 * N_HEADS), m_new_gl, 0.0)
    yield

    lo = lane < HEAD_DIM
    pairs = range(N_PAIRS)
    sls = [slice(j * LANES, (j + 1) * LANES) for j in pairs]
    ones_full = jnp.ones((c, LANES), BF16)
    v1s = [jnp.concatenate(
        [pm_ref[0, :, 2 * GROUP + j * LANES:2 * GROUP + (j + 1) * LANES].astype(BF16), ones_full],
        axis=1) for j in pairs]
    cms = [cm_ref[j] for j in pairs]
    scores = [_dot_nt(_stack2(q_all[:, sls[j]], lo), k_all[:, sls[j]])
              * jnp.concatenate([dmat[2 * j], dmat[2 * j + 1]], axis=0) for j in pairs]
    pvr = [_dot(scores[j], v1s[j]) for j in pairs]
    qcs = [_dot(q_all[:, sls[j]], cms[j]) for j in pairs]
    qns = [_bsum(q_all[:, sls[j]] * n_old[:, sls[j]], ones_bd) for j in pairs]
    yield
    for j in pairs:
        cm_ref[j] = cms[j] * decay[:, sls[j]] + bd * _dot_tn(kw_all[:, sls[j]], v1s[j][:, 0:LANES])
    hhs = []
    for j in pairs:
        wi = jnp.where(lo, w_inter[2 * j], w_inter[2 * j + 1])
        num = jnp.where(lo, pvr[j][0:c, 0:LANES], pvr[j][c:2 * c, 0:LANES]) + wi * qcs[j]
        den = (jnp.where(lo, pvr[j][0:c, LANES:2 * LANES], pvr[j][c:2 * c, LANES:2 * LANES])
               + wi * qns[j])
        floor = jnp.where(lo, e_neg_m[2 * j], e_neg_m[2 * j + 1])
        hhs.append(num / jnp.maximum(jnp.abs(den), floor))
    mss = [_bsum(hhs[j] * hhs[j], ones_bd) * (1.0 / HEAD_DIM) for j in pairs]
    for j in pairs:
        oj = pm_ref[0, :, 3 * GROUP + j * LANES:3 * GROUP + (j + 1) * LANES]
        y = hhs[j] * lax.rsqrt(mss[j] + NORM_EPS) * normw_ref[:, sls[j]]
        o_ref[0, :, sls[j]] = (_sigmoid(oj) * y).astype(BF16)
    yield


def _mlstm_spec(p3, conv_w, conv_b, i_bias, f_bias, norm_w, consts):
    b, s, _ = p3.shape
    tri, esel, epf, epi, bd, ones_bd = consts
    gbias = jnp.zeros((1, LANES), F32).at[0, 0:N_HEADS].set(i_bias).at[0, N_HEADS:2 * N_HEADS].set(f_bias)
    args = [p3, p3, conv_w, conv_b.reshape(1, -1), gbias, norm_w.reshape(1, GROUP), tri, esel, epf,
            epi, bd, ones_bd]
    return dict(
        name="mlstm", body=_mlstm_body, args=args,
        in_specs=[_seq_spec(4 * GROUP, COL_ML // (4 * GROUP)), _seq_spec(LANES, COL_ML_GATES // LANES)]
        + [_full(a) for a in args[2:]],
        out_specs=[_seq_spec(GROUP)], out_shape=[_out_bf16(b, s)],
        scratch=[pltpu.VMEM((8, 2 * GROUP), F32),
                 pltpu.VMEM((N_PAIRS, LANES, LANES), F32),
                 pltpu.VMEM((1, GROUP), F32),
                 pltpu.VMEM((1, GROUP), F32),
                 pltpu.VMEM((1, LANES), F32)])


def _rwkv_body(first_layer, *refs):
    ts, c = MIX_TS, RWKV_C
    n_ch = ts // c
    if first_layer:
        (prkv_ref, plora_ref, mu_rkv_ref, mu_lora_ref, w0_ref, wup_ref, a0_ref, aup_ref, gup_ref,
         kk_ref, ka_ref, rk_ref, lnw_ref, lnb_ref, tri_ref, ones_ref,
         o_ref, vfirst_out_ref, prev_rkv_ref, prev_lora_ref, st_ref) = refs
    else:
        (prkv_ref, plora_ref, vfirst_ref, mu_rkv_ref, mu_lora_ref, w0_ref, wup_ref, a0_ref, aup_ref,
         gup_ref, kk_ref, ka_ref, rk_ref, lnw_ref, lnb_ref, v0_ref, vdown_ref, vup_ref, tri_ref,
         ones_ref, o_ref, prev_rkv_ref, prev_lora_ref, st_ref) = refs

    ones_bd = ones_ref[...]

    def token_mix(x_ref, prev_ref, mu_ref):
        x = x_ref[0]
        shifted = _shift_rows(x, prev_ref[...], 1)
        prev_ref[...] = x[ts - 8:ts]
        return x + (shifted - x) * mu_ref[...]

    rkv = token_mix(prkv_ref, prev_rkv_ref, mu_rkv_ref)
    lora = token_mix(plora_ref, prev_lora_ref, mu_lora_ref)
    r = rkv[:, 0:GROUP]
    k = rkv[:, GROUP:2 * GROUP]
    v = rkv[:, 2 * GROUP:3 * GROUP]
    wa = lora[:, 0:LANES]
    log_w = -RWKV_DECAY_SCALE * _sigmoid(w0_ref[...] + _dot(jnp.tanh(wa), wup_ref[...]))
    a = _sigmoid(a0_ref[...] + _dot(wa, aup_ref[...]))
    g = _dot(_sigmoid(lora[:, LANES:2 * LANES]), gup_ref[...])
    if first_layer:
        vfirst_out_ref[0] = v
    else:
        mix = _sigmoid(v0_ref[...] + _dot(_dot(v, vdown_ref[...]), vup_ref[...]))
        v = v + (vfirst_ref[0] - v) * mix

    kk_raw = k * kk_ref[...]
    k2 = k * (1.0 + (a - 1.0) * ka_ref[...])
    cum = _dot_3x(tri_ref[...], log_w)
    c_end = [cum[(ci + 1) * c - 1:(ci + 1) * c, :] for ci in range(n_ch)]
    c_end_rows = jnp.concatenate([jnp.broadcast_to(e, (c, GROUP)) for e in c_end], axis=0)
    r_g = r * jnp.exp(cum)
    e_prev = jnp.exp(cum - log_w)
    e_neg = jnp.exp(-cum)
    e_end = jnp.exp(c_end_rows - cum)
    rk_term = r * k2 * rk_ref[...]
    yield

    lane = lax.broadcasted_iota(jnp.int32, (c, LANES), 1)
    lo = lane < HEAD_DIM
    row = lax.broadcasted_iota(jnp.int32, (2 * c, 2 * c), 0)
    col = lax.broadcasted_iota(jnp.int32, (2 * c, 2 * c), 1)
    strict = col < row
    incl = col <= row
    eye = (col == row).astype(F32)

    kk_all, b_all = [], []
    for j in range(N_PAIRS):
        sl = slice(j * LANES, (j + 1) * LANES)
        kkr = kk_raw[:, sl]
        kk = kkr / jnp.maximum(jnp.sqrt(_bsum(kkr * kkr, ones_bd)), 1e-12)
        kk_all.append(kk)
        b_all.append(a[:, sl] * kk)

    insts = [(ci, j) for ci in range(n_ch) for j in range(N_PAIRS)]
    st2 = {}
    for ci, j in insts:
        rs = slice(ci * c, (ci + 1) * c)
        sl = slice(j * LANES, (j + 1) * LANES)
        st2[ci, j] = dict(
            r_t=_stack2(r_g[rs, sl], lo),
            kk_t=_stack2(kk_all[j][rs] * e_prev[rs, sl], lo),
            k_h=_stack2(k2[rs, sl] * e_neg[rs, sl], lo),
            b_h=_stack2(b_all[j][rs] * e_neg[rs, sl], lo),
            k_e=_stack2(k2[rs, sl] * e_end[rs, sl], lo),
            b_e=_stack2(b_all[j][rs] * e_end[rs, sl], lo),
            v2=_stack2(v[rs, sl], lo))
    yield
    for key in insts:
        d = st2[key]
        prod = _dot_nt(jnp.concatenate([d["kk_t"], d["r_t"]], axis=0),
                       jnp.concatenate([d["b_h"], d["k_h"]], axis=0))
        d["lmat"] = jnp.where(strict, prod[0:2 * c, 0:2 * c], 0.0)
        d["a_kk"] = jnp.where(strict, prod[0:2 * c, 2 * c:4 * c], 0.0)
        d["a_rb"] = jnp.where(incl, prod[2 * c:4 * c, 0:2 * c], 0.0)
        d["a_rk"] = jnp.where(incl, prod[2 * c:4 * c, 2 * c:4 * c], 0.0)
    yield
    for key in insts:
        d = st2[key]
        d["s"] = eye - d["lmat"]
        d["p"] = _dot(d["lmat"], d["lmat"])
    yield
    for _ in range(4):
        for key in insts:
            d = st2[key]
            both = _dot(jnp.concatenate([d["p"], d["s"]], axis=0), d["p"])
            d["p"] = both[0:2 * c]
            d["s"] = d["s"] + both[2 * c:4 * c]
        yield
    for key in insts:
        d = st2[key]
        d["s"] = d["s"] + _dot(d["s"], d["p"])
        d["akv"] = _dot(d["a_kk"], d["v2"])
    yield
    for key in insts:
        d = st2[key]
        tw = _dot(d["s"], jnp.concatenate([d["kk_t"], d["akv"]], axis=1))
        d["w"] = -tw[:, 0:LANES]
        d["u0"] = -tw[:, LANES:2 * LANES]
    yield
    for key in insts:
        d = st2[key]
        zero = jnp.zeros((2 * c, LANES), F32)
        rhs = jnp.concatenate([jnp.concatenate([d["v2"], zero], axis=1),
                               jnp.concatenate([d["u0"], d["w"]], axis=1)], axis=0)
        yq = _dot(jnp.concatenate([d["a_rk"], d["a_rb"]], axis=1), rhs)
        d["y0"] = yq[:, 0:LANES]
        d["q"] = d["r_t"] + yq[:, LANES:2 * LANES]
        d["m"] = _dot_tn(d["b_e"], d["w"])
        d["n_t"] = _dot_tn(jnp.concatenate([d["v2"], d["u0"]], axis=0),
                           jnp.concatenate([d["k_e"], d["b_e"]], axis=0))
    yield

    hts = [st_ref[j] for j in range(N_PAIRS)]
    ys = [[] for _ in range(N_PAIRS)]
    for ci in range(n_ch):
        for j in range(N_PAIRS):
            sl = slice(j * LANES, (j + 1) * LANES)
            d = st2[ci, j]
            ht = hts[j]
            y2 = _dot_nt(d["q"], ht) + d["y0"]
            hts[j] = ht * jnp.exp(c_end[ci][:, sl]) + _dot_nt(ht, d["m"]) + d["n_t"]
            ys[j].append(y2[0:c] + y2[c:2 * c])
    for j in range(N_PAIRS):
        st_ref[j] = hts[j]
    yield
    y_pairs = [jnp.concatenate(ys[j], axis=0) for j in range(N_PAIRS)]
    means = [_dot(y, ones_bd) * (1.0 / HEAD_DIM) for y in y_pairs]
    ycs = [y - m for y, m in zip(y_pairs, means)]
    vars_ = [_dot(yc * yc, ones_bd) * (1.0 / HEAD_DIM) for yc in ycs]
    for j in range(N_PAIRS):
        sl = slice(j * LANES, (j + 1) * LANES)
        yn = ycs[j] * lax.rsqrt(vars_[j] + RWKV_LN_EPS) * lnw_ref[:, sl] + lnb_ref[:, sl]
        bonus = _bsum(rk_term[:, sl], ones_bd) * v[:, sl]
        o_ref[0, :, sl] = ((yn + bonus) * g[:, sl]).astype(BF16)
    yield


def _rwkv_spec(p3, v_first, prm, consts):
    b, s, _ = p3.shape
    tri, ones_bd = consts
    first_layer = v_first is None
    args = [p3, p3]
    in_specs = [_seq_spec(3 * GROUP, COL_RW // (3 * GROUP)),
                _seq_spec(RWKV_LORA_COLS, COL_RW_LORA // RWKV_LORA_COLS)]
    if not first_layer:
        args.append(v_first)
        in_specs.append(_seq_spec(GROUP))
    consts_in = [prm[n] for n in ("mu_rkv", "mu_lora", "w0", "w_up", "a0", "a_up", "g_up", "k_k",
                                  "k_a", "r_k", "ln_w", "ln_b")]
    if not first_layer:
        consts_in += [prm["v0"], prm["v_down"], prm["v_up"]]
    consts_in += [tri, ones_bd]
    args += consts_in
    in_specs += [_full(a) for a in consts_in]
    out_specs = [_seq_spec(GROUP)]
    out_shape = [_out_bf16(b, s)]
    if first_layer:
        out_specs.append(_seq_spec(GROUP))
        out_shape.append(jax.ShapeDtypeStruct((b, s, GROUP), F32))
    return dict(
        name="rwkv", body=functools.partial(_rwkv_body, first_layer), args=args,
        in_specs=in_specs, out_specs=out_specs, out_shape=out_shape,
        scratch=[pltpu.VMEM((8, 3 * GROUP), F32),
                 pltpu.VMEM((8, RWKV_LORA_COLS), F32),
                 pltpu.VMEM((N_PAIRS, LANES, LANES), F32)])


def _pad_cols(w, n):
    return jnp.pad(w, ((0, 0), (0, n - w.shape[1])))


def _pad_rows(w, n):
    return jnp.pad(w, ((0, n - w.shape[0]), (0, 0)))


def _layout_w_in(w):
    g = GROUP
    ret, hg = w[:, 0:4 * g], w[:, 4 * g:8 * g]
    ml = w[:, 8 * g:12 * g + 2 * N_HEADS]
    rw = w[:, 12 * g + 2 * N_HEADS:]
    ml_main, ml_gates = ml[:, 0:4 * g], ml[:, 4 * g:]
    out = jnp.concatenate([ret, hg, ml_main, rw, ml_gates], axis=1)
    return _pad_cols(out, N_IN_PAD).astype(BF16)


def _rwkv_params(l, rwkv_mu, rwkv_w0, rwkv_w_up, rwkv_a0, rwkv_a_up, rwkv_g_up, rwkv_k_k, rwkv_k_a,
                 rwkv_r_k, rwkv_ln_w, rwkv_ln_b, rwkv_v0, rwkv_v_down, rwkv_v_up):
    g = GROUP
    mu = rwkv_mu[l]
    prm = {
        "mu_rkv": mu[0:3 * g].reshape(1, -1), "mu_lora": mu[3 * g:].reshape(1, -1),
        "w0": rwkv_w0[l].reshape(1, g), "a0": rwkv_a0[l].reshape(1, g),
        "w_up": _pad_rows(rwkv_w_up[l], LANES).astype(BF16),
        "a_up": jnp.concatenate([jnp.zeros_like(rwkv_w_up[l]), rwkv_a_up[l]], axis=0).astype(BF16),
        "g_up": rwkv_g_up[l].astype(BF16),
        "k_k": rwkv_k_k[l].reshape(1, g), "k_a": rwkv_k_a[l].reshape(1, g),
        "r_k": rwkv_r_k[l].reshape(1, g), "ln_w": rwkv_ln_w[l].reshape(1, g),
        "ln_b": rwkv_ln_b[l].reshape(1, g),
    }
    if l > 0:
        prm["v0"] = rwkv_v0[l - 1].reshape(1, g)
        prm["v_down"] = _pad_cols(rwkv_v_down[l - 1], LANES).astype(BF16)
        prm["v_up"] = _pad_rows(rwkv_v_up[l - 1], LANES).astype(BF16)
    return prm


def kernel(x, w_in, w_out, norm_pre_mix, norm_post_mix, norm_pre_ffn, norm_post_ffn, w_ffn_gate, w_ffn_up, w_ffn_down, hgrn_lb_logits, hgrn_norm_w, mlstm_conv_w, mlstm_conv_b, mlstm_i_bias, mlstm_f_bias, mlstm_norm_w, rwkv_mu, rwkv_w0, rwkv_w_up, rwkv_a0, rwkv_a_up, rwkv_g_up, rwkv_k_k, rwkv_k_a, rwkv_r_k, rwkv_ln_w, rwkv_ln_b, rwkv_v0, rwkv_v_down, rwkv_v_up):
    b, s, d = x.shape
    depth = w_in.shape[0]
    t = b * s
    g = GROUP

    lb_cum = jnp.cumsum(jax.nn.softmax(hgrn_lb_logits.astype(F32), axis=0), axis=0)
    lower_bounds = lb_cum - lb_cum[0]

    bd_f32 = jnp.asarray(_block_ones(), F32)
    ones_bd = jnp.asarray(_block_ones(), BF16)
    cos, sin = _rope_tables(s)
    ret_tables = (cos, sin) + _retention_tables() + (bd_f32, ones_bd)
    hgrn_consts = (jnp.asarray(_tri_incl(MIX_TS), BF16), bd_f32, ones_bd)
    mlstm_consts = _mlstm_tables() + (bd_f32, ones_bd)
    rwkv_consts = (jnp.asarray(_tri_incl(MIX_TS, RWKV_C), BF16), ones_bd)

    h = x.reshape(t, d)
    u = _prenorm(h, norm_pre_mix[0])
    v_first = None
    for l in range(depth):
        p3 = _in_proj(u, _layout_w_in(w_in[l])).reshape(b, s, N_IN_PAD)

        prm = _rwkv_params(l, rwkv_mu, rwkv_w0, rwkv_w_up, rwkv_a0, rwkv_a_up, rwkv_g_up, rwkv_k_k,
                           rwkv_k_a, rwkv_r_k, rwkv_ln_w, rwkv_ln_b, rwkv_v0, rwkv_v_down, rwkv_v_up)
        specs = {
            "ret": _ret_spec(p3, ret_tables),
            "hgrn": _hgrn_spec(p3, lower_bounds[l], hgrn_norm_w[l], hgrn_consts),
            "mlstm": _mlstm_spec(p3, mlstm_conv_w[l], mlstm_conv_b[l], mlstm_i_bias[l],
                                 mlstm_f_bias[l], mlstm_norm_w[l], mlstm_consts),
            "rwkv": _rwkv_spec(p3, v_first, prm, rwkv_consts),
        }
        mixed = {}
        for group in MIXER_GROUPS:
            mixed.update(_run_mixers([specs[name] for name, _ in group], b, s,
                                     [delay for _, delay in group]))
        if v_first is None:
            v_first = mixed["rwkv"][1]

        outs = [mixed[name][0].reshape(t, g) for name in ("ret", "hgrn", "mlstm", "rwkv")]
        h, u = _out_proj(outs, w_out[l].astype(BF16), h, norm_post_mix[l], norm_pre_ffn[l])
        w_pre_next = norm_pre_mix[l + 1] if l + 1 < depth else norm_pre_mix[l]
        h, u = _ffn(u, h, w_ffn_gate[l].astype(BF16), w_ffn_up[l].astype(BF16),
                    w_ffn_down[l].astype(BF16), norm_post_ffn[l], w_pre_next)
    return h.reshape(b, s, d)
```

```python
import functools

import numpy as np
import jax
import jax.numpy as jnp
from jax import lax
from jax.experimental import pallas as pl
from jax.experimental.pallas import tpu as pltpu

F32 = jnp.float32
BF16 = jnp.bfloat16

HEAD_DIM = 64
N_HEADS = 8
GROUP = N_HEADS * HEAD_DIM
LANES = 128
N_PAIRS = GROUP // LANES
NORM_EPS = 1e-6
ROPE_BASE = 10000.0
RWKV_DECAY_SCALE = 0.6065306597126334
RWKV_LN_EPS = 64e-5
RWKV_LORA_COLS = 256
CONV_WIDTH = 4
NEG_BIG = -1e30

MIX_TS = 128
RWKV_C = 64
MIXER_GROUPS = ((("rwkv", 0), ("hgrn", 0)), (("mlstm", 0), ("ret", 0)))

COL_RET = 0
COL_HGRN = 4 * GROUP
COL_ML = 8 * GROUP
COL_RW = 12 * GROUP
COL_RW_LORA = 15 * GROUP
COL_ML_GATES = 15 * GROUP + RWKV_LORA_COLS
N_IN_PAD = 16 * GROUP

VMEM_LIMIT = 56 * 1024 * 1024


def _cparams(n_axes):
    return pltpu.CompilerParams(dimension_semantics=("arbitrary",) * n_axes,
                                vmem_limit_bytes=VMEM_LIMIT)


def _dot(a, b):
    return jnp.dot(a.astype(BF16), b.astype(BF16), preferred_element_type=F32)


def _dot_nt(a, b):
    return lax.dot_general(a.astype(BF16), b.astype(BF16), (((1,), (1,)), ((), ())),
                           preferred_element_type=F32)


def _dot_tn(a, b):
    return lax.dot_general(a.astype(BF16), b.astype(BF16), (((0,), (0,)), ((), ())),
                           preferred_element_type=F32)


def _split3(x):
    hi = x.astype(BF16)
    r1 = x - hi.astype(F32)
    mid = r1.astype(BF16)
    lo = (r1 - mid.astype(F32)).astype(BF16)
    return hi, mid, lo


def _dot_x3(x, sel):
    hi, mid, lo = _split3(x)
    d = lambda t: jnp.dot(t, sel, preferred_element_type=F32)
    return d(hi) + d(mid) + d(lo)


def _dot_3x(sel, x):
    hi, mid, lo = _split3(x)
    d = lambda t: jnp.dot(sel, t, preferred_element_type=F32)
    return d(hi) + d(mid) + d(lo)


def _bsum(x, ones_bd):
    hi = x.astype(BF16)
    lo = (x - hi.astype(F32)).astype(BF16)
    return (jnp.dot(hi, ones_bd, preferred_element_type=F32)
            + jnp.dot(lo, ones_bd, preferred_element_type=F32))


def _sigmoid(x):
    return 1.0 / (1.0 + jnp.exp(-x))


def _silu(x):
    return x * _sigmoid(x)


def _log_sigmoid(x):
    return jnp.minimum(x, 0.0) - jnp.log1p(jnp.exp(-jnp.abs(x)))


def _rms_rows(x, w):
    return x * lax.rsqrt(jnp.mean(x * x, axis=-1, keepdims=True) + NORM_EPS) * w


def _shift_rows(x, prev8, k):
    rolled = pltpu.roll(x, k, axis=0)
    row8 = lax.broadcasted_iota(jnp.int32, (8, x.shape[1]), 0)
    top = jnp.where(row8 < k, pltpu.roll(prev8, k, axis=0), rolled[0:8])
    return jnp.concatenate([top, rolled[8:]], axis=0)


def _stack2(x, lo):
    return jnp.concatenate([jnp.where(lo, x, 0.0), jnp.where(lo, 0.0, x)], axis=0)


def _rms_kernel(x_ref, w_ref, u_ref):
    u_ref[...] = _rms_rows(x_ref[...], w_ref[...]).astype(BF16)


def _prenorm(x2d, w, tm=512):
    t, d = x2d.shape
    return pl.pallas_call(
        _rms_kernel,
        grid=(t // tm,),
        in_specs=[pl.BlockSpec((tm, d), lambda i: (i, 0)),
                  pl.BlockSpec((1, d), lambda i: (0, 0))],
        out_specs=pl.BlockSpec((tm, d), lambda i: (i, 0)),
        out_shape=jax.ShapeDtypeStruct((t, d), BF16),
        compiler_params=_cparams(1),
        name="prenorm",
    )(x2d, w.reshape(1, d))


def _mm_kernel(u_ref, w_ref, o_ref):
    o_ref[...] = jnp.dot(u_ref[...], w_ref[...], preferred_element_type=F32)


def _in_proj(u, w, tm=2048, tn=1024):
    t, d = u.shape
    n = w.shape[1]
    tm = min(tm, t)
    return pl.pallas_call(
        _mm_kernel,
        grid=(t // tm, n // tn),
        in_specs=[pl.BlockSpec((tm, d), lambda i, j: (i, 0)),
                  pl.BlockSpec((d, tn), lambda i, j: (0, j))],
        out_specs=pl.BlockSpec((tm, tn), lambda i, j: (i, j)),
        out_shape=jax.ShapeDtypeStruct((t, n), F32),
        compiler_params=_cparams(2),
        name="in_proj",
    )(u, w)


def _outproj_kernel(o0_ref, o1_ref, o2_ref, o3_ref, w_ref, h_ref, wpost_ref, wpre_ref,
                    hout_ref, u_ref):
    acc = jnp.dot(o0_ref[...], w_ref[0 * GROUP:1 * GROUP, :], preferred_element_type=F32)
    acc += jnp.dot(o1_ref[...], w_ref[1 * GROUP:2 * GROUP, :], preferred_element_type=F32)
    acc += jnp.dot(o2_ref[...], w_ref[2 * GROUP:3 * GROUP, :], preferred_element_type=F32)
    acc += jnp.dot(o3_ref[...], w_ref[3 * GROUP:4 * GROUP, :], preferred_element_type=F32)
    h = h_ref[...] + _rms_rows(acc, wpost_ref[...])
    hout_ref[...] = h
    u_ref[...] = _rms_rows(h, wpre_ref[...]).astype(BF16)


def _out_proj(outs, w, h, w_post, w_pre, tm=512):
    t, d = h.shape
    tm = min(tm, t)
    row = lambda i: (i, 0)
    fixed = lambda i: (0, 0)
    return pl.pallas_call(
        _outproj_kernel,
        grid=(t // tm,),
        in_specs=[pl.BlockSpec((tm, GROUP), row)] * 4 + [
            pl.BlockSpec((d, d), fixed),
            pl.BlockSpec((tm, d), row),
            pl.BlockSpec((1, d), fixed),
            pl.BlockSpec((1, d), fixed)],
        out_specs=[pl.BlockSpec((tm, d), row), pl.BlockSpec((tm, d), row)],
        out_shape=[jax.ShapeDtypeStruct((t, d), F32), jax.ShapeDtypeStruct((t, d), BF16)],
        compiler_params=_cparams(1),
        name="out_proj",
    )(*outs, w, h, w_post.reshape(1, d), w_pre.reshape(1, d))


def _ffn_up_kernel(u_ref, wg_ref, wu_ref, a_ref):
    u = u_ref[...]
    g = jnp.dot(u, wg_ref[...].astype(BF16), preferred_element_type=F32)
    up = jnp.dot(u, wu_ref[...].astype(BF16), preferred_element_type=F32)
    a_ref[...] = (_silu(g) * up).astype(BF16)


def _ffn_down_kernel(a_ref, wd_ref, h_ref, wpost_ref, wpre_ref, hout_ref, u_next_ref):
    kk = pl.program_id(1)

    @pl.when(kk == 0)
    def _():
        hout_ref[...] = jnp.zeros_like(hout_ref)

    hout_ref[...] += jnp.dot(a_ref[...], wd_ref[...], preferred_element_type=F32)

    @pl.when(kk == pl.num_programs(1) - 1)
    def _():
        h = h_ref[...] + _rms_rows(hout_ref[...], wpost_ref[...])
        hout_ref[...] = h
        u_next_ref[...] = _rms_rows(h, wpre_ref[...]).astype(BF16)


def _ffn(u, h, wg, wu, wd, w_post, w_pre_next, tm_up=2048, tf=512, tm_down=1024, tk=512):
    t, d = h.shape
    f = wg.shape[1]
    tm_up, tm_down = min(tm_up, t), min(tm_down, t)
    act = pl.pallas_call(
        _ffn_up_kernel,
        grid=(t // tm_up, f // tf),
        in_specs=[pl.BlockSpec((tm_up, d), lambda i, j: (i, 0)),
                  pl.BlockSpec((d, tf), lambda i, j: (0, j)),
                  pl.BlockSpec((d, tf), lambda i, j: (0, j))],
        out_specs=pl.BlockSpec((tm_up, tf), lambda i, j: (i, j)),
        out_shape=jax.ShapeDtypeStruct((t, f), BF16),
        compiler_params=_cparams(2),
        name="ffn_up",
    )(u, wg, wu)
    row = lambda i, k: (i, 0)
    fixed = lambda i, k: (0, 0)
    return pl.pallas_call(
        _ffn_down_kernel,
        grid=(t // tm_down, f // tk),
        in_specs=[pl.BlockSpec((tm_down, tk), lambda i, k: (i, k)),
                  pl.BlockSpec((tk, d), lambda i, k: (k, 0)),
                  pl.BlockSpec((tm_down, d), row),
                  pl.BlockSpec((1, d), fixed),
                  pl.BlockSpec((1, d), fixed)],
        out_specs=[pl.BlockSpec((tm_down, d), row), pl.BlockSpec((tm_down, d), row)],
        out_shape=[jax.ShapeDtypeStruct((t, d), F32), jax.ShapeDtypeStruct((t, d), BF16)],
        compiler_params=_cparams(2),
        name="ffn_down",
    )(act, wd, h, w_post.reshape(1, d), w_pre_next.reshape(1, d))


def _block_ones():
    i = np.arange(LANES)
    return (i[:, None] // HEAD_DIM == i[None, :] // HEAD_DIM).astype(np.float32)


def _tri_incl(c, block=None):
    i = np.arange(c)
    m = i[None, :] <= i[:, None]
    if block is not None:
        m = m & (i[None, :] // block == i[:, None] // block)
    return m.astype(np.float32)


def _rope_tables(seq):
    half = HEAD_DIM // 2
    lane = np.arange(LANES)
    inv_freq = ROPE_BASE ** (-jnp.arange(half, dtype=F32) / half)
    ang = jnp.arange(seq, dtype=F32)[:, None] * inv_freq[None, :]
    cos, sin = jnp.cos(ang), jnp.sin(ang)
    fidx = (lane % HEAD_DIM) % half
    sign = np.where((lane % HEAD_DIM) < half, -1.0, 1.0).astype(np.float32)
    return cos[:, fidx], sin[:, fidx] * sign


def _retention_tables():
    c = MIX_TS
    log_gamma = jnp.log1p(-jnp.exp2(-5.0 - jnp.arange(N_HEADS, dtype=F32)))
    idx = jnp.arange(c, dtype=F32)
    rel = idx[:, None] - idx[None, :]
    intra = jnp.where(rel >= 0, jnp.exp(log_gamma[:, None, None] * jnp.maximum(rel, 0.0)), 0.0)
    q_dec = jnp.exp(log_gamma[:, None] * (idx + 1.0))
    k_dec = jnp.exp(log_gamma[:, None] * (c - 1.0 - idx))
    chunk_dec = jnp.exp(log_gamma * c)
    per_lane = lambda t: jnp.repeat(t.T, HEAD_DIM, axis=1)
    bd = jnp.asarray(_block_ones())
    lane_head = np.arange(LANES) // HEAD_DIM
    sdec = jnp.stack([bd * chunk_dec[2 * j + lane_head][None, :] for j in range(N_PAIRS)])
    intra2 = intra.reshape(N_PAIRS, 2 * c, c)
    return intra2, per_lane(q_dec), per_lane(k_dec), sdec


def _mlstm_tables():
    c = MIX_TS
    esel = np.zeros((LANES, N_HEADS * c), np.float32)
    epf = np.zeros((LANES, GROUP), np.float32)
    epi = np.zeros((LANES, GROUP), np.float32)
    for h in range(N_HEADS):
        esel[N_HEADS + h, h * c:(h + 1) * c] = 1.0
        epf[N_HEADS + h, h * HEAD_DIM:(h + 1) * HEAD_DIM] = 1.0
        epi[h, h * HEAD_DIM:(h + 1) * HEAD_DIM] = 1.0
    return (jnp.asarray(_tri_incl(c), BF16), jnp.asarray(esel, BF16), jnp.asarray(epf, BF16),
            jnp.asarray(epi, BF16))


def _fixed(ndim):
    return lambda i, t: (0,) * ndim


def _full(a):
    return pl.BlockSpec(a.shape, _fixed(a.ndim))


def _seq_spec(width, col_block=0):
    return pl.BlockSpec((1, MIX_TS, width), lambda i, t: (i, t, col_block))


def _mixer_kernel(bodies, n_in, n_out, n_scr, delays, *refs):
    ins, outs, scrs = [], [], []
    pos = 0
    for group, counts in ((ins, n_in), (outs, n_out), (scrs, n_scr)):
        for n in counts:
            group.append(refs[pos:pos + n])
            pos += n

    @pl.when(pl.program_id(1) == 0)
    def _():
        for scr in scrs:
            for ref in scr:
                ref[...] = jnp.zeros_like(ref)

    active = [body(*i, *o, *s) for body, i, o, s in zip(bodies, ins, outs, scrs)]
    waits = list(delays)
    while active:
        for gen in list(active):
            k = active.index(gen)
            if waits[k] > 0:
                waits[k] -= 1
            elif next(gen, "done") == "done":
                active.pop(k)
                waits.pop(k)


def _run_mixers(specs, b, s, delays=None):
    kern = functools.partial(_mixer_kernel, tuple(sp["body"] for sp in specs),
                             tuple(len(sp["args"]) for sp in specs),
                             tuple(len(sp["out_shape"]) for sp in specs),
                             tuple(len(sp["scratch"]) for sp in specs),
                             tuple(delays) if delays else (0,) * len(specs))
    flat = lambda key: [x for sp in specs for x in sp[key]]
    res = pl.pallas_call(
        kern,
        grid=(b, s // MIX_TS),
        in_specs=flat("in_specs"),
        out_specs=flat("out_specs"),
        out_shape=flat("out_shape"),
        scratch_shapes=flat("scratch"),
        compiler_params=_cparams(2),
        name="mix_" + "_".join(sp["name"] for sp in specs),
    )(*flat("args"))
    outs, pos = {}, 0
    for sp in specs:
        n = len(sp["out_shape"])
        outs[sp["name"]] = res[pos:pos + n]
        pos += n
    return outs


def _out_bf16(b, s):
    return jax.ShapeDtypeStruct((b, s, GROUP), BF16)


def _ret_body(p_ref, cos_ref, sin_ref, intra_ref, qdec_ref, kdec_ref, sdec_ref, bd_ref,
              ones_ref, o_ref, st_ref):
    c = MIX_TS
    cos = cos_ref[...]
    sin = sin_ref[...]
    lane = lax.broadcasted_iota(jnp.int32, (c, LANES), 1)
    first_half = (lane & (HEAD_DIM - 1)) < HEAD_DIM // 2
    lo = lane < HEAD_DIM
    ones_bd = ones_ref[...]
    bd = bd_ref[...]

    def rot(t):
        swapped = jnp.where(first_half, pltpu.roll(t, LANES - HEAD_DIM // 2, axis=1),
                            pltpu.roll(t, HEAD_DIM // 2, axis=1))
        return t * cos + swapped * sin

    pairs = range(N_PAIRS)
    sls = [slice(j * LANES, (j + 1) * LANES) for j in pairs]
    qs = [rot(p_ref[0, :, j * LANES:(j + 1) * LANES]) for j in pairs]
    ks = [rot(p_ref[0, :, GROUP + j * LANES:GROUP + (j + 1) * LANES]) * HEAD_DIM ** -0.5
          for j in pairs]
    vbs = [p_ref[0, :, 2 * GROUP + j * LANES:2 * GROUP + (j + 1) * LANES].astype(BF16)
           for j in pairs]
    sts = [st_ref[j] for j in pairs]
    yield
    scores = [_dot_nt(_stack2(qs[j], lo), ks[j]) * intra_ref[j] for j in pairs]
    pvs = [_dot(scores[j], vbs[j]) for j in pairs]
    inters = [_dot(qs[j] * qdec_ref[:, sls[j]], sts[j]) for j in pairs]
    outs = [jnp.where(lo, pvs[j][0:c], pvs[j][c:2 * c]) + inters[j] for j in pairs]
    yield
    for j in pairs:
        st_ref[j] = sdec_ref[j] * sts[j] + bd * _dot_tn(ks[j] * kdec_ref[:, sls[j]], vbs[j])
    mss = [_bsum(outs[j] * outs[j], ones_bd) * (1.0 / HEAD_DIM) for j in pairs]
    for j in pairs:
        g = p_ref[0, :, 3 * GROUP + j * LANES:3 * GROUP + (j + 1) * LANES]
        o_ref[0, :, sls[j]] = (outs[j] * lax.rsqrt(mss[j] + NORM_EPS) * _silu(g)).astype(BF16)
    yield


def _ret_spec(p3, tables):
    b, s, _ = p3.shape
    c = MIX_TS
    cos, sin, intra, qdec, kdec, sdec, bd, ones_bd = tables
    tab_spec = pl.BlockSpec((c, LANES), lambda i, t: (t, 0))
    return dict(
        name="ret", body=_ret_body,
        args=[p3, cos, sin, intra, qdec, kdec, sdec, bd, ones_bd],
        in_specs=[_seq_spec(4 * GROUP, COL_RET // (4 * GROUP)), tab_spec, tab_spec,
                  _full(intra), _full(qdec), _full(kdec), _full(sdec), _full(bd), _full(ones_bd)],
        out_specs=[_seq_spec(GROUP)], out_shape=[_out_bf16(b, s)],
        scratch=[pltpu.VMEM((N_PAIRS, LANES, LANES), F32)])


def _level_ref(cum, h):
    ts, n = cum.shape
    if h >= 8:
        return jnp.concatenate(
            [jnp.broadcast_to(cum[b0 + h - 1:b0 + h, :], (2 * h, n)) for b0 in range(0, ts, 2 * h)],
            axis=0)
    cum3 = cum.reshape(ts // 8, 8, n)
    brow = lambda i: jnp.broadcast_to(cum3[:, i:i + 1, :], cum3.shape).reshape(ts, n)
    sub = lax.broadcasted_iota(jnp.int32, (ts, n), 0) & 7
    if h == 4:
        return brow(3)
    if h == 2:
        return jnp.where(sub < 4, brow(1), brow(5))
    return jnp.where(sub < 2, brow(0), jnp.where(sub < 4, brow(2), jnp.where(sub < 6, brow(4), brow(6))))


def _hgrn_body(p_ref, loglb_ref, log1mlb_ref, onemlb_ref, normw_ref, tri_ref, bd_ref, ones_ref,
               o_ref, st_ref):
    ts = MIX_TS
    pf = p_ref[0, :, GROUP:2 * GROUP]
    q = _silu(p_ref[0, :, 0:GROUP]) * HEAD_DIM ** -0.5
    e_f = jnp.exp(-jnp.abs(pf))
    k = onemlb_ref[...] * (jnp.where(pf >= 0.0, e_f, 1.0) / (1.0 + e_f))
    a = loglb_ref[...]
    bb = log1mlb_ref[...] + (jnp.minimum(pf, 0.0) - jnp.log1p(e_f))
    log_f = jnp.maximum(a, bb) + jnp.log1p(jnp.exp(-jnp.abs(a - bb)))
    cum = _dot_3x(tri_ref[...], log_f)
    yield

    ones_bd = ones_ref[...]
    bd = bd_ref[...]
    lo = lax.broadcasted_iota(jnp.int32, (ts, LANES), 1) < HEAD_DIM
    row2 = lax.broadcasted_iota(jnp.int32, (2 * ts, ts), 0) & (ts - 1)
    col2 = lax.broadcasted_iota(jnp.int32, (2 * ts, ts), 1)
    diff2 = jnp.where(col2 < row2, row2 ^ col2, 0)

    lo4 = (lax.broadcasted_iota(jnp.int32, (ts, GROUP), 1) & HEAD_DIM) == 0
    q_a = jnp.where(lo4, q, 0.0)
    q_b = q - q_a
    scores = [None] * N_PAIRS
    h = ts // 2
    while h >= 1:
        e = jnp.exp(-jnp.abs(cum - _level_ref(cum, h)))
        qa_l, qb_l, k_l = q_a * e, q_b * e, k * e
        level = (diff2 >> (h.bit_length() - 1)) == 1
        for j in range(N_PAIRS):
            sl = slice(j * LANES, (j + 1) * LANES)
            sc = _dot_nt(jnp.concatenate([qa_l[:, sl], qb_l[:, sl]], axis=0), k_l[:, sl])
            scores[j] = jnp.where(level, sc, 0.0 if scores[j] is None else scores[j])
        h //= 2
        yield

    q_in = q * jnp.exp(cum)
    c_end = cum[ts - 1:ts, :]
    k_out = k * jnp.exp(c_end - cum)
    diag = q * k
    for j in range(N_PAIRS):
        sl = slice(j * LANES, (j + 1) * LANES)
        vj = p_ref[0, :, 2 * GROUP + j * LANES:2 * GROUP + (j + 1) * LANES]
        gj = p_ref[0, :, 3 * GROUP + j * LANES:3 * GROUP + (j + 1) * LANES]
        pv = _dot(scores[j], vj)
        st_t = st_ref[j]
        out = (jnp.where(lo, pv[0:ts], pv[ts:2 * ts]) + _bsum(diag[:, sl], ones_bd) * vj
               + _dot_nt(q_in[:, sl], st_t))
        st_ref[j] = st_t * jnp.exp(c_end[:, sl]) + bd * _dot_tn(vj, k_out[:, sl])
        ms = _bsum(out * out, ones_bd) * (1.0 / HEAD_DIM)
        y = out * lax.rsqrt(ms + NORM_EPS) * normw_ref[:, sl]
        o_ref[0, :, sl] = (y * _sigmoid(gj)).astype(BF16)
        yield


def _hgrn_spec(p3, lb, norm_w, consts):
    b, s, _ = p3.shape
    tri, bd, ones_bd = consts
    row = lambda v: v.reshape(1, GROUP)
    args = [p3, row(jnp.log(lb)), row(jnp.log1p(-lb)), row(1.0 - lb), row(norm_w), tri, bd, ones_bd]
    return dict(
        name="hgrn", body=_hgrn_body, args=args,
        in_specs=[_seq_spec(4 * GROUP, COL_HGRN // (4 * GROUP))] + [_full(a) for a in args[1:]],
        out_specs=[_seq_spec(GROUP)], out_shape=[_out_bf16(b, s)],
        scratch=[pltpu.VMEM((N_PAIRS, LANES, LANES), F32)])


def _mlstm_body(pm_ref, pg_ref, convw_ref, convb_ref, gbias_ref, normw_ref, tri_ref, esel_ref,
                epf_ref, epi_ref, bd_ref, ones_ref, o_ref,
                prev_ref, cm_ref, n_ref, m_ref, mgl_ref):
    c = MIX_TS
    ones_bd = ones_ref[...]
    bd = bd_ref[...]

    x = pm_ref[0, :, 0:2 * GROUP]
    prev = prev_ref[...]
    acc = convb_ref[...] + x * convw_ref[CONV_WIDTH - 1:CONV_WIDTH, :]
    for sh in range(1, CONV_WIDTH):
        acc = acc + _shift_rows(x, prev, sh) * convw_ref[CONV_WIDTH - 1 - sh:CONV_WIDTH - sh, :]
    prev_ref[...] = x[c - 8:c]
    qk = _silu(acc)
    q_all = qk[:, 0:GROUP]
    k_all = qk[:, GROUP:2 * GROUP] * HEAD_DIM ** -0.5
    yield

    lane = lax.broadcasted_iota(jnp.int32, (c, LANES), 1)
    gb = pg_ref[0] + gbias_ref[...]
    gl = jnp.where(lane < N_HEADS, gb, jnp.where(lane < 2 * N_HEADS, _log_sigmoid(gb), 0.0))
    cum = _dot_3x(tri_ref[...], gl)
    mgl = mgl_ref[...]
    xm = cum + mgl
    colb = _dot_x3(xm, esel_ref[...])
    xm_t = xm.T
    gl_t = gl.T

    row = lax.broadcasted_iota(jnp.int32, (c, c), 0)
    col = lax.broadcasted_iota(jnp.int32, (c, c), 1)
    causal = col <= row
    dmat, w_inter, e_neg_m = [], [], []
    for h in range(N_HEADS):
        log_inter = colb[:, h * c:(h + 1) * c]
        logd = jnp.where(
            causal, log_inter + (gl_t[h:h + 1, :] - xm_t[N_HEADS + h:N_HEADS + h + 1, :]), NEG_BIG)
        m_t = jnp.maximum(jnp.max(logd, axis=1, keepdims=True), log_inter)
        dmat.append(jnp.exp(logd - m_t))
        w_inter.append(jnp.exp(log_inter - m_t))
        e_neg_m.append(jnp.exp(-m_t))
    yield

    cumcol = _dot_x3(cum, epf_ref[...])
    icol = _dot_x3(gl, epi_ref[...])
    c_end = cumcol[c - 1:c, :]
    m_old = m_ref[...]
    log_w = c_end - cumcol + icol
    m_new = jnp.maximum(c_end + m_old, jnp.max(log_w, axis=0, keepdims=True))
    decay = jnp.exp(c_end + m_old - m_new)
    kw_all = k_all * jnp.exp(log_w - m_new)
    n_old = n_ref[...]
    n_ref[...] = decay * n_old + jnp.sum(kw_all, axis=0, keepdims=True)
    m_ref[...] = m_new
    c_end_gl = cum[c - 1:c, :]
    log_w_gl = c_end_gl - cum + pltpu.roll(gl, N_HEADS, axis=1)
    m_new_gl = jnp.maximum(c_end_gl + mgl, jnp.max(log_w_gl, axis=0, keepdims=True))
    lane1 = lax.broadcasted_iota(jnp.int32, (1, LANES), 1)
    mgl_ref[...] = jnp.where((lane1 >= N_HEADS) & (lane1 < 2 * N_HEADS), m_new_gl, 0.0)
    yield

    lo = lane < HEAD_DIM
    pairs = range(N_PAIRS)
    sls = [slice(j * LANES, (j + 1) * LANES) for j in pairs]
    ones_full = jnp.ones((c, LANES), BF16)
    v1s = [jnp.concatenate(
        [pm_ref[0, :, 2 * GROUP + j * LANES:2 * GROUP + (j + 1) * LANES].astype(BF16), ones_full],
        axis=1) for j in pairs]
    cms = [cm_ref[j] for j in pairs]
    scores = [_dot_nt(_stack2(q_all[:, sls[j]], lo), k_all[:, sls[j]])
              * jnp.concatenate([dmat[2 * j], dmat[2 * j + 1]], axis=0) for j in pairs]
    pvr = [_dot(scores[j], v1s[j]) for j in pairs]
    qcs = [_dot(q_all[:, sls[j]], cms[j]) for j in pairs]
    qns = [_bsum(q_all[:, sls[j]] * n_old[:, sls[j]], ones_bd) for j in pairs]
    yield
    for j in pairs:
        cm_ref[j] = cms[j] * decay[:, sls[j]] + bd * _dot_tn(kw_all[:, sls[j]], v1s[j][:, 0:LANES])
    hhs = []
    for j in pairs:
        wi = jnp.where(lo, w_inter[2 * j], w_inter[2 * j + 1])
        num = jnp.where(lo, pvr[j][0:c, 0:LANES], pvr[j][c:2 * c, 0:LANES]) + wi * qcs[j]
        den = (jnp.where(lo, pvr[j][0:c, LANES:2 * LANES], pvr[j][c:2 * c, LANES:2 * LANES])
               + wi * qns[j])
        floor = jnp.where(lo, e_neg_m[2 * j], e_neg_m[2 * j + 1])
        hhs.append(num / jnp.maximum(jnp.abs(den), floor))
    mss = [_bsum(hhs[j] * hhs[j], ones_bd) * (1.0 / HEAD_DIM) for j in pairs]
    for j in pairs:
        oj = pm_ref[0, :, 3 * GROUP + j * LANES:3 * GROUP + (j + 1) * LANES]
        y = hhs[j] * lax.rsqrt(mss[j] + NORM_EPS) * normw_ref[:, sls[j]]
        o_ref[0, :, sls[j]] = (_sigmoid(oj) * y).astype(BF16)
    yield


def _mlstm_spec(p3, conv_w, conv_b, i_bias, f_bias, norm_w, consts):
    b, s, _ = p3.shape
    tri, esel, epf, epi, bd, ones_bd = consts
    gbias = jnp.zeros((1, LANES), F32).at[0, 0:N_HEADS].set(i_bias).at[0, N_HEADS:2 * N_HEADS].set(f_bias)
    args = [p3, p3, conv_w, conv_b.reshape(1, -1), gbias, norm_w.reshape(1, GROUP), tri, esel, epf,
            epi, bd, ones_bd]
    return dict(
        name="mlstm", body=_mlstm_body, args=args,
        in_specs=[_seq_spec(4 * GROUP, COL_ML // (4 * GROUP)), _seq_spec(LANES, COL_ML_GATES // LANES)]
        + [_full(a) for a in args[2:]],
        out_specs=[_seq_spec(GROUP)], out_shape=[_out_bf16(b, s)],
        scratch=[pltpu.VMEM((8, 2 * GROUP), F32),
                 pltpu.VMEM((N_PAIRS, LANES, LANES), F32),
                 pltpu.VMEM((1, GROUP), F32),
                 pltpu.VMEM((1, GROUP), F32),
                 pltpu.VMEM((1, LANES), F32)])


def _rwkv_body(first_layer, *refs):
    ts, c = MIX_TS, RWKV_C
    n_ch = ts // c
    if first_layer:
        (prkv_ref, plora_ref, mu_rkv_ref, mu_lora_ref, w0_ref, wup_ref, a0_ref, aup_ref, gup_ref,
         kk_ref, ka_ref, rk_ref, lnw_ref, lnb_ref, tri_ref, ones_ref,
         o_ref, vfirst_out_ref, prev_rkv_ref, prev_lora_ref, st_ref) = refs
    else:
        (prkv_ref, plora_ref, vfirst_ref, mu_rkv_ref, mu_lora_ref, w0_ref, wup_ref, a0_ref, aup_ref,
         gup_ref, kk_ref, ka_ref, rk_ref, lnw_ref, lnb_ref, v0_ref, vdown_ref, vup_ref, tri_ref,
         ones_ref, o_ref, prev_rkv_ref, prev_lora_ref, st_ref) = refs

    ones_bd = ones_ref[...]

    def token_mix(x_ref, prev_ref, mu_ref):
        x = x_ref[0]
        shifted = _shift_rows(x, prev_ref[...], 1)
        prev_ref[...] = x[ts - 8:ts]
        return x + (shifted - x) * mu_ref[...]

    rkv = token_mix(prkv_ref, prev_rkv_ref, mu_rkv_ref)
    lora = token_mix(plora_ref, prev_lora_ref, mu_lora_ref)
    r = rkv[:, 0:GROUP]
    k = rkv[:, GROUP:2 * GROUP]
    v = rkv[:, 2 * GROUP:3 * GROUP]
    wa = lora[:, 0:LANES]
    log_w = -RWKV_DECAY_SCALE * _sigmoid(w0_ref[...] + _dot(jnp.tanh(wa), wup_ref[...]))
    a = _sigmoid(a0_ref[...] + _dot(wa, aup_ref[...]))
    g = _dot(_sigmoid(lora[:, LANES:2 * LANES]), gup_ref[...])
    if first_layer:
        vfirst_out_ref[0] = v
    else:
        mix = _sigmoid(v0_ref[...] + _dot(_dot(v, vdown_ref[...]), vup_ref[...]))
        v = v + (vfirst_ref[0] - v) * mix

    kk_raw = k * kk_ref[...]
    k2 = k * (1.0 + (a - 1.0) * ka_ref[...])
    cum = _dot_3x(tri_ref[...], log_w)
    c_end = [cum[(ci + 1) * c - 1:(ci + 1) * c, :] for ci in range(n_ch)]
    c_end_rows = jnp.concatenate([jnp.broadcast_to(e, (c, GROUP)) for e in c_end], axis=0)
    r_g = r * jnp.exp(cum)
    e_prev = jnp.exp(cum - log_w)
    e_neg = jnp.exp(-cum)
    e_end = jnp.exp(c_end_rows - cum)
    rk_term = r * k2 * rk_ref[...]
    yield

    lane = lax.broadcasted_iota(jnp.int32, (c, LANES), 1)
    lo = lane < HEAD_DIM
    row = lax.broadcasted_iota(jnp.int32, (2 * c, 2 * c), 0)
    col = lax.broadcasted_iota(jnp.int32, (2 * c, 2 * c), 1)
    strict = col < row
    incl = col <= row
    eye = (col == row).astype(F32)

    kk_all, b_all = [], []
    for j in range(N_PAIRS):
        sl = slice(j * LANES, (j + 1) * LANES)
        kkr = kk_raw[:, sl]
        kk = kkr / jnp.maximum(jnp.sqrt(_bsum(kkr * kkr, ones_bd)), 1e-12)
        kk_all.append(kk)
        b_all.append(a[:, sl] * kk)

    insts = [(ci, j) for ci in range(n_ch) for j in range(N_PAIRS)]
    st2 = {}
    for ci, j in insts:
        rs = slice(ci * c, (ci + 1) * c)
        sl = slice(j * LANES, (j + 1) * LANES)
        st2[ci, j] = dict(
            r_t=_stack2(r_g[rs, sl], lo),
            kk_t=_stack2(kk_all[j][rs] * e_prev[rs, sl], lo),
            k_h=_stack2(k2[rs, sl] * e_neg[rs, sl], lo),
            b_h=_stack2(b_all[j][rs] * e_neg[rs, sl], lo),
            k_e=_stack2(k2[rs, sl] * e_end[rs, sl], lo),
            b_e=_stack2(b_all[j][rs] * e_end[rs, sl], lo),
            v2=_stack2(v[rs, sl], lo))
    yield
    for key in insts:
        d = st2[key]
        prod = _dot_nt(jnp.concatenate([d["kk_t"], d["r_t"]], axis=0),
                       jnp.concatenate([d["b_h"], d["k_h"]], axis=0))
        d["lmat"] = jnp.where(strict, prod[0:2 * c, 0:2 * c], 0.0)
        d["a_kk"] = jnp.where(strict, prod[0:2 * c, 2 * c:4 * c], 0.0)
        d["a_rb"] = jnp.where(incl, prod[2 * c:4 * c, 0:2 * c], 0.0)
        d["a_rk"] = jnp.where(incl, prod[2 * c:4 * c, 2 * c:4 * c], 0.0)
    yield
    for key in insts:
        d = st2[key]
        d["s"] = eye - d["lmat"]
        d["p"] = _dot(d["lmat"], d["lmat"])
    yield
    for _ in range(4):
        for key in insts:
            d = st2[key]
            both = _dot(jnp.concatenate([d["p"], d["s"]], axis=0), d["p"])
            d["p"] = both[0:2 * c]
            d["s"] = d["s"] + both[2 * c:4 * c]
        yield
    for key in insts:
        d = st2[key]
        d["s"] = d["s"] + _dot(d["s"], d["p"])
        d["akv"] = _dot(d["a_kk"], d["v2"])
    yield
    for key in insts:
        d = st2[key]
        tw = _dot(d["s"], jnp.concatenate([d["kk_t"], d["akv"]], axis=1))
        d["w"] = -tw[:, 0:LANES]
        d["u0"] = -tw[:, LANES:2 * LANES]
    yield
    for key in insts:
        d = st2[key]
        zero = jnp.zeros((2 * c, LANES), F32)
        rhs = jnp.concatenate([jnp.concatenate([d["v2"], zero], axis=1),
                               jnp.concatenate([d["u0"], d["w"]], axis=1)], axis=0)
        yq = _dot(jnp.concatenate([d["a_rk"], d["a_rb"]], axis=1), rhs)
        d["y0"] = yq[:, 0:LANES]
        d["q"] = d["r_t"] + yq[:, LANES:2 * LANES]
        d["m"] = _dot_tn(d["b_e"], d["w"])
        d["n_t"] = _dot_tn(jnp.concatenate([d["v2"], d["u0"]], axis=0),
                           jnp.concatenate([d["k_e"], d["b_e"]], axis=0))
    yield

    hts = [st_ref[j] for j in range(N_PAIRS)]
    ys = [[] for _ in range(N_PAIRS)]
    for ci in range(n_ch):
        for j in range(N_PAIRS):
            sl = slice(j * LANES, (j + 1) * LANES)
            d = st2[ci, j]
            ht = hts[j]
            y2 = _dot_nt(d["q"], ht) + d["y0"]
            hts[j] = ht * jnp.exp(c_end[ci][:, sl]) + _dot_nt(ht, d["m"]) + d["n_t"]
            ys[j].append(y2[0:c] + y2[c:2 * c])
    for j in range(N_PAIRS):
        st_ref[j] = hts[j]
    yield
    y_pairs = [jnp.concatenate(ys[j], axis=0) for j in range(N_PAIRS)]
    means = [_dot(y, ones_bd) * (1.0 / HEAD_DIM) for y in y_pairs]
    ycs = [y - m for y, m in zip(y_pairs, means)]
    vars_ = [_dot(yc * yc, ones_bd) * (1.0 / HEAD_DIM) for yc in ycs]
    for j in range(N_PAIRS):
        sl = slice(j * LANES, (j + 1) * LANES)
        yn = ycs[j] * lax.rsqrt(vars_[j] + RWKV_LN_EPS) * lnw_ref[:, sl] + lnb_ref[:, sl]
        bonus = _bsum(rk_term[:, sl], ones_bd) * v[:, sl]
        o_ref[0, :, sl] = ((yn + bonus) * g[:, sl]).astype(BF16)
    yield


def _rwkv_spec(p3, v_first, prm, consts):
    b, s, _ = p3.shape
    tri, ones_bd = consts
    first_layer = v_first is None
    args = [p3, p3]
    in_specs = [_seq_spec(3 * GROUP, COL_RW // (3 * GROUP)),
                _seq_spec(RWKV_LORA_COLS, COL_RW_LORA // RWKV_LORA_COLS)]
    if not first_layer:
        args.append(v_first)
        in_specs.append(_seq_spec(GROUP))
    consts_in = [prm[n] for n in ("mu_rkv", "mu_lora", "w0", "w_up", "a0", "a_up", "g_up", "k_k",
                                  "k_a", "r_k", "ln_w", "ln_b")]
    if not first_layer:
        consts_in += [prm["v0"], prm["v_down"], prm["v_up"]]
    consts_in += [tri, ones_bd]
    args += consts_in
    in_specs += [_full(a) for a in consts_in]
    out_specs = [_seq_spec(GROUP)]
    out_shape = [_out_bf16(b, s)]
    if first_layer:
        out_specs.append(_seq_spec(GROUP))
        out_shape.append(jax.ShapeDtypeStruct((b, s, GROUP), F32))
    return dict(
        name="rwkv", body=functools.partial(_rwkv_body, first_layer), args=args,
        in_specs=in_specs, out_specs=out_specs, out_shape=out_shape,
        scratch=[pltpu.VMEM((8, 3 * GROUP), F32),
                 pltpu.VMEM((8, RWKV_LORA_COLS), F32),
                 pltpu.VMEM((N_PAIRS, LANES, LANES), F32)])


def _pad_cols(w, n):
    return jnp.pad(w, ((0, 0), (0, n - w.shape[1])))


def _pad_rows(w, n):
    return jnp.pad(w, ((0, n - w.shape[0]), (0, 0)))


def _layout_w_in(w):
    g = GROUP
    ret, hg = w[:, 0:4 * g], w[:, 4 * g:8 * g]
    ml = w[:, 8 * g:12 * g + 2 * N_HEADS]
    rw = w[:, 12 * g + 2 * N_HEADS:]
    ml_main, ml_gates = ml[:, 0:4 * g], ml[:, 4 * g:]
    out = jnp.concatenate([ret, hg, ml_main, rw, ml_gates], axis=1)
    return _pad_cols(out, N_IN_PAD).astype(BF16)


def _rwkv_params(l, rwkv_mu, rwkv_w0, rwkv_w_up, rwkv_a0, rwkv_a_up, rwkv_g_up, rwkv_k_k, rwkv_k_a,
                 rwkv_r_k, rwkv_ln_w, rwkv_ln_b, rwkv_v0, rwkv_v_down, rwkv_v_up):
    g = GROUP
    mu = rwkv_mu[l]
    prm = {
        "mu_rkv": mu[0:3 * g].reshape(1, -1), "mu_lora": mu[3 * g:].reshape(1, -1),
        "w0": rwkv_w0[l].reshape(1, g), "a0": rwkv_a0[l].reshape(1, g),
        "w_up": _pad_rows(rwkv_w_up[l], LANES).astype(BF16),
        "a_up": jnp.concatenate([jnp.zeros_like(rwkv_w_up[l]), rwkv_a_up[l]], axis=0).astype(BF16),
        "g_up": rwkv_g_up[l].astype(BF16),
        "k_k": rwkv_k_k[l].reshape(1, g), "k_a": rwkv_k_a[l].reshape(1, g),
        "r_k": rwkv_r_k[l].reshape(1, g), "ln_w": rwkv_ln_w[l].reshape(1, g),
        "ln_b": rwkv_ln_b[l].reshape(1, g),
    }
    if l > 0:
        prm["v0"] = rwkv_v0[l - 1].reshape(1, g)
        prm["v_down"] = _pad_cols(rwkv_v_down[l - 1], LANES).astype(BF16)
        prm["v_up"] = _pad_rows(rwkv_v_up[l - 1], LANES).astype(BF16)
    return prm


def kernel(x, w_in, w_out, norm_pre_mix, norm_post_mix, norm_pre_ffn, norm_post_ffn, w_ffn_gate, w_ffn_up, w_ffn_down, hgrn_lb_logits, hgrn_norm_w, mlstm_conv_w, mlstm_conv_b, mlstm_i_bias, mlstm_f_bias, mlstm_norm_w, rwkv_mu, rwkv_w0, rwkv_w_up, rwkv_a0, rwkv_a_up, rwkv_g_up, rwkv_k_k, rwkv_k_a, rwkv_r_k, rwkv_ln_w, rwkv_ln_b, rwkv_v0, rwkv_v_down, rwkv_v_up):
    b, s, d = x.shape
    depth = w_in.shape[0]
    t = b * s
    g = GROUP

    lb_cum = jnp.cumsum(jax.nn.softmax(hgrn_lb_logits.astype(F32), axis=0), axis=0)
    lower_bounds = lb_cum - lb_cum[0]

    bd_f32 = jnp.asarray(_block_ones(), F32)
    ones_bd = jnp.asarray(_block_ones(), BF16)
    cos, sin = _rope_tables(s)
    ret_tables = (cos, sin) + _retention_tables() + (bd_f32, ones_bd)
    hgrn_consts = (jnp.asarray(_tri_incl(MIX_TS), BF16), bd_f32, ones_bd)
    mlstm_consts = _mlstm_tables() + (bd_f32, ones_bd)
    rwkv_consts = (jnp.asarray(_tri_incl(MIX_TS, RWKV_C), BF16), ones_bd)

    h = x.reshape(t, d)
    u = _prenorm(h, norm_pre_mix[0])
    v_first = None
    for l in range(depth):
        p3 = _in_proj(u, _layout_w_in(w_in[l])).reshape(b, s, N_IN_PAD)

        prm = _rwkv_params(l, rwkv_mu, rwkv_w0, rwkv_w_up, rwkv_a0, rwkv_a_up, rwkv_g_up, rwkv_k_k,
                           rwkv_k_a, rwkv_r_k, rwkv_ln_w, rwkv_ln_b, rwkv_v0, rwkv_v_down, rwkv_v_up)
        specs = {
            "ret": _ret_spec(p3, ret_tables),
            "hgrn": _hgrn_spec(p3, lower_bounds[l], hgrn_norm_w[l], hgrn_consts),
            "mlstm": _mlstm_spec(p3, mlstm_conv_w[l], mlstm_conv_b[l], mlstm_i_bias[l],
                                 mlstm_f_bias[l], mlstm_norm_w[l], mlstm_consts),
            "rwkv": _rwkv_spec(p3, v_first, prm, rwkv_consts),
        }
        mixed = {}
        for group in MIXER_GROUPS:
            mixed.update(_run_mixers([specs[name] for name, _ in group], b, s,
                                     [delay for _, delay in group]))
        if v_first is None:
            v_first = mixed["rwkv"][1]

        outs = [mixed[name][0].reshape(t, g) for name in ("ret", "hgrn", "mlstm", "rwkv")]
        h, u = _out_proj(outs, w_out[l].astype(BF16), h, norm_post_mix[l], norm_pre_ffn[l])
        w_pre_next = norm_pre_mix[l + 1] if l + 1 < depth else norm_pre_mix[l]
        h, u = _ffn(u, h, w_ffn_gate[l], w_ffn_up[l],
                    w_ffn_down[l].astype(BF16), norm_post_ffn[l], w_pre_next)
    return h.reshape(b, s, d)
```

```python
import functools

import numpy as np
import jax
import jax.numpy as jnp
from jax import lax
from jax.experimental import pallas as pl
from jax.experimental.pallas import tpu as pltpu

F32 = jnp.float32
BF16 = jnp.bfloat16

HEAD_DIM = 64
N_HEADS = 8
GROUP = N_HEADS * HEAD_DIM
LANES = 128
N_PAIRS = GROUP // LANES
NORM_EPS = 1e-6
ROPE_BASE = 10000.0
RWKV_DECAY_SCALE = 0.6065306597126334
RWKV_LN_EPS = 64e-5
RWKV_LORA_COLS = 256
CONV_WIDTH = 4
NEG_BIG = -1e30

MIX_TS = 128
RWKV_C = 64
RWKV_WAVE = 8
MIXER_GROUPS = ((("rwkv", 0), ("hgrn", 0)), (("mlstm", 0), ("ret", 0)))

COL_RET = 0
COL_HGRN = 4 * GROUP
COL_ML = 8 * GROUP
COL_RW = 12 * GROUP
COL_RW_LORA = 15 * GROUP
COL_ML_GATES = 15 * GROUP + RWKV_LORA_COLS
N_IN_PAD = 16 * GROUP

VMEM_LIMIT = 56 * 1024 * 1024


def _cparams(n_axes):
    return pltpu.CompilerParams(dimension_semantics=("arbitrary",) * n_axes,
                                vmem_limit_bytes=VMEM_LIMIT)


def _dot(a, b):
    return jnp.dot(a.astype(BF16), b.astype(BF16), preferred_element_type=F32)


def _dot_nt(a, b):
    return lax.dot_general(a.astype(BF16), b.astype(BF16), (((1,), (1,)), ((), ())),
                           preferred_element_type=F32)


def _dot_tn(a, b):
    return lax.dot_general(a.astype(BF16), b.astype(BF16), (((0,), (0,)), ((), ())),
                           preferred_element_type=F32)


def _split3(x):
    hi = x.astype(BF16)
    r1 = x - hi.astype(F32)
    mid = r1.astype(BF16)
    lo = (r1 - mid.astype(F32)).astype(BF16)
    return hi, mid, lo


def _dot_x3(x, sel):
    hi, mid, lo = _split3(x)
    d = lambda t: jnp.dot(t, sel, preferred_element_type=F32)
    return d(hi) + d(mid) + d(lo)


def _dot_3x(sel, x):
    hi, mid, lo = _split3(x)
    d = lambda t: jnp.dot(sel, t, preferred_element_type=F32)
    return d(hi) + d(mid) + d(lo)


def _bsum(x, ones_bd):
    hi = x.astype(BF16)
    lo = (x - hi.astype(F32)).astype(BF16)
    return (jnp.dot(hi, ones_bd, preferred_element_type=F32)
            + jnp.dot(lo, ones_bd, preferred_element_type=F32))


def _sigmoid(x):
    return 1.0 / (1.0 + jnp.exp(-x))


def _silu(x):
    return x * _sigmoid(x)


def _log_sigmoid(x):
    return jnp.minimum(x, 0.0) - jnp.log1p(jnp.exp(-jnp.abs(x)))


def _rms_rows(x, w):
    return x * lax.rsqrt(jnp.mean(x * x, axis=-1, keepdims=True) + NORM_EPS) * w


def _shift_rows(x, prev8, k):
    rolled = pltpu.roll(x, k, axis=0)
    row8 = lax.broadcasted_iota(jnp.int32, (8, x.shape[1]), 0)
    top = jnp.where(row8 < k, pltpu.roll(prev8, k, axis=0), rolled[0:8])
    return jnp.concatenate([top, rolled[8:]], axis=0)


def _stack2(x, lo):
    return jnp.concatenate([jnp.where(lo, x, 0.0), jnp.where(lo, 0.0, x)], axis=0)


def _rms_kernel(x_ref, w_ref, u_ref):
    u_ref[...] = _rms_rows(x_ref[...], w_ref[...]).astype(BF16)


def _prenorm(x2d, w, tm=512):
    t, d = x2d.shape
    return pl.pallas_call(
        _rms_kernel,
        grid=(t // tm,),
        in_specs=[pl.BlockSpec((tm, d), lambda i: (i, 0)),
                  pl.BlockSpec((1, d), lambda i: (0, 0))],
        out_specs=pl.BlockSpec((tm, d), lambda i: (i, 0)),
        out_shape=jax.ShapeDtypeStruct((t, d), BF16),
        compiler_params=_cparams(1),
        name="prenorm",
    )(x2d, w.reshape(1, d))


def _mm_kernel(u_ref, w_ref, o_ref):
    o_ref[...] = jnp.dot(u_ref[...], w_ref[...], preferred_element_type=F32)


def _in_proj(u, w, layer, tm=2048, tn=1024):
    t, d = u.shape
    n = w.shape[2]
    tm = min(tm, t)
    return pl.pallas_call(
        _mm_kernel,
        grid=(t // tm, n // tn),
        in_specs=[pl.BlockSpec((tm, d), lambda i, j: (i, 0)),
                  pl.BlockSpec((None, d, tn), lambda i, j: (layer, 0, j))],
        out_specs=pl.BlockSpec((tm, tn), lambda i, j: (i, j)),
        out_shape=jax.ShapeDtypeStruct((t, n), F32),
        compiler_params=_cparams(2),
        name="in_proj",
    )(u, w)


def _outproj_kernel(o0_ref, o1_ref, o2_ref, o3_ref, w_ref, h_ref, wpost_ref, wpre_ref,
                    hout_ref, u_ref):
    acc = jnp.dot(o0_ref[...], w_ref[0 * GROUP:1 * GROUP, :], preferred_element_type=F32)
    acc += jnp.dot(o1_ref[...], w_ref[1 * GROUP:2 * GROUP, :], preferred_element_type=F32)
    acc += jnp.dot(o2_ref[...], w_ref[2 * GROUP:3 * GROUP, :], preferred_element_type=F32)
    acc += jnp.dot(o3_ref[...], w_ref[3 * GROUP:4 * GROUP, :], preferred_element_type=F32)
    h = h_ref[...] + _rms_rows(acc, wpost_ref[...])
    hout_ref[...] = h
    u_ref[...] = _rms_rows(h, wpre_ref[...]).astype(BF16)


def _out_proj(outs, w, layer, h, w_post, w_pre, tm=512):
    t, d = h.shape
    tm = min(tm, t)
    row = lambda i: (i, 0)
    fixed = lambda i: (0, 0)
    return pl.pallas_call(
        _outproj_kernel,
        grid=(t // tm,),
        in_specs=[pl.BlockSpec((tm, GROUP), row)] * 4 + [
            pl.BlockSpec((None, d, d), lambda i: (layer, 0, 0)),
            pl.BlockSpec((tm, d), row),
            pl.BlockSpec((1, d), fixed),
            pl.BlockSpec((1, d), fixed)],
        out_specs=[pl.BlockSpec((tm, d), row), pl.BlockSpec((tm, d), row)],
        out_shape=[jax.ShapeDtypeStruct((t, d), F32), jax.ShapeDtypeStruct((t, d), BF16)],
        compiler_params=_cparams(1),
        name="out_proj",
    )(*outs, w, h, w_post.reshape(1, d), w_pre.reshape(1, d))


def _ffn_up_kernel(u_ref, wg_ref, wu_ref, a_ref):
    u = u_ref[...]
    tf = a_ref.shape[1]
    for c0 in range(0, tf, tf // 2):
        cols = slice(c0, c0 + tf // 2)
        g = jnp.dot(u, wg_ref[:, cols].astype(BF16), preferred_element_type=F32)
        up = jnp.dot(u, wu_ref[:, cols].astype(BF16), preferred_element_type=F32)
        a_ref[:, cols] = (_silu(g) * up).astype(BF16)


def _ffn_down_kernel(a_ref, wd_ref, h_ref, wpost_ref, wpre_ref, hout_ref, u_next_ref):
    kk = pl.program_id(1)

    @pl.when(kk == 0)
    def _():
        hout_ref[...] = jnp.zeros_like(hout_ref)

    hout_ref[...] += jnp.dot(a_ref[...], wd_ref[...], preferred_element_type=F32)

    @pl.when(kk == pl.num_programs(1) - 1)
    def _():
        h = h_ref[...] + _rms_rows(hout_ref[...], wpost_ref[...])
        hout_ref[...] = h
        u_next_ref[...] = _rms_rows(h, wpre_ref[...]).astype(BF16)


def _ffn(u, h, layer, wg, wu, wd, w_post, w_pre_next, tm_up=2048, tf=512, tm_down=1024, tk=512):
    t, d = h.shape
    f = wg.shape[2]
    tm_up, tm_down = min(tm_up, t), min(tm_down, t)
    act = pl.pallas_call(
        _ffn_up_kernel,
        grid=(t // tm_up, f // tf),
        in_specs=[pl.BlockSpec((tm_up, d), lambda i, j: (i, 0)),
                  pl.BlockSpec((None, d, tf), lambda i, j: (layer, 0, j)),
                  pl.BlockSpec((None, d, tf), lambda i, j: (layer, 0, j))],
        out_specs=pl.BlockSpec((tm_up, tf), lambda i, j: (i, j)),
        out_shape=jax.ShapeDtypeStruct((t, f), BF16),
        compiler_params=_cparams(2),
        name="ffn_up",
    )(u, wg, wu)
    row = lambda i, k: (i, 0)
    fixed = lambda i, k: (0, 0)
    return pl.pallas_call(
        _ffn_down_kernel,
        grid=(t // tm_down, f // tk),
        in_specs=[pl.BlockSpec((tm_down, tk), lambda i, k: (i, k)),
                  pl.BlockSpec((None, tk, d), lambda i, k: (layer, k, 0)),
                  pl.BlockSpec((tm_down, d), row),
                  pl.BlockSpec((1, d), fixed),
                  pl.BlockSpec((1, d), fixed)],
        out_specs=[pl.BlockSpec((tm_down, d), row), pl.BlockSpec((tm_down, d), row)],
        out_shape=[jax.ShapeDtypeStruct((t, d), F32), jax.ShapeDtypeStruct((t, d), BF16)],
        compiler_params=_cparams(2),
        name="ffn_down",
    )(act, wd, h, w_post.reshape(1, d), w_pre_next.reshape(1, d))


def _block_ones():
    i = np.arange(LANES)
    return (i[:, None] // HEAD_DIM == i[None, :] // HEAD_DIM).astype(np.float32)


def _tri_incl(c, block=None):
    i = np.arange(c)
    m = i[None, :] <= i[:, None]
    if block is not None:
        m = m & (i[None, :] // block == i[:, None] // block)
    return m.astype(np.float32)


def _rope_tables(seq):
    half = HEAD_DIM // 2
    lane = np.arange(LANES)
    inv_freq = ROPE_BASE ** (-jnp.arange(half, dtype=F32) / half)
    ang = jnp.arange(seq, dtype=F32)[:, None] * inv_freq[None, :]
    cos, sin = jnp.cos(ang), jnp.sin(ang)
    fidx = (lane % HEAD_DIM) % half
    sign = np.where((lane % HEAD_DIM) < half, -1.0, 1.0).astype(np.float32)
    return cos[:, fidx], sin[:, fidx] * sign


def _retention_tables():
    c = MIX_TS
    log_gamma = jnp.log1p(-jnp.exp2(-5.0 - jnp.arange(N_HEADS, dtype=F32)))
    idx = jnp.arange(c, dtype=F32)
    rel = idx[:, None] - idx[None, :]
    intra = jnp.where(rel >= 0, jnp.exp(log_gamma[:, None, None] * jnp.maximum(rel, 0.0)), 0.0)
    q_dec = jnp.exp(log_gamma[:, None] * (idx + 1.0))
    k_dec = jnp.exp(log_gamma[:, None] * (c - 1.0 - idx))
    chunk_dec = jnp.exp(log_gamma * c)
    per_lane = lambda t: jnp.repeat(t.T, HEAD_DIM, axis=1)
    bd = jnp.asarray(_block_ones())
    lane_head = np.arange(LANES) // HEAD_DIM
    sdec = jnp.stack([bd * chunk_dec[2 * j + lane_head][None, :] for j in range(N_PAIRS)])
    intra2 = intra.reshape(N_PAIRS, 2 * c, c)
    return intra2, per_lane(q_dec), per_lane(k_dec), sdec


def _mlstm_tables():
    c = MIX_TS
    esel = np.zeros((LANES, N_HEADS * c), np.float32)
    epf = np.zeros((LANES, GROUP), np.float32)
    epi = np.zeros((LANES, GROUP), np.float32)
    for h in range(N_HEADS):
        esel[N_HEADS + h, h * c:(h + 1) * c] = 1.0
        epf[N_HEADS + h, h * HEAD_DIM:(h + 1) * HEAD_DIM] = 1.0
        epi[h, h * HEAD_DIM:(h + 1) * HEAD_DIM] = 1.0
    return (jnp.asarray(_tri_incl(c), BF16), jnp.asarray(esel, BF16), jnp.asarray(epf, BF16),
            jnp.asarray(epi, BF16))


def _fixed(ndim):
    return lambda i, t: (0,) * ndim


def _full(a):
    return pl.BlockSpec(a.shape, _fixed(a.ndim))


def _seq_spec(width, col_block=0):
    return pl.BlockSpec((1, MIX_TS, width), lambda i, t: (i, t, col_block))


def _mixer_kernel(bodies, n_in, n_out, n_scr, delays, *refs):
    ins, outs, scrs = [], [], []
    pos = 0
    for group, counts in ((ins, n_in), (outs, n_out), (scrs, n_scr)):
        for n in counts:
            group.append(refs[pos:pos + n])
            pos += n

    @pl.when(pl.program_id(1) == 0)
    def _():
        for scr in scrs:
            for ref in scr:
                ref[...] = jnp.zeros_like(ref)

    active = [body(*i, *o, *s) for body, i, o, s in zip(bodies, ins, outs, scrs)]
    waits = list(delays)
    while active:
        for gen in list(active):
            k = active.index(gen)
            if waits[k] > 0:
                waits[k] -= 1
            elif next(gen, "done") == "done":
                active.pop(k)
                waits.pop(k)


def _run_mixers(specs, b, s, delays=None):
    kern = functools.partial(_mixer_kernel, tuple(sp["body"] for sp in specs),
                             tuple(len(sp["args"]) for sp in specs),
                             tuple(len(sp["out_shape"]) for sp in specs),
                             tuple(len(sp["scratch"]) for sp in specs),
                             tuple(delays) if delays else (0,) * len(specs))
    flat = lambda key: [x for sp in specs for x in sp[key]]
    res = pl.pallas_call(
        kern,
        grid=(b, s // MIX_TS),
        in_specs=flat("in_specs"),
        out_specs=flat("out_specs"),
        out_shape=flat("out_shape"),
        scratch_shapes=flat("scratch"),
        compiler_params=_cparams(2),
        name="mix_" + "_".join(sp["name"] for sp in specs),
    )(*flat("args"))
    outs, pos = {}, 0
    for sp in specs:
        n = len(sp["out_shape"])
        outs[sp["name"]] = res[pos:pos + n]
        pos += n
    return outs


def _out_bf16(b, s):
    return jax.ShapeDtypeStruct((b, s, GROUP), BF16)


def _ret_body(p_ref, cos_ref, sin_ref, intra_ref, qdec_ref, kdec_ref, sdec_ref, bd_ref,
              ones_ref, o_ref, st_ref):
    c = MIX_TS
    cos = cos_ref[...]
    sin = sin_ref[...]
    lane = lax.broadcasted_iota(jnp.int32, (c, LANES), 1)
    first_half = (lane & (HEAD_DIM - 1)) < HEAD_DIM // 2
    lo = lane < HEAD_DIM
    ones_bd = ones_ref[...]
    bd = bd_ref[...]

    def rot(t):
        swapped = jnp.where(first_half, pltpu.roll(t, LANES - HEAD_DIM // 2, axis=1),
                            pltpu.roll(t, HEAD_DIM // 2, axis=1))
        return t * cos + swapped * sin

    pairs = range(N_PAIRS)
    sls = [slice(j * LANES, (j + 1) * LANES) for j in pairs]
    qs = [rot(p_ref[0, :, j * LANES:(j + 1) * LANES]) for j in pairs]
    ks = [rot(p_ref[0, :, GROUP + j * LANES:GROUP + (j + 1) * LANES]) * HEAD_DIM ** -0.5
          for j in pairs]
    vbs = [p_ref[0, :, 2 * GROUP + j * LANES:2 * GROUP + (j + 1) * LANES].astype(BF16)
           for j in pairs]
    sts = [st_ref[j] for j in pairs]
    yield
    scores = [_dot_nt(_stack2(qs[j], lo), ks[j]) * intra_ref[j] for j in pairs]
    pvs = [_dot(scores[j], vbs[j]) for j in pairs]
    inters = [_dot(qs[j] * qdec_ref[:, sls[j]], sts[j]) for j in pairs]
    outs = [jnp.where(lo, pvs[j][0:c], pvs[j][c:2 * c]) + inters[j] for j in pairs]
    yield
    for j in pairs:
        st_ref[j] = sdec_ref[j] * sts[j] + bd * _dot_tn(ks[j] * kdec_ref[:, sls[j]], vbs[j])
    mss = [_bsum(outs[j] * outs[j], ones_bd) * (1.0 / HEAD_DIM) for j in pairs]
    for j in pairs:
        g = p_ref[0, :, 3 * GROUP + j * LANES:3 * GROUP + (j + 1) * LANES]
        o_ref[0, :, sls[j]] = (outs[j] * lax.rsqrt(mss[j] + NORM_EPS) * _silu(g)).astype(BF16)
    yield


def _ret_spec(p3, tables):
    b, s, _ = p3.shape
    c = MIX_TS
    cos, sin, intra, qdec, kdec, sdec, bd, ones_bd = tables
    tab_spec = pl.BlockSpec((c, LANES), lambda i, t: (t, 0))
    return dict(
        name="ret", body=_ret_body,
        args=[p3, cos, sin, intra, qdec, kdec, sdec, bd, ones_bd],
        in_specs=[_seq_spec(4 * GROUP, COL_RET // (4 * GROUP)), tab_spec, tab_spec,
                  _full(intra), _full(qdec), _full(kdec), _full(sdec), _full(bd), _full(ones_bd)],
        out_specs=[_seq_spec(GROUP)], out_shape=[_out_bf16(b, s)],
        scratch=[pltpu.VMEM((N_PAIRS, LANES, LANES), F32)])


def _level_ref(cum, h):
    ts, n = cum.shape
    if h >= 8:
        return jnp.concatenate(
            [jnp.broadcast_to(cum[b0 + h - 1:b0 + h, :], (2 * h, n)) for b0 in range(0, ts, 2 * h)],
            axis=0)
    cum3 = cum.reshape(ts // 8, 8, n)
    brow = lambda i: jnp.broadcast_to(cum3[:, i:i + 1, :], cum3.shape).reshape(ts, n)
    sub = lax.broadcasted_iota(jnp.int32, (ts, n), 0) & 7
    if h == 4:
        return brow(3)
    if h == 2:
        return jnp.where(sub < 4, brow(1), brow(5))
    return jnp.where(sub < 2, brow(0), jnp.where(sub < 4, brow(2), jnp.where(sub < 6, brow(4), brow(6))))


def _hgrn_body(p_ref, loglb_ref, log1mlb_ref, onemlb_ref, normw_ref, tri_ref, bd_ref, ones_ref,
               o_ref, st_ref):
    ts = MIX_TS
    pf = p_ref[0, :, GROUP:2 * GROUP]
    q = _silu(p_ref[0, :, 0:GROUP]) * HEAD_DIM ** -0.5
    e_f = jnp.exp(-jnp.abs(pf))
    k = onemlb_ref[...] * (jnp.where(pf >= 0.0, e_f, 1.0) / (1.0 + e_f))
    a = loglb_ref[...]
    bb = log1mlb_ref[...] + (jnp.minimum(pf, 0.0) - jnp.log1p(e_f))
    log_f = jnp.maximum(a, bb) + jnp.log1p(jnp.exp(-jnp.abs(a - bb)))
    cum = _dot_3x(tri_ref[...], log_f)
    yield

    ones_bd = ones_ref[...]
    bd = bd_ref[...]
    lo = lax.broadcasted_iota(jnp.int32, (ts, LANES), 1) < HEAD_DIM
    row2 = lax.broadcasted_iota(jnp.int32, (2 * ts, ts), 0) & (ts - 1)
    col2 = lax.broadcasted_iota(jnp.int32, (2 * ts, ts), 1)
    diff2 = jnp.where(col2 < row2, row2 ^ col2, 0)

    lo4 = (lax.broadcasted_iota(jnp.int32, (ts, GROUP), 1) & HEAD_DIM) == 0
    q_a = jnp.where(lo4, q, 0.0)
    q_b = (q - q_a).astype(BF16)
    q_a = q_a.astype(BF16)
    k_bf = k.astype(BF16)
    scores = [None] * N_PAIRS
    h = ts // 2
    while h >= 1:
        e = jnp.exp(-jnp.abs(cum - _level_ref(cum, h))).astype(BF16)
        qa_l, qb_l, k_l = q_a * e, q_b * e, k_bf * e
        level = (diff2 >> (h.bit_length() - 1)) == 1
        for j in range(N_PAIRS):
            sl = slice(j * LANES, (j + 1) * LANES)
            sc = _dot_nt(jnp.concatenate([qa_l[:, sl], qb_l[:, sl]], axis=0), k_l[:, sl])
            scores[j] = jnp.where(level, sc, 0.0 if scores[j] is None else scores[j])
        h //= 2
        yield

    q_in = q * jnp.exp(cum)
    c_end = cum[ts - 1:ts, :]
    k_out = k * jnp.exp(c_end - cum)
    diag = q * k
    for j in range(N_PAIRS):
        sl = slice(j * LANES, (j + 1) * LANES)
        vj = p_ref[0, :, 2 * GROUP + j * LANES:2 * GROUP + (j + 1) * LANES]
        gj = p_ref[0, :, 3 * GROUP + j * LANES:3 * GROUP + (j + 1) * LANES]
        pv = _dot(scores[j], vj)
        st_t = st_ref[j]
        out = (jnp.where(lo, pv[0:ts], pv[ts:2 * ts]) + _bsum(diag[:, sl], ones_bd) * vj
               + _dot_nt(q_in[:, sl], st_t))
        st_ref[j] = st_t * jnp.exp(c_end[:, sl]) + bd * _dot_tn(vj, k_out[:, sl])
        ms = _bsum(out * out, ones_bd) * (1.0 / HEAD_DIM)
        y = out * lax.rsqrt(ms + NORM_EPS) * normw_ref[:, sl]
        o_ref[0, :, sl] = (y * _sigmoid(gj)).astype(BF16)
        yield


def _hgrn_spec(p3, lb, norm_w, consts):
    b, s, _ = p3.shape
    tri, bd, ones_bd = consts
    row = lambda v: v.reshape(1, GROUP)
    args = [p3, row(jnp.log(lb)), row(jnp.log1p(-lb)), row(1.0 - lb), row(norm_w), tri, bd, ones_bd]
    return dict(
        name="hgrn", body=_hgrn_body, args=args,
        in_specs=[_seq_spec(4 * GROUP, COL_HGRN // (4 * GROUP))] + [_full(a) for a in args[1:]],
        out_specs=[_seq_spec(GROUP)], out_shape=[_out_bf16(b, s)],
        scratch=[pltpu.VMEM((N_PAIRS, LANES, LANES), F32)])


def _mlstm_body(pm_ref, pg_ref, convw_ref, convb_ref, gbias_ref, normw_ref, tri_ref, esel_ref,
                epf_ref, epi_ref, bd_ref, ones_ref, o_ref,
                prev_ref, cm_ref, n_ref, m_ref, mgl_ref):
    c = MIX_TS
    ones_bd = ones_ref[...]
    bd = bd_ref[...]

    x = pm_ref[0, :, 0:2 * GROUP]
    prev = prev_ref[...]
    acc = convb_ref[...] + x * convw_ref[CONV_WIDTH - 1:CONV_WIDTH, :]
    for sh in range(1, CONV_WIDTH):
        acc = acc + _shift_rows(x, prev, sh) * convw_ref[CONV_WIDTH - 1 - sh:CONV_WIDTH - sh, :]
    prev_ref[...] = x[c - 8:c]
    qk = _silu(acc)
    q_all = qk[:, 0:GROUP]
    k_all = qk[:, GROUP:2 * GROUP] * HEAD_DIM ** -0.5
    yield

    lane = lax.broadcasted_iota(jnp.int32, (c, LANES), 1)
    gb = pg_ref[0] + gbias_ref[...]
    gl = jnp.where(lane < N_HEADS, gb, jnp.where(lane < 2 * N_HEADS, _log_sigmoid(gb), 0.0))
    cum = _dot_3x(tri_ref[...], gl)
    mgl = mgl_ref[...]
    xm = cum + mgl
    colb = _dot_x3(xm, esel_ref[...])
    xm_t = xm.T
    gl_t = gl.T

    row = lax.broadcasted_iota(jnp.int32, (c, c), 0)
    col = lax.broadcasted_iota(jnp.int32, (c, c), 1)
    causal = col <= row
    dmat, w_inter, e_neg_m = [], [], []
    for h in range(N_HEADS):
        log_inter = colb[:, h * c:(h + 1) * c]
        logd = jnp.where(
            causal, log_inter + (gl_t[h:h + 1, :] - xm_t[N_HEADS + h:N_HEADS + h + 1, :]), NEG_BIG)
        m_t = jnp.maximum(jnp.max(logd, axis=1, keepdims=True), log_inter)
        dmat.append(jnp.exp(logd - m_t))
        w_inter.append(jnp.exp(log_inter - m_t))
        e_neg_m.append(jnp.exp(-m_t))
    yield

    cumcol = _dot_x3(cum, epf_ref[...])
    icol = _dot_x3(gl, epi_ref[...])
    c_end = cumcol[c - 1:c, :]
    m_old = m_ref[...]
    log_w = c_end - cumcol + icol
    m_new = jnp.maximum(c_end + m_old, jnp.max(log_w, axis=0, keepdims=True))
    decay = jnp.exp(c_end + m_old - m_new)
    kw_all = k_all * jnp.exp(log_w - m_new)
    n_old = n_ref[...]
    n_ref[...] = decay * n_old + jnp.sum(kw_all, axis=0, keepdims=True)
    m_ref[...] = m_new
    c_end_gl = cum[c - 1:c, :]
    log_w_gl = c_end_gl - cum + pltpu.roll(gl, N_HEADS, axis=1)
    m_new_gl = jnp.maximum(c_end_gl + mgl, jnp.max(log_w_gl, axis=0, keepdims=True))
    lane1 = lax.broadcasted_iota(jnp.int32, (1, LANES), 1)
    mgl_ref[...] = jnp.where((lane1 >= N_HEADS) & (lane1 < 2 * N_HEADS), m_new_gl, 0.0)
    yield

    lo = lane < HEAD_DIM
    pairs = range(N_PAIRS)
    sls = [slice(j * LANES, (j + 1) * LANES) for j in pairs]
    ones_full = jnp.ones((c, LANES), BF16)
    v1s = [jnp.concatenate(
        [pm_ref[0, :, 2 * GROUP + j * LANES:2 * GROUP + (j + 1) * LANES].astype(BF16), ones_full],
        axis=1) for j in pairs]
    cms = [cm_ref[j] for j in pairs]
    scores = [_dot_nt(_stack2(q_all[:, sls[j]], lo), k_all[:, sls[j]])
              * jnp.concatenate([dmat[2 * j], dmat[2 * j + 1]], axis=0) for j in pairs]
    pvr = [_dot(scores[j], v1s[j]) for j in pairs]
    qcs = [_dot(q_all[:, sls[j]], cms[j]) for j in pairs]
    qns = [_bsum(q_all[:, sls[j]] * n_old[:, sls[j]], ones_bd) for j in pairs]
    yield
    for j in pairs:
        cm_ref[j] = cms[j] * decay[:, sls[j]] + bd * _dot_tn(kw_all[:, sls[j]], v1s[j][:, 0:LANES])
    hhs = []
    for j in pairs:
        wi = jnp.where(lo, w_inter[2 * j], w_inter[2 * j + 1])
        num = jnp.where(lo, pvr[j][0:c, 0:LANES], pvr[j][c:2 * c, 0:LANES]) + wi * qcs[j]
        den = (jnp.where(lo, pvr[j][0:c, LANES:2 * LANES], pvr[j][c:2 * c, LANES:2 * LANES])
               + wi * qns[j])
        floor = jnp.where(lo, e_neg_m[2 * j], e_neg_m[2 * j + 1])
        hhs.append(num / jnp.maximum(jnp.abs(den), floor))
    mss = [_bsum(hhs[j] * hhs[j], ones_bd) * (1.0 / HEAD_DIM) for j in pairs]
    for j in pairs:
        oj = pm_ref[0, :, 3 * GROUP + j * LANES:3 * GROUP + (j + 1) * LANES]
        y = hhs[j] * lax.rsqrt(mss[j] + NORM_EPS) * normw_ref[:, sls[j]]
        o_ref[0, :, sls[j]] = (_sigmoid(oj) * y).astype(BF16)
    yield


def _mlstm_spec(p3, conv_w, conv_b, i_bias, f_bias, norm_w, consts):
    b, s, _ = p3.shape
    tri, esel, epf, epi, bd, ones_bd = consts
    gbias = jnp.zeros((1, LANES), F32).at[0, 0:N_HEADS].set(i_bias).at[0, N_HEADS:2 * N_HEADS].set(f_bias)
    args = [p3, p3, conv_w, conv_b.reshape(1, -1), gbias, norm_w.reshape(1, GROUP), tri, esel, epf,
            epi, bd, ones_bd]
    return dict(
        name="mlstm", body=_mlstm_body, args=args,
        in_specs=[_seq_spec(4 * GROUP, COL_ML // (4 * GROUP)), _seq_spec(LANES, COL_ML_GATES // LANES)]
        + [_full(a) for a in args[2:]],
        out_specs=[_seq_spec(GROUP)], out_shape=[_out_bf16(b, s)],
        scratch=[pltpu.VMEM((8, 2 * GROUP), F32),
                 pltpu.VMEM((N_PAIRS, LANES, LANES), F32),
                 pltpu.VMEM((1, GROUP), F32),
                 pltpu.VMEM((1, GROUP), F32),
                 pltpu.VMEM((1, LANES), F32)])


def _rwkv_body(first_layer, *refs):
    ts, c = MIX_TS, RWKV_C
    n_ch = ts // c
    if first_layer:
        (prkv_ref, plora_ref, mu_rkv_ref, mu_lora_ref, w0_ref, wup_ref, a0_ref, aup_ref, gup_ref,
         kk_ref, ka_ref, rk_ref, lnw_ref, lnb_ref, tri_ref, ones_ref,
         o_ref, vfirst_out_ref, prev_rkv_ref, prev_lora_ref, st_ref) = refs
    else:
        (prkv_ref, plora_ref, vfirst_ref, mu_rkv_ref, mu_lora_ref, w0_ref, wup_ref, a0_ref, aup_ref,
         gup_ref, kk_ref, ka_ref, rk_ref, lnw_ref, lnb_ref, v0_ref, vdown_ref, vup_ref, tri_ref,
         ones_ref, o_ref, prev_rkv_ref, prev_lora_ref, st_ref) = refs

    ones_bd = ones_ref[...]

    def token_mix(x_ref, prev_ref, mu_ref):
        x = x_ref[0]
        shifted = _shift_rows(x, prev_ref[...], 1)
        prev_ref[...] = x[ts - 8:ts]
        return x + (shifted - x) * mu_ref[...]

    rkv = token_mix(prkv_ref, prev_rkv_ref, mu_rkv_ref)
    lora = token_mix(plora_ref, prev_lora_ref, mu_lora_ref)
    r = rkv[:, 0:GROUP]
    k = rkv[:, GROUP:2 * GROUP]
    v = rkv[:, 2 * GROUP:3 * GROUP]
    wa = lora[:, 0:LANES]
    log_w = -RWKV_DECAY_SCALE * _sigmoid(w0_ref[...] + _dot(jnp.tanh(wa), wup_ref[...]))
    a = _sigmoid(a0_ref[...] + _dot(wa, aup_ref[...]))
    g = _dot(_sigmoid(lora[:, LANES:2 * LANES]), gup_ref[...])
    if first_layer:
        vfirst_out_ref[0] = v
    else:
        mix = _sigmoid(v0_ref[...] + _dot(_dot(v, vdown_ref[...]), vup_ref[...]))
        v = v + (vfirst_ref[0] - v) * mix

    kk_raw = k * kk_ref[...]
    k2 = k * (1.0 + (a - 1.0) * ka_ref[...])
    cum = _dot_3x(tri_ref[...], log_w)
    c_end = [cum[(ci + 1) * c - 1:(ci + 1) * c, :] for ci in range(n_ch)]
    c_end_rows = jnp.concatenate([jnp.broadcast_to(e, (c, GROUP)) for e in c_end], axis=0)
    r_g = r * jnp.exp(cum)
    e_prev = jnp.exp(cum - log_w)
    e_neg = jnp.exp(-cum)
    e_end = jnp.exp(c_end_rows - cum)
    rk_term = r * k2 * rk_ref[...]
    yield

    lane = lax.broadcasted_iota(jnp.int32, (c, LANES), 1)
    lo = lane < HEAD_DIM
    row = lax.broadcasted_iota(jnp.int32, (2 * c, 2 * c), 0)
    col = lax.broadcasted_iota(jnp.int32, (2 * c, 2 * c), 1)
    strict = col < row
    incl = col <= row
    eye = (col == row).astype(F32)

    kk_all, b_all = [], []
    for j in range(N_PAIRS):
        sl = slice(j * LANES, (j + 1) * LANES)
        kkr = kk_raw[:, sl]
        kk = kkr / jnp.maximum(jnp.sqrt(_bsum(kkr * kkr, ones_bd)), 1e-12)
        kk_all.append(kk)
        b_all.append(a[:, sl] * kk)

    insts = [(ci, j) for ci in range(n_ch) for j in range(N_PAIRS)]
    st2 = {}
    for ci, j in insts:
        rs = slice(ci * c, (ci + 1) * c)
        sl = slice(j * LANES, (j + 1) * LANES)
        st2[ci, j] = dict(
            r_t=_stack2(r_g[rs, sl].astype(BF16), lo),
            kk_t=_stack2((kk_all[j][rs] * e_prev[rs, sl]).astype(BF16), lo),
            k_h=_stack2((k2[rs, sl] * e_neg[rs, sl]).astype(BF16), lo),
            b_h=_stack2((b_all[j][rs] * e_neg[rs, sl]).astype(BF16), lo),
            k_e=_stack2((k2[rs, sl] * e_end[rs, sl]).astype(BF16), lo),
            b_e=_stack2((b_all[j][rs] * e_end[rs, sl]).astype(BF16), lo),
            v2=_stack2(v[rs, sl].astype(BF16), lo))
    yield

    def chunk_stages(wave):
        for key in wave:
            d = st2[key]
            prod = _dot_nt(jnp.concatenate([d["kk_t"], d["r_t"]], axis=0),
                           jnp.concatenate([d["b_h"], d["k_h"]], axis=0)).astype(BF16)
            d["lmat"] = jnp.where(strict, prod[0:2 * c, 0:2 * c], 0.0)
            d["a_kk"] = jnp.where(strict, prod[0:2 * c, 2 * c:4 * c], 0.0)
            d["a_rb"] = jnp.where(incl, prod[2 * c:4 * c, 0:2 * c], 0.0)
            d["a_rk"] = jnp.where(incl, prod[2 * c:4 * c, 2 * c:4 * c], 0.0)
        yield
        for key in wave:
            d = st2[key]
            d["s"] = eye - d["lmat"]
            d["p"] = _dot(d["lmat"], d["lmat"]).astype(BF16)
        yield
        for _ in range(4):
            for key in wave:
                d = st2[key]
                both = _dot(jnp.concatenate([d["p"], d["s"].astype(BF16)], axis=0), d["p"])
                d["p"] = both[0:2 * c].astype(BF16)
                d["s"] = d["s"] + both[2 * c:4 * c]
            yield
        for key in wave:
            d = st2[key]
            d["s"] = d["s"] + _dot(d["s"], d["p"])
            d["akv"] = _dot(d["a_kk"], d["v2"]).astype(BF16)
        yield
        for key in wave:
            d = st2[key]
            tw = -_dot(d["s"], jnp.concatenate([d["kk_t"], d["akv"]], axis=1))
            d["w"] = tw[:, 0:LANES].astype(BF16)
            d["u0"] = tw[:, LANES:2 * LANES].astype(BF16)
        yield
        for key in wave:
            d = st2[key]
            zero = jnp.zeros((2 * c, LANES), BF16)
            rhs = jnp.concatenate([jnp.concatenate([d["v2"], zero], axis=1),
                                   jnp.concatenate([d["u0"], d["w"]], axis=1)], axis=0)
            yq = _dot(jnp.concatenate([d["a_rk"], d["a_rb"]], axis=1), rhs)
            d["y0"] = yq[:, 0:LANES]
            d["q"] = d["r_t"] + yq[:, LANES:2 * LANES]
            d["m"] = _dot_tn(d["b_e"], d["w"])
            d["n_t"] = _dot_tn(jnp.concatenate([d["v2"], d["u0"]], axis=0),
                               jnp.concatenate([d["k_e"], d["b_e"]], axis=0))
        yield

    for w0 in range(0, len(insts), RWKV_WAVE):
        yield from chunk_stages(insts[w0:w0 + RWKV_WAVE])

    hts = [st_ref[j] for j in range(N_PAIRS)]
    ys = [[] for _ in range(N_PAIRS)]
    for ci in range(n_ch):
        for j in range(N_PAIRS):
            sl = slice(j * LANES, (j + 1) * LANES)
            d = st2[ci, j]
            ht = hts[j]
            y2 = _dot_nt(d["q"], ht) + d["y0"]
            hts[j] = ht * jnp.exp(c_end[ci][:, sl]) + _dot_nt(ht, d["m"]) + d["n_t"]
            ys[j].append(y2[0:c] + y2[c:2 * c])
    for j in range(N_PAIRS):
        st_ref[j] = hts[j]
    yield
    y_pairs = [jnp.concatenate(ys[j], axis=0) for j in range(N_PAIRS)]
    means = [_dot(y, ones_bd) * (1.0 / HEAD_DIM) for y in y_pairs]
    ycs = [y - m for y, m in zip(y_pairs, means)]
    vars_ = [_dot(yc * yc, ones_bd) * (1.0 / HEAD_DIM) for yc in ycs]
    for j in range(N_PAIRS):
        sl = slice(j * LANES, (j + 1) * LANES)
        yn = ycs[j] * lax.rsqrt(vars_[j] + RWKV_LN_EPS) * lnw_ref[:, sl] + lnb_ref[:, sl]
        bonus = _bsum(rk_term[:, sl], ones_bd) * v[:, sl]
        o_ref[0, :, sl] = ((yn + bonus) * g[:, sl]).astype(BF16)
    yield


def _rwkv_spec(p3, v_first, prm, consts):
    b, s, _ = p3.shape
    tri, ones_bd = consts
    first_layer = v_first is None
    args = [p3, p3]
    in_specs = [_seq_spec(3 * GROUP, COL_RW // (3 * GROUP)),
                _seq_spec(RWKV_LORA_COLS, COL_RW_LORA // RWKV_LORA_COLS)]
    if not first_layer:
        args.append(v_first)
        in_specs.append(_seq_spec(GROUP))
    consts_in = [prm[n] for n in ("mu_rkv", "mu_lora", "w0", "w_up", "a0", "a_up", "g_up", "k_k",
                                  "k_a", "r_k", "ln_w", "ln_b")]
    if not first_layer:
        consts_in += [prm["v0"], prm["v_down"], prm["v_up"]]
    consts_in += [tri, ones_bd]
    args += consts_in
    in_specs += [_full(a) for a in consts_in]
    out_specs = [_seq_spec(GROUP)]
    out_shape = [_out_bf16(b, s)]
    if first_layer:
        out_specs.append(_seq_spec(GROUP))
        out_shape.append(jax.ShapeDtypeStruct((b, s, GROUP), F32))
    return dict(
        name="rwkv", body=functools.partial(_rwkv_body, first_layer), args=args,
        in_specs=in_specs, out_specs=out_specs, out_shape=out_shape,
        scratch=[pltpu.VMEM((8, 3 * GROUP), F32),
                 pltpu.VMEM((8, RWKV_LORA_COLS), F32),
                 pltpu.VMEM((N_PAIRS, LANES, LANES), F32)])


def _pad_cols(w, n):
    return jnp.pad(w, ((0, 0), (0, n - w.shape[1])))


def _pad_rows(w, n):
    return jnp.pad(w, ((0, n - w.shape[0]), (0, 0)))


def _layout_w_in(w):
    g = GROUP
    ml_gates = w[..., 12 * g:12 * g + 2 * N_HEADS]
    pad = jnp.zeros(w.shape[:-1] + (N_IN_PAD - w.shape[-1],), w.dtype)
    out = jnp.concatenate([w[..., 0:12 * g], w[..., 12 * g + 2 * N_HEADS:], ml_gates, pad], axis=-1)
    return out.astype(BF16)


def _rwkv_params(l, rwkv_mu, rwkv_w0, rwkv_w_up, rwkv_a0, rwkv_a_up, rwkv_g_up, rwkv_k_k, rwkv_k_a,
                 rwkv_r_k, rwkv_ln_w, rwkv_ln_b, rwkv_v0, rwkv_v_down, rwkv_v_up):
    g = GROUP
    mu = rwkv_mu[l]
    prm = {
        "mu_rkv": mu[0:3 * g].reshape(1, -1), "mu_lora": mu[3 * g:].reshape(1, -1),
        "w0": rwkv_w0[l].reshape(1, g), "a0": rwkv_a0[l].reshape(1, g),
        "w_up": _pad_rows(rwkv_w_up[l], LANES).astype(BF16),
        "a_up": jnp.concatenate([jnp.zeros_like(rwkv_w_up[l]), rwkv_a_up[l]], axis=0).astype(BF16),
        "g_up": rwkv_g_up[l].astype(BF16),
        "k_k": rwkv_k_k[l].reshape(1, g), "k_a": rwkv_k_a[l].reshape(1, g),
        "r_k": rwkv_r_k[l].reshape(1, g), "ln_w": rwkv_ln_w[l].reshape(1, g),
        "ln_b": rwkv_ln_b[l].reshape(1, g),
    }
    if l > 0:
        prm["v0"] = rwkv_v0[l - 1].reshape(1, g)
        prm["v_down"] = _pad_cols(rwkv_v_down[l - 1], LANES).astype(BF16)
        prm["v_up"] = _pad_rows(rwkv_v_up[l - 1], LANES).astype(BF16)
    return prm


def kernel(x, w_in, w_out, norm_pre_mix, norm_post_mix, norm_pre_ffn, norm_post_ffn, w_ffn_gate, w_ffn_up, w_ffn_down, hgrn_lb_logits, hgrn_norm_w, mlstm_conv_w, mlstm_conv_b, mlstm_i_bias, mlstm_f_bias, mlstm_norm_w, rwkv_mu, rwkv_w0, rwkv_w_up, rwkv_a0, rwkv_a_up, rwkv_g_up, rwkv_k_k, rwkv_k_a, rwkv_r_k, rwkv_ln_w, rwkv_ln_b, rwkv_v0, rwkv_v_down, rwkv_v_up):
    b, s, d = x.shape
    depth = w_in.shape[0]
    t = b * s
    g = GROUP

    lb_cum = jnp.cumsum(jax.nn.softmax(hgrn_lb_logits.astype(F32), axis=0), axis=0)
    lower_bounds = lb_cum - lb_cum[0]

    bd_f32 = jnp.asarray(_block_ones(), F32)
    ones_bd = jnp.asarray(_block_ones(), BF16)
    cos, sin = _rope_tables(s)
    ret_tables = (cos, sin) + _retention_tables() + (bd_f32, ones_bd)
    hgrn_consts = (jnp.asarray(_tri_incl(MIX_TS), BF16), bd_f32, ones_bd)
    mlstm_consts = _mlstm_tables() + (bd_f32, ones_bd)
    rwkv_consts = (jnp.asarray(_tri_incl(MIX_TS, RWKV_C), BF16), ones_bd)

    w_in_bf = _layout_w_in(w_in)
    w_out_bf = w_out.astype(BF16)
    w_down_bf = w_ffn_down.astype(BF16)

    h = x.reshape(t, d)
    u = _prenorm(h, norm_pre_mix[0])
    v_first = None
    for l in range(depth):
        p3 = _in_proj(u, w_in_bf, l).reshape(b, s, N_IN_PAD)

        prm = _rwkv_params(l, rwkv_mu, rwkv_w0, rwkv_w_up, rwkv_a0, rwkv_a_up, rwkv_g_up, rwkv_k_k,
                           rwkv_k_a, rwkv_r_k, rwkv_ln_w, rwkv_ln_b, rwkv_v0, rwkv_v_down, rwkv_v_up)
        specs = {
            "ret": _ret_spec(p3, ret_tables),
            "hgrn": _hgrn_spec(p3, lower_bounds[l], hgrn_norm_w[l], hgrn_consts),
            "mlstm": _mlstm_spec(p3, mlstm_conv_w[l], mlstm_conv_b[l], mlstm_i_bias[l],
                                 mlstm_f_bias[l], mlstm_norm_w[l], mlstm_consts),
            "rwkv": _rwkv_spec(p3, v_first, prm, rwkv_consts),
        }
        mixed = {}
        for group in MIXER_GROUPS:
            mixed.update(_run_mixers([specs[name] for name, _ in group], b, s,
                                     [delay for _, delay in group]))
        if v_first is None:
            v_first = mixed["rwkv"][1]

        outs = [mixed[name][0].reshape(t, g) for name in ("ret", "hgrn", "mlstm", "rwkv")]
        h, u = _out_proj(outs, w_out_bf, l, h, norm_post_mix[l], norm_pre_ffn[l])
        w_pre_next = norm_pre_mix[l + 1] if l + 1 < depth else norm_pre_mix[l]
        h, u = _ffn(u, h, l, w_ffn_gate, w_ffn_up, w_down_bf, norm_post_ffn[l], w_pre_next)
    return h.reshape(b, s, d)
```

```python
import functools

import numpy as np
import jax
import jax.numpy as jnp
from jax import lax
from jax.experimental import pallas as pl
from jax.experimental.pallas import tpu as pltpu

F32 = jnp.float32
BF16 = jnp.bfloat16

HEAD_DIM = 64
N_HEADS = 8
GROUP = N_HEADS * HEAD_DIM
LANES = 128
N_PAIRS = GROUP // LANES
NORM_EPS = 1e-6
ROPE_BASE = 10000.0
RWKV_DECAY_SCALE = 0.6065306597126334
RWKV_LN_EPS = 64e-5
RWKV_LORA_COLS = 256
CONV_WIDTH = 4
NEG_BIG = -1e30

MIX_TS = 128
RWKV_C = 64
RWKV_WAVE = 8
MIXER_GROUPS = ((("rwkv", 0), ("hgrn", 0)), (("mlstm", 0), ("ret", 0)))

COL_RET = 0
COL_HGRN = 4 * GROUP
COL_ML = 8 * GROUP
COL_RW = 12 * GROUP
COL_RW_LORA = 15 * GROUP
COL_ML_GATES = 15 * GROUP + RWKV_LORA_COLS
N_IN_PAD = 16 * GROUP

VMEM_LIMIT = 56 * 1024 * 1024


def _cparams(n_axes):
    return pltpu.CompilerParams(dimension_semantics=("arbitrary",) * n_axes,
                                vmem_limit_bytes=VMEM_LIMIT)


def _dot(a, b):
    return jnp.dot(a.astype(BF16), b.astype(BF16), preferred_element_type=F32)


def _dot_nt(a, b):
    return lax.dot_general(a.astype(BF16), b.astype(BF16), (((1,), (1,)), ((), ())),
                           preferred_element_type=F32)


def _dot_tn(a, b):
    return lax.dot_general(a.astype(BF16), b.astype(BF16), (((0,), (0,)), ((), ())),
                           preferred_element_type=F32)


def _split3(x):
    hi = x.astype(BF16)
    r1 = x - hi.astype(F32)
    mid = r1.astype(BF16)
    lo = (r1 - mid.astype(F32)).astype(BF16)
    return hi, mid, lo


def _dot_x3(x, sel):
    hi, mid, lo = _split3(x)
    d = lambda t: jnp.dot(t, sel, preferred_element_type=F32)
    return d(hi) + d(mid) + d(lo)


def _dot_3x(sel, x):
    hi, mid, lo = _split3(x)
    d = lambda t: jnp.dot(sel, t, preferred_element_type=F32)
    return d(hi) + d(mid) + d(lo)


def _bsum(x, ones_bd):
    return jnp.dot(x.astype(BF16), ones_bd, preferred_element_type=F32)


def _sigmoid(x):
    return 1.0 / (1.0 + jnp.exp(-x))


def _silu(x):
    return x * _sigmoid(x)


def _log_sigmoid(x):
    return jnp.minimum(x, 0.0) - jnp.log1p(jnp.exp(-jnp.abs(x)))


def _rms_rows(x, w):
    return x * lax.rsqrt(jnp.mean(x * x, axis=-1, keepdims=True) + NORM_EPS) * w


def _shift_rows(x, prev8, k):
    rolled = pltpu.roll(x, k, axis=0)
    row8 = lax.broadcasted_iota(jnp.int32, (8, x.shape[1]), 0)
    top = jnp.where(row8 < k, pltpu.roll(prev8, k, axis=0), rolled[0:8])
    return jnp.concatenate([top, rolled[8:]], axis=0)


def _stack2(x, lo):
    return jnp.concatenate([jnp.where(lo, x, 0.0), jnp.where(lo, 0.0, x)], axis=0)


def _rms_kernel(x_ref, w_ref, u_ref):
    u_ref[...] = _rms_rows(x_ref[...], w_ref[...]).astype(BF16)


def _prenorm(x2d, w, tm=512):
    t, d = x2d.shape
    return pl.pallas_call(
        _rms_kernel,
        grid=(t // tm,),
        in_specs=[pl.BlockSpec((tm, d), lambda i: (i, 0)),
                  pl.BlockSpec((1, d), lambda i: (0, 0))],
        out_specs=pl.BlockSpec((tm, d), lambda i: (i, 0)),
        out_shape=jax.ShapeDtypeStruct((t, d), BF16),
        compiler_params=_cparams(1),
        name="prenorm",
    )(x2d, w.reshape(1, d))


def _mm_kernel(u_ref, w_ref, o_ref):
    o_ref[...] = jnp.dot(u_ref[...], w_ref[...], preferred_element_type=F32)


def _in_proj(u, w, layer, tm=2048, tn=1024):
    t, d = u.shape
    n = w.shape[2]
    tm = min(tm, t)
    return pl.pallas_call(
        _mm_kernel,
        grid=(t // tm, n // tn),
        in_specs=[pl.BlockSpec((tm, d), lambda i, j: (i, 0)),
                  pl.BlockSpec((None, d, tn), lambda i, j: (layer, 0, j))],
        out_specs=pl.BlockSpec((tm, tn), lambda i, j: (i, j)),
        out_shape=jax.ShapeDtypeStruct((t, n), F32),
        compiler_params=_cparams(2),
        name="in_proj",
    )(u, w)


def _outproj_kernel(o0_ref, o1_ref, o2_ref, o3_ref, w_ref, h_ref, wpost_ref, wpre_ref,
                    hout_ref, u_ref):
    acc = jnp.dot(o0_ref[...], w_ref[0 * GROUP:1 * GROUP, :], preferred_element_type=F32)
    acc += jnp.dot(o1_ref[...], w_ref[1 * GROUP:2 * GROUP, :], preferred_element_type=F32)
    acc += jnp.dot(o2_ref[...], w_ref[2 * GROUP:3 * GROUP, :], preferred_element_type=F32)
    acc += jnp.dot(o3_ref[...], w_ref[3 * GROUP:4 * GROUP, :], preferred_element_type=F32)
    h = h_ref[...] + _rms_rows(acc, wpost_ref[...])
    hout_ref[...] = h
    u_ref[...] = _rms_rows(h, wpre_ref[...]).astype(BF16)


def _out_proj(outs, w, layer, h, w_post, w_pre, tm=512):
    t, d = h.shape
    tm = min(tm, t)
    row = lambda i: (i, 0)
    fixed = lambda i: (0, 0)
    return pl.pallas_call(
        _outproj_kernel,
        grid=(t // tm,),
        in_specs=[pl.BlockSpec((tm, GROUP), row)] * 4 + [
            pl.BlockSpec((None, d, d), lambda i: (layer, 0, 0)),
            pl.BlockSpec((tm, d), row),
            pl.BlockSpec((1, d), fixed),
            pl.BlockSpec((1, d), fixed)],
        out_specs=[pl.BlockSpec((tm, d), row), pl.BlockSpec((tm, d), row)],
        out_shape=[jax.ShapeDtypeStruct((t, d), F32), jax.ShapeDtypeStruct((t, d), BF16)],
        compiler_params=_cparams(1),
        name="out_proj",
    )(*outs, w, h, w_post.reshape(1, d), w_pre.reshape(1, d))


def _ffn_up_kernel(u_ref, wg_ref, wu_ref, a_ref):
    u = u_ref[...]
    tf = a_ref.shape[1]
    for c0 in range(0, tf, tf // 2):
        cols = slice(c0, c0 + tf // 2)
        g = jnp.dot(u, wg_ref[:, cols].astype(BF16), preferred_element_type=F32)
        up = jnp.dot(u, wu_ref[:, cols].astype(BF16), preferred_element_type=F32)
        a_ref[:, cols] = (_silu(g) * up).astype(BF16)


def _ffn_down_kernel(a_ref, wd_ref, h_ref, wpost_ref, wpre_ref, hout_ref, u_next_ref):
    kk = pl.program_id(1)

    @pl.when(kk == 0)
    def _():
        hout_ref[...] = jnp.dot(a_ref[...], wd_ref[...], preferred_element_type=F32)

    @pl.when(kk > 0)
    def _():
        hout_ref[...] += jnp.dot(a_ref[...], wd_ref[...], preferred_element_type=F32)

    @pl.when(kk == pl.num_programs(1) - 1)
    def _():
        h = h_ref[...] + _rms_rows(hout_ref[...], wpost_ref[...])
        hout_ref[...] = h
        u_next_ref[...] = _rms_rows(h, wpre_ref[...]).astype(BF16)


def _ffn(u, h, layer, wg, wu, wd, w_post, w_pre_next, tm_up=2048, tf=512, tm_down=1024, tk=512):
    t, d = h.shape
    f = wg.shape[2]
    tm_up, tm_down = min(tm_up, t), min(tm_down, t)
    act = pl.pallas_call(
        _ffn_up_kernel,
        grid=(t // tm_up, f // tf),
        in_specs=[pl.BlockSpec((tm_up, d), lambda i, j: (i, 0)),
                  pl.BlockSpec((None, d, tf), lambda i, j: (layer, 0, j)),
                  pl.BlockSpec((None, d, tf), lambda i, j: (layer, 0, j))],
        out_specs=pl.BlockSpec((tm_up, tf), lambda i, j: (i, j)),
        out_shape=jax.ShapeDtypeStruct((t, f), BF16),
        compiler_params=_cparams(2),
        name="ffn_up",
    )(u, wg, wu)
    row = lambda i, k: (i, 0)
    fixed = lambda i, k: (0, 0)
    return pl.pallas_call(
        _ffn_down_kernel,
        grid=(t // tm_down, f // tk),
        in_specs=[pl.BlockSpec((tm_down, tk), lambda i, k: (i, k)),
                  pl.BlockSpec((None, tk, d), lambda i, k: (layer, k, 0)),
                  pl.BlockSpec((tm_down, d), row),
                  pl.BlockSpec((1, d), fixed),
                  pl.BlockSpec((1, d), fixed)],
        out_specs=[pl.BlockSpec((tm_down, d), row), pl.BlockSpec((tm_down, d), row)],
        out_shape=[jax.ShapeDtypeStruct((t, d), F32), jax.ShapeDtypeStruct((t, d), BF16)],
        compiler_params=_cparams(2),
        name="ffn_down",
    )(act, wd, h, w_post.reshape(1, d), w_pre_next.reshape(1, d))


def _block_ones():
    i = np.arange(LANES)
    return (i[:, None] // HEAD_DIM == i[None, :] // HEAD_DIM).astype(np.float32)


def _tri_incl(c, block=None):
    i = np.arange(c)
    m = i[None, :] <= i[:, None]
    if block is not None:
        m = m & (i[None, :] // block == i[:, None] // block)
    return m.astype(np.float32)


def _rope_tables(seq):
    half = HEAD_DIM // 2
    lane = np.arange(LANES)
    inv_freq = ROPE_BASE ** (-jnp.arange(half, dtype=F32) / half)
    ang = jnp.arange(seq, dtype=F32)[:, None] * inv_freq[None, :]
    cos, sin = jnp.cos(ang), jnp.sin(ang)
    fidx = (lane % HEAD_DIM) % half
    sign = np.where((lane % HEAD_DIM) < half, -1.0, 1.0).astype(np.float32)
    return cos[:, fidx], sin[:, fidx] * sign


def _retention_tables():
    c = MIX_TS
    log_gamma = jnp.log1p(-jnp.exp2(-5.0 - jnp.arange(N_HEADS, dtype=F32)))
    idx = jnp.arange(c, dtype=F32)
    rel = idx[:, None] - idx[None, :]
    intra = jnp.where(rel >= 0, jnp.exp(log_gamma[:, None, None] * jnp.maximum(rel, 0.0)), 0.0)
    q_dec = jnp.exp(log_gamma[:, None] * (idx + 1.0))
    k_dec = jnp.exp(log_gamma[:, None] * (c - 1.0 - idx))
    chunk_dec = jnp.exp(log_gamma * c)
    per_lane = lambda t: jnp.repeat(t.T, HEAD_DIM, axis=1)
    bd = jnp.asarray(_block_ones())
    lane_head = np.arange(LANES) // HEAD_DIM
    sdec = jnp.stack([bd * chunk_dec[2 * j + lane_head][None, :] for j in range(N_PAIRS)])
    intra2 = intra.reshape(N_PAIRS, 2 * c, c)
    return intra2, per_lane(q_dec), per_lane(k_dec), sdec


def _mlstm_tables():
    c = MIX_TS
    esel = np.zeros((LANES, N_HEADS * c), np.float32)
    epf = np.zeros((LANES, GROUP), np.float32)
    epi = np.zeros((LANES, GROUP), np.float32)
    for h in range(N_HEADS):
        esel[N_HEADS + h, h * c:(h + 1) * c] = 1.0
        epf[N_HEADS + h, h * HEAD_DIM:(h + 1) * HEAD_DIM] = 1.0
        epi[h, h * HEAD_DIM:(h + 1) * HEAD_DIM] = 1.0
    return (jnp.asarray(_tri_incl(c), BF16), jnp.asarray(esel, BF16), jnp.asarray(epf, BF16),
            jnp.asarray(epi, BF16))


def _fixed(ndim):
    return lambda i, t: (0,) * ndim


def _full(a):
    return pl.BlockSpec(a.shape, _fixed(a.ndim))


def _seq_spec(width, col_block=0):
    return pl.BlockSpec((1, MIX_TS, width), lambda i, t: (i, t, col_block))


def _mixer_kernel(bodies, n_in, n_out, n_scr, delays, *refs):
    ins, outs, scrs = [], [], []
    pos = 0
    for group, counts in ((ins, n_in), (outs, n_out), (scrs, n_scr)):
        for n in counts:
            group.append(refs[pos:pos + n])
            pos += n

    @pl.when(pl.program_id(1) == 0)
    def _():
        for scr in scrs:
            for ref in scr:
                ref[...] = jnp.zeros_like(ref)

    active = [body(*i, *o, *s) for body, i, o, s in zip(bodies, ins, outs, scrs)]
    waits = list(delays)
    while active:
        for gen in list(active):
            k = active.index(gen)
            if waits[k] > 0:
                waits[k] -= 1
            elif next(gen, "done") == "done":
                active.pop(k)
                waits.pop(k)


def _run_mixers(specs, b, s, delays=None):
    kern = functools.partial(_mixer_kernel, tuple(sp["body"] for sp in specs),
                             tuple(len(sp["args"]) for sp in specs),
                             tuple(len(sp["out_shape"]) for sp in specs),
                             tuple(len(sp["scratch"]) for sp in specs),
                             tuple(delays) if delays else (0,) * len(specs))
    flat = lambda key: [x for sp in specs for x in sp[key]]
    res = pl.pallas_call(
        kern,
        grid=(b, s // MIX_TS),
        in_specs=flat("in_specs"),
        out_specs=flat("out_specs"),
        out_shape=flat("out_shape"),
        scratch_shapes=flat("scratch"),
        compiler_params=_cparams(2),
        name="mix_" + "_".join(sp["name"] for sp in specs),
    )(*flat("args"))
    outs, pos = {}, 0
    for sp in specs:
        n = len(sp["out_shape"])
        outs[sp["name"]] = res[pos:pos + n]
        pos += n
    return outs


def _out_bf16(b, s):
    return jax.ShapeDtypeStruct((b, s, GROUP), BF16)


def _ret_body(p_ref, cos_ref, sin_ref, intra_ref, qdec_ref, kdec_ref, sdec_ref, bd_ref,
              ones_ref, o_ref, st_ref):
    c = MIX_TS
    cos = cos_ref[...]
    sin = sin_ref[...]
    lane = lax.broadcasted_iota(jnp.int32, (c, LANES), 1)
    first_half = (lane & (HEAD_DIM - 1)) < HEAD_DIM // 2
    lo = lane < HEAD_DIM
    ones_bd = ones_ref[...]
    bd = bd_ref[...]

    def rot(t):
        swapped = jnp.where(first_half, pltpu.roll(t, LANES - HEAD_DIM // 2, axis=1),
                            pltpu.roll(t, HEAD_DIM // 2, axis=1))
        return t * cos + swapped * sin

    pairs = range(N_PAIRS)
    sls = [slice(j * LANES, (j + 1) * LANES) for j in pairs]
    qs = [rot(p_ref[0, :, j * LANES:(j + 1) * LANES]) for j in pairs]
    ks = [rot(p_ref[0, :, GROUP + j * LANES:GROUP + (j + 1) * LANES]) * HEAD_DIM ** -0.5
          for j in pairs]
    vbs = [p_ref[0, :, 2 * GROUP + j * LANES:2 * GROUP + (j + 1) * LANES].astype(BF16)
           for j in pairs]
    sts = [st_ref[j] for j in pairs]
    yield
    scores = [_dot_nt(_stack2(qs[j], lo), ks[j]) * intra_ref[j] for j in pairs]
    pvs = [_dot(scores[j], vbs[j]) for j in pairs]
    inters = [_dot(qs[j] * qdec_ref[:, sls[j]], sts[j]) for j in pairs]
    outs = [jnp.where(lo, pvs[j][0:c], pvs[j][c:2 * c]) + inters[j] for j in pairs]
    yield
    for j in pairs:
        st_ref[j] = sdec_ref[j] * sts[j] + bd * _dot_tn(ks[j] * kdec_ref[:, sls[j]], vbs[j])
    mss = [_bsum(outs[j] * outs[j], ones_bd) * (1.0 / HEAD_DIM) for j in pairs]
    for j in pairs:
        g = p_ref[0, :, 3 * GROUP + j * LANES:3 * GROUP + (j + 1) * LANES]
        o_ref[0, :, sls[j]] = (outs[j] * lax.rsqrt(mss[j] + NORM_EPS) * _silu(g)).astype(BF16)
    yield


def _ret_spec(p3, tables):
    b, s, _ = p3.shape
    c = MIX_TS
    cos, sin, intra, qdec, kdec, sdec, bd, ones_bd = tables
    tab_spec = pl.BlockSpec((c, LANES), lambda i, t: (t, 0))
    return dict(
        name="ret", body=_ret_body,
        args=[p3, cos, sin, intra, qdec, kdec, sdec, bd, ones_bd],
        in_specs=[_seq_spec(4 * GROUP, COL_RET // (4 * GROUP)), tab_spec, tab_spec,
                  _full(intra), _full(qdec), _full(kdec), _full(sdec), _full(bd), _full(ones_bd)],
        out_specs=[_seq_spec(GROUP)], out_shape=[_out_bf16(b, s)],
        scratch=[pltpu.VMEM((N_PAIRS, LANES, LANES), F32)])


def _level_ref(cum, h):
    ts, n = cum.shape
    if h >= 8:
        return jnp.concatenate(
            [jnp.broadcast_to(cum[b0 + h - 1:b0 + h, :], (2 * h, n)) for b0 in range(0, ts, 2 * h)],
            axis=0)
    cum3 = cum.reshape(ts // 8, 8, n)
    brow = lambda i: jnp.broadcast_to(cum3[:, i:i + 1, :], cum3.shape).reshape(ts, n)
    sub = lax.broadcasted_iota(jnp.int32, (ts, n), 0) & 7
    if h == 4:
        return brow(3)
    if h == 2:
        return jnp.where(sub < 4, brow(1), brow(5))
    return jnp.where(sub < 2, brow(0), jnp.where(sub < 4, brow(2), jnp.where(sub < 6, brow(4), brow(6))))


def _hgrn_body(p_ref, loglb_ref, log1mlb_ref, onemlb_ref, normw_ref, tri_ref, bd_ref, ones_ref,
               o_ref, st_ref):
    ts = MIX_TS
    pf = p_ref[0, :, GROUP:2 * GROUP]
    q = _silu(p_ref[0, :, 0:GROUP]) * HEAD_DIM ** -0.5
    e_f = jnp.exp(-jnp.abs(pf))
    k = onemlb_ref[...] * (jnp.where(pf >= 0.0, e_f, 1.0) / (1.0 + e_f))
    a = loglb_ref[...]
    bb = log1mlb_ref[...] + (jnp.minimum(pf, 0.0) - jnp.log1p(e_f))
    log_f = jnp.maximum(a, bb) + jnp.log1p(jnp.exp(-jnp.abs(a - bb)))
    cum = _dot_3x(tri_ref[...], log_f)
    yield

    ones_bd = ones_ref[...]
    bd = bd_ref[...]
    lo = lax.broadcasted_iota(jnp.int32, (ts, LANES), 1) < HEAD_DIM
    row2 = lax.broadcasted_iota(jnp.int32, (2 * ts, ts), 0) & (ts - 1)
    col2 = lax.broadcasted_iota(jnp.int32, (2 * ts, ts), 1)
    diff2 = jnp.where(col2 < row2, row2 ^ col2, 0)

    lo4 = (lax.broadcasted_iota(jnp.int32, (ts, GROUP), 1) & HEAD_DIM) == 0
    q_a = jnp.where(lo4, q, 0.0)
    q_b = (q - q_a).astype(BF16)
    q_a = q_a.astype(BF16)
    k_bf = k.astype(BF16)
    scores = [None] * N_PAIRS
    h = ts // 2
    while h >= 1:
        e = jnp.exp(-jnp.abs(cum - _level_ref(cum, h))).astype(BF16)
        qa_l, qb_l, k_l = q_a * e, q_b * e, k_bf * e
        level = (diff2 >> (h.bit_length() - 1)) == 1
        for j in range(N_PAIRS):
            sl = slice(j * LANES, (j + 1) * LANES)
            sc = _dot_nt(jnp.concatenate([qa_l[:, sl], qb_l[:, sl]], axis=0), k_l[:, sl])
            scores[j] = jnp.where(level, sc, 0.0 if scores[j] is None else scores[j])
        h //= 2
        yield

    q_in = q * jnp.exp(cum)
    c_end = cum[ts - 1:ts, :]
    k_out = k * jnp.exp(c_end - cum)
    diag = q * k
    for j in range(N_PAIRS):
        sl = slice(j * LANES, (j + 1) * LANES)
        vj = p_ref[0, :, 2 * GROUP + j * LANES:2 * GROUP + (j + 1) * LANES]
        gj = p_ref[0, :, 3 * GROUP + j * LANES:3 * GROUP + (j + 1) * LANES]
        pv = _dot(scores[j], vj)
        st_t = st_ref[j]
        out = (jnp.where(lo, pv[0:ts], pv[ts:2 * ts]) + _bsum(diag[:, sl], ones_bd) * vj
               + _dot_nt(q_in[:, sl], st_t))
        st_ref[j] = st_t * jnp.exp(c_end[:, sl]) + bd * _dot_tn(vj, k_out[:, sl])
        ms = _bsum(out * out, ones_bd) * (1.0 / HEAD_DIM)
        y = out * lax.rsqrt(ms + NORM_EPS) * normw_ref[:, sl]
        o_ref[0, :, sl] = (y * _sigmoid(gj)).astype(BF16)
        yield


def _hgrn_spec(p3, lb, norm_w, consts):
    b, s, _ = p3.shape
    tri, bd, ones_bd = consts
    row = lambda v: v.reshape(1, GROUP)
    args = [p3, row(jnp.log(lb)), row(jnp.log1p(-lb)), row(1.0 - lb), row(norm_w), tri, bd, ones_bd]
    return dict(
        name="hgrn", body=_hgrn_body, args=args,
        in_specs=[_seq_spec(4 * GROUP, COL_HGRN // (4 * GROUP))] + [_full(a) for a in args[1:]],
        out_specs=[_seq_spec(GROUP)], out_shape=[_out_bf16(b, s)],
        scratch=[pltpu.VMEM((N_PAIRS, LANES, LANES), F32)])


def _mlstm_body(pm_ref, pg_ref, convw_ref, convb_ref, gbias_ref, normw_ref, tri_ref, esel_ref,
                epf_ref, epi_ref, bd_ref, ones_ref, o_ref,
                prev_ref, cm_ref, n_ref, m_ref, mgl_ref):
    c = MIX_TS
    ones_bd = ones_ref[...]
    bd = bd_ref[...]

    x = pm_ref[0, :, 0:2 * GROUP]
    prev = prev_ref[...]
    acc = convb_ref[...] + x * convw_ref[CONV_WIDTH - 1:CONV_WIDTH, :]
    for sh in range(1, CONV_WIDTH):
        acc = acc + _shift_rows(x, prev, sh) * convw_ref[CONV_WIDTH - 1 - sh:CONV_WIDTH - sh, :]
    prev_ref[...] = x[c - 8:c]
    qk = _silu(acc)
    q_all = qk[:, 0:GROUP]
    k_all = qk[:, GROUP:2 * GROUP] * HEAD_DIM ** -0.5
    yield

    lane = lax.broadcasted_iota(jnp.int32, (c, LANES), 1)
    gb = pg_ref[0] + gbias_ref[...]
    gl = jnp.where(lane < N_HEADS, gb, jnp.where(lane < 2 * N_HEADS, _log_sigmoid(gb), 0.0))
    cum = _dot_3x(tri_ref[...], gl)
    mgl = mgl_ref[...]
    xm = cum + mgl
    colb = _dot_x3(xm, esel_ref[...])
    xm_t = xm.T
    gl_t = gl.T

    row = lax.broadcasted_iota(jnp.int32, (c, c), 0)
    col = lax.broadcasted_iota(jnp.int32, (c, c), 1)
    causal = col <= row
    dmat, w_inter, e_neg_m = [], [], []
    for h in range(N_HEADS):
        log_inter = colb[:, h * c:(h + 1) * c]
        logd = jnp.where(
            causal, log_inter + (gl_t[h:h + 1, :] - xm_t[N_HEADS + h:N_HEADS + h + 1, :]), NEG_BIG)
        m_t = jnp.maximum(jnp.max(logd, axis=1, keepdims=True), log_inter)
        dmat.append(jnp.exp(logd - m_t))
        w_inter.append(jnp.exp(log_inter - m_t))
        e_neg_m.append(jnp.exp(-m_t))
    yield

    cumcol = _dot_x3(cum, epf_ref[...])
    icol = _dot_x3(gl, epi_ref[...])
    c_end = cumcol[c - 1:c, :]
    m_old = m_ref[...]
    log_w = c_end - cumcol + icol
    m_new = jnp.maximum(c_end + m_old, jnp.max(log_w, axis=0, keepdims=True))
    decay = jnp.exp(c_end + m_old - m_new)
    kw_all = k_all * jnp.exp(log_w - m_new)
    n_old = n_ref[...]
    n_ref[...] = decay * n_old + jnp.sum(kw_all, axis=0, keepdims=True)
    m_ref[...] = m_new
    c_end_gl = cum[c - 1:c, :]
    log_w_gl = c_end_gl - cum + pltpu.roll(gl, N_HEADS, axis=1)
    m_new_gl = jnp.maximum(c_end_gl + mgl, jnp.max(log_w_gl, axis=0, keepdims=True))
    lane1 = lax.broadcasted_iota(jnp.int32, (1, LANES), 1)
    mgl_ref[...] = jnp.where((lane1 >= N_HEADS) & (lane1 < 2 * N_HEADS), m_new_gl, 0.0)
    yield

    lo = lane < HEAD_DIM
    pairs = range(N_PAIRS)
    sls = [slice(j * LANES, (j + 1) * LANES) for j in pairs]
    ones_full = jnp.ones((c, LANES), BF16)
    v1s = [jnp.concatenate(
        [pm_ref[0, :, 2 * GROUP + j * LANES:2 * GROUP + (j + 1) * LANES].astype(BF16), ones_full],
        axis=1) for j in pairs]
    cms = [cm_ref[j] for j in pairs]
    scores = [_dot_nt(_stack2(q_all[:, sls[j]], lo), k_all[:, sls[j]])
              * jnp.concatenate([dmat[2 * j], dmat[2 * j + 1]], axis=0) for j in pairs]
    pvr = [_dot(scores[j], v1s[j]) for j in pairs]
    qcs = [_dot(q_all[:, sls[j]], cms[j]) for j in pairs]
    qns = [_bsum(q_all[:, sls[j]] * n_old[:, sls[j]], ones_bd) for j in pairs]
    yield
    for j in pairs:
        cm_ref[j] = cms[j] * decay[:, sls[j]] + bd * _dot_tn(kw_all[:, sls[j]], v1s[j][:, 0:LANES])
    hhs = []
    for j in pairs:
        wi = jnp.where(lo, w_inter[2 * j], w_inter[2 * j + 1])
        num = jnp.where(lo, pvr[j][0:c, 0:LANES], pvr[j][c:2 * c, 0:LANES]) + wi * qcs[j]
        den = (jnp.where(lo, pvr[j][0:c, LANES:2 * LANES], pvr[j][c:2 * c, LANES:2 * LANES])
               + wi * qns[j])
        floor = jnp.where(lo, e_neg_m[2 * j], e_neg_m[2 * j + 1])
        hhs.append(num / jnp.maximum(jnp.abs(den), floor))
    mss = [_bsum(hhs[j] * hhs[j], ones_bd) * (1.0 / HEAD_DIM) for j in pairs]
    for j in pairs:
        oj = pm_ref[0, :, 3 * GROUP + j * LANES:3 * GROUP + (j + 1) * LANES]
        y = hhs[j] * lax.rsqrt(mss[j] + NORM_EPS) * normw_ref[:, sls[j]]
        o_ref[0, :, sls[j]] = (_sigmoid(oj) * y).astype(BF16)
    yield


def _mlstm_spec(p3, conv_w, conv_b, i_bias, f_bias, norm_w, consts):
    b, s, _ = p3.shape
    tri, esel, epf, epi, bd, ones_bd = consts
    gbias = jnp.zeros((1, LANES), F32).at[0, 0:N_HEADS].set(i_bias).at[0, N_HEADS:2 * N_HEADS].set(f_bias)
    args = [p3, p3, conv_w, conv_b.reshape(1, -1), gbias, norm_w.reshape(1, GROUP), tri, esel, epf,
            epi, bd, ones_bd]
    return dict(
        name="mlstm", body=_mlstm_body, args=args,
        in_specs=[_seq_spec(4 * GROUP, COL_ML // (4 * GROUP)), _seq_spec(LANES, COL_ML_GATES // LANES)]
        + [_full(a) for a in args[2:]],
        out_specs=[_seq_spec(GROUP)], out_shape=[_out_bf16(b, s)],
        scratch=[pltpu.VMEM((8, 2 * GROUP), F32),
                 pltpu.VMEM((N_PAIRS, LANES, LANES), F32),
                 pltpu.VMEM((1, GROUP), F32),
                 pltpu.VMEM((1, GROUP), F32),
                 pltpu.VMEM((1, LANES), F32)])


def _rwkv_body(first_layer, *refs):
    ts, c = MIX_TS, RWKV_C
    n_ch = ts // c
    if first_layer:
        (prkv_ref, plora_ref, mu_rkv_ref, mu_lora_ref, w0_ref, wup_ref, a0_ref, aup_ref, gup_ref,
         kk_ref, ka_ref, rk_ref, lnw_ref, lnb_ref, tri_ref, ones_ref,
         o_ref, vfirst_out_ref, prev_rkv_ref, prev_lora_ref, st_ref) = refs
    else:
        (prkv_ref, plora_ref, vfirst_ref, mu_rkv_ref, mu_lora_ref, w0_ref, wup_ref, a0_ref, aup_ref,
         gup_ref, kk_ref, ka_ref, rk_ref, lnw_ref, lnb_ref, v0_ref, vdown_ref, vup_ref, tri_ref,
         ones_ref, o_ref, prev_rkv_ref, prev_lora_ref, st_ref) = refs

    ones_bd = ones_ref[...]

    def token_mix(x_ref, prev_ref, mu_ref):
        x = x_ref[0]
        shifted = _shift_rows(x, prev_ref[...], 1)
        prev_ref[...] = x[ts - 8:ts]
        return x + (shifted - x) * mu_ref[...]

    rkv = token_mix(prkv_ref, prev_rkv_ref, mu_rkv_ref)
    lora = token_mix(plora_ref, prev_lora_ref, mu_lora_ref)
    r = rkv[:, 0:GROUP]
    k = rkv[:, GROUP:2 * GROUP]
    v = rkv[:, 2 * GROUP:3 * GROUP]
    wa = lora[:, 0:LANES]
    log_w = -RWKV_DECAY_SCALE * _sigmoid(w0_ref[...] + _dot(jnp.tanh(wa), wup_ref[...]))
    a = _sigmoid(a0_ref[...] + _dot(wa, aup_ref[...]))
    g = _dot(_sigmoid(lora[:, LANES:2 * LANES]), gup_ref[...])
    if first_layer:
        vfirst_out_ref[0] = v
    else:
        mix = _sigmoid(v0_ref[...] + _dot(_dot(v, vdown_ref[...]), vup_ref[...]))
        v = v + (vfirst_ref[0] - v) * mix

    kk_raw = k * kk_ref[...]
    k2 = k * (1.0 + (a - 1.0) * ka_ref[...])
    cum = _dot_3x(tri_ref[...], log_w)
    c_end = [cum[(ci + 1) * c - 1:(ci + 1) * c, :] for ci in range(n_ch)]
    c_end_rows = jnp.concatenate([jnp.broadcast_to(e, (c, GROUP)) for e in c_end], axis=0)
    r_g = r * jnp.exp(cum)
    e_prev = jnp.exp(cum - log_w)
    e_neg = jnp.exp(-cum)
    e_end = jnp.exp(c_end_rows - cum)
    rk_term = r * k2 * rk_ref[...]
    yield

    lane = lax.broadcasted_iota(jnp.int32, (c, LANES), 1)
    lo = lane < HEAD_DIM
    row = lax.broadcasted_iota(jnp.int32, (2 * c, 2 * c), 0)
    col = lax.broadcasted_iota(jnp.int32, (2 * c, 2 * c), 1)
    strict = col < row
    incl = col <= row
    eye = (col == row).astype(F32)

    kk_all, b_all = [], []
    for j in range(N_PAIRS):
        sl = slice(j * LANES, (j + 1) * LANES)
        kkr = kk_raw[:, sl]
        kk = kkr / jnp.maximum(jnp.sqrt(_bsum(kkr * kkr, ones_bd)), 1e-12)
        kk_all.append(kk)
        b_all.append(a[:, sl] * kk)

    insts = [(ci, j) for ci in range(n_ch) for j in range(N_PAIRS)]
    st2 = {}
    for ci, j in insts:
        rs = slice(ci * c, (ci + 1) * c)
        sl = slice(j * LANES, (j + 1) * LANES)
        st2[ci, j] = dict(
            r_t=_stack2(r_g[rs, sl].astype(BF16), lo),
            kk_t=_stack2((kk_all[j][rs] * e_prev[rs, sl]).astype(BF16), lo),
            k_h=_stack2((k2[rs, sl] * e_neg[rs, sl]).astype(BF16), lo),
            b_h=_stack2((b_all[j][rs] * e_neg[rs, sl]).astype(BF16), lo),
            k_e=_stack2((k2[rs, sl] * e_end[rs, sl]).astype(BF16), lo),
            b_e=_stack2((b_all[j][rs] * e_end[rs, sl]).astype(BF16), lo),
            v2=_stack2(v[rs, sl].astype(BF16), lo))
    yield

    def chunk_stages(wave):
        for key in wave:
            d = st2[key]
            prod = _dot_nt(jnp.concatenate([d["kk_t"], d["r_t"]], axis=0),
                           jnp.concatenate([d["b_h"], d["k_h"]], axis=0)).astype(BF16)
            d["lmat"] = jnp.where(strict, prod[0:2 * c, 0:2 * c], 0.0)
            d["a_kk"] = jnp.where(strict, prod[0:2 * c, 2 * c:4 * c], 0.0)
            d["a_rb"] = jnp.where(incl, prod[2 * c:4 * c, 0:2 * c], 0.0)
            d["a_rk"] = jnp.where(incl, prod[2 * c:4 * c, 2 * c:4 * c], 0.0)
        yield
        for key in wave:
            d = st2[key]
            d["s"] = eye - d["lmat"]
            d["p"] = _dot(d["lmat"], d["lmat"]).astype(BF16)
        yield
        for _ in range(4):
            for key in wave:
                d = st2[key]
                both = _dot(jnp.concatenate([d["p"], d["s"].astype(BF16)], axis=0), d["p"])
                d["p"] = both[0:2 * c].astype(BF16)
                d["s"] = d["s"] + both[2 * c:4 * c]
            yield
        for key in wave:
            d = st2[key]
            d["s"] = d["s"] + _dot(d["s"], d["p"])
            d["akv"] = _dot(d["a_kk"], d["v2"]).astype(BF16)
        yield
        for key in wave:
            d = st2[key]
            tw = -_dot(d["s"], jnp.concatenate([d["kk_t"], d["akv"]], axis=1))
            d["w"] = tw[:, 0:LANES].astype(BF16)
            d["u0"] = tw[:, LANES:2 * LANES].astype(BF16)
        yield
        for key in wave:
            d = st2[key]
            zero = jnp.zeros((2 * c, LANES), BF16)
            rhs = jnp.concatenate([jnp.concatenate([d["v2"], zero], axis=1),
                                   jnp.concatenate([d["u0"], d["w"]], axis=1)], axis=0)
            yq = _dot(jnp.concatenate([d["a_rk"], d["a_rb"]], axis=1), rhs)
            d["y0"] = yq[:, 0:LANES]
            d["q"] = d["r_t"] + yq[:, LANES:2 * LANES]
            d["m"] = _dot_tn(d["b_e"], d["w"])
            d["n_t"] = _dot_tn(jnp.concatenate([d["v2"], d["u0"]], axis=0),
                               jnp.concatenate([d["k_e"], d["b_e"]], axis=0))
        yield

    for w0 in range(0, len(insts), RWKV_WAVE):
        yield from chunk_stages(insts[w0:w0 + RWKV_WAVE])

    hts = [st_ref[j] for j in range(N_PAIRS)]
    ys = [[] for _ in range(N_PAIRS)]
    for ci in range(n_ch):
        for j in range(N_PAIRS):
            sl = slice(j * LANES, (j + 1) * LANES)
            d = st2[ci, j]
            ht = hts[j]
            y2 = _dot_nt(d["q"], ht) + d["y0"]
            hts[j] = ht * jnp.exp(c_end[ci][:, sl]) + _dot_nt(ht, d["m"]) + d["n_t"]
            ys[j].append(y2[0:c] + y2[c:2 * c])
    for j in range(N_PAIRS):
        st_ref[j] = hts[j]
    yield
    y_pairs = [jnp.concatenate(ys[j], axis=0) for j in range(N_PAIRS)]
    means = [_dot(y, ones_bd) * (1.0 / HEAD_DIM) for y in y_pairs]
    ycs = [y - m for y, m in zip(y_pairs, means)]
    vars_ = [_dot(yc * yc, ones_bd) * (1.0 / HEAD_DIM) for yc in ycs]
    for j in range(N_PAIRS):
        sl = slice(j * LANES, (j + 1) * LANES)
        yn = ycs[j] * lax.rsqrt(vars_[j] + RWKV_LN_EPS) * lnw_ref[:, sl] + lnb_ref[:, sl]
        bonus = _bsum(rk_term[:, sl], ones_bd) * v[:, sl]
        o_ref[0, :, sl] = ((yn + bonus) * g[:, sl]).astype(BF16)
    yield


def _rwkv_spec(p3, v_first, prm, consts):
    b, s, _ = p3.shape
    tri, ones_bd = consts
    first_layer = v_first is None
    args = [p3, p3]
    in_specs = [_seq_spec(3 * GROUP, COL_RW // (3 * GROUP)),
                _seq_spec(RWKV_LORA_COLS, COL_RW_LORA // RWKV_LORA_COLS)]
    if not first_layer:
        args.append(v_first)
        in_specs.append(_seq_spec(GROUP))
    consts_in = [prm[n] for n in ("mu_rkv", "mu_lora", "w0", "w_up", "a0", "a_up", "g_up", "k_k",
                                  "k_a", "r_k", "ln_w", "ln_b")]
    if not first_layer:
        consts_in += [prm["v0"], prm["v_down"], prm["v_up"]]
    consts_in += [tri, ones_bd]
    args += consts_in
    in_specs += [_full(a) for a in consts_in]
    out_specs = [_seq_spec(GROUP)]
    out_shape = [_out_bf16(b, s)]
    if first_layer:
        out_specs.append(_seq_spec(GROUP))
        out_shape.append(jax.ShapeDtypeStruct((b, s, GROUP), F32))
    return dict(
        name="rwkv", body=functools.partial(_rwkv_body, first_layer), args=args,
        in_specs=in_specs, out_specs=out_specs, out_shape=out_shape,
        scratch=[pltpu.VMEM((8, 3 * GROUP), F32),
                 pltpu.VMEM((8, RWKV_LORA_COLS), F32),
                 pltpu.VMEM((N_PAIRS, LANES, LANES), F32)])


def _pad_cols(w, n):
    return jnp.pad(w, ((0, 0), (0, n - w.shape[1])))


def _pad_rows(w, n):
    return jnp.pad(w, ((0, n - w.shape[0]), (0, 0)))


def _layout_w_in(w):
    g = GROUP
    ml_gates = w[..., 12 * g:12 * g + 2 * N_HEADS]
    pad = jnp.zeros(w.shape[:-1] + (N_IN_PAD - w.shape[-1],), w.dtype)
    out = jnp.concatenate([w[..., 0:12 * g], w[..., 12 * g + 2 * N_HEADS:], ml_gates, pad], axis=-1)
    return out.astype(BF16)


def _rwkv_params(l, rwkv_mu, rwkv_w0, rwkv_w_up, rwkv_a0, rwkv_a_up, rwkv_g_up, rwkv_k_k, rwkv_k_a,
                 rwkv_r_k, rwkv_ln_w, rwkv_ln_b, rwkv_v0, rwkv_v_down, rwkv_v_up):
    g = GROUP
    mu = rwkv_mu[l]
    prm = {
        "mu_rkv": mu[0:3 * g].reshape(1, -1), "mu_lora": mu[3 * g:].reshape(1, -1),
        "w0": rwkv_w0[l].reshape(1, g), "a0": rwkv_a0[l].reshape(1, g),
        "w_up": _pad_rows(rwkv_w_up[l], LANES).astype(BF16),
        "a_up": jnp.concatenate([jnp.zeros_like(rwkv_w_up[l]), rwkv_a_up[l]], axis=0).astype(BF16),
        "g_up": rwkv_g_up[l].astype(BF16),
        "k_k": rwkv_k_k[l].reshape(1, g), "k_a": rwkv_k_a[l].reshape(1, g),
        "r_k": rwkv_r_k[l].reshape(1, g), "ln_w": rwkv_ln_w[l].reshape(1, g),
        "ln_b": rwkv_ln_b[l].reshape(1, g),
    }
    if l > 0:
        prm["v0"] = rwkv_v0[l - 1].reshape(1, g)
        prm["v_down"] = _pad_cols(rwkv_v_down[l - 1], LANES).astype(BF16)
        prm["v_up"] = _pad_rows(rwkv_v_up[l - 1], LANES).astype(BF16)
    return prm


def kernel(x, w_in, w_out, norm_pre_mix, norm_post_mix, norm_pre_ffn, norm_post_ffn, w_ffn_gate, w_ffn_up, w_ffn_down, hgrn_lb_logits, hgrn_norm_w, mlstm_conv_w, mlstm_conv_b, mlstm_i_bias, mlstm_f_bias, mlstm_norm_w, rwkv_mu, rwkv_w0, rwkv_w_up, rwkv_a0, rwkv_a_up, rwkv_g_up, rwkv_k_k, rwkv_k_a, rwkv_r_k, rwkv_ln_w, rwkv_ln_b, rwkv_v0, rwkv_v_down, rwkv_v_up):
    b, s, d = x.shape
    depth = w_in.shape[0]
    t = b * s
    g = GROUP

    lb_cum = jnp.cumsum(jax.nn.softmax(hgrn_lb_logits.astype(F32), axis=0), axis=0)
    lower_bounds = lb_cum - lb_cum[0]

    bd_f32 = jnp.asarray(_block_ones(), F32)
    ones_bd = jnp.asarray(_block_ones(), BF16)
    cos, sin = _rope_tables(s)
    ret_tables = (cos, sin) + _retention_tables() + (bd_f32, ones_bd)
    hgrn_consts = (jnp.asarray(_tri_incl(MIX_TS), BF16), bd_f32, ones_bd)
    mlstm_consts = _mlstm_tables() + (bd_f32, ones_bd)
    rwkv_consts = (jnp.asarray(_tri_incl(MIX_TS, RWKV_C), BF16), ones_bd)

    w_in_bf = _layout_w_in(w_in)
    w_out_bf = w_out.astype(BF16)
    w_down_bf = w_ffn_down.astype(BF16)

    h = x.reshape(t, d)
    u = _prenorm(h, norm_pre_mix[0])
    v_first = None
    for l in range(depth):
        p3 = _in_proj(u, w_in_bf, l).reshape(b, s, N_IN_PAD)

        prm = _rwkv_params(l, rwkv_mu, rwkv_w0, rwkv_w_up, rwkv_a0, rwkv_a_up, rwkv_g_up, rwkv_k_k,
                           rwkv_k_a, rwkv_r_k, rwkv_ln_w, rwkv_ln_b, rwkv_v0, rwkv_v_down, rwkv_v_up)
        specs = {
            "ret": _ret_spec(p3, ret_tables),
            "hgrn": _hgrn_spec(p3, lower_bounds[l], hgrn_norm_w[l], hgrn_consts),
            "mlstm": _mlstm_spec(p3, mlstm_conv_w[l], mlstm_conv_b[l], mlstm_i_bias[l],
                                 mlstm_f_bias[l], mlstm_norm_w[l], mlstm_consts),
            "rwkv": _rwkv_spec(p3, v_first, prm, rwkv_consts),
        }
        mixed = {}
        for group in MIXER_GROUPS:
            mixed.update(_run_mixers([specs[name] for name, _ in group], b, s,
                                     [delay for _, delay in group]))
        if v_first is None:
            v_first = mixed["rwkv"][1]

        outs = [mixed[name][0].reshape(t, g) for name in ("ret", "hgrn", "mlstm", "rwkv")]
        h, u = _out_proj(outs, w_out_bf, l, h, norm_post_mix[l], norm_pre_ffn[l])
        w_pre_next = norm_pre_mix[l + 1] if l + 1 < depth else norm_pre_mix[l]
        h, u = _ffn(u, h, l, w_ffn_gate, w_ffn_up, w_down_bf, norm_post_ffn[l], w_pre_next)
    return h.reshape(b, s, d)
```

```python
import functools

import numpy as np
import jax
import jax.numpy as jnp
from jax import lax
from jax.experimental import pallas as pl
from jax.experimental.pallas import tpu as pltpu

F32 = jnp.float32
BF16 = jnp.bfloat16

HEAD_DIM = 64
N_HEADS = 8
GROUP = N_HEADS * HEAD_DIM
LANES = 128
N_PAIRS = GROUP // LANES
NORM_EPS = 1e-6
ROPE_BASE = 10000.0
RWKV_DECAY_SCALE = 0.6065306597126334
RWKV_LN_EPS = 64e-5
RWKV_LORA_COLS = 256
CONV_WIDTH = 4
NEG_BIG = -1e30

MIX_TS = 128
MIX_G = 4
RWKV_C = 64
RWKV_WAVE = 8
MIXER_GROUPS = ((("rwkv", 0), ("hgrn", 0)), (("mlstm", 0), ("ret", 0)))

COL_RET = 0
COL_HGRN = 4 * GROUP
COL_ML = 8 * GROUP
COL_RW = 12 * GROUP
COL_RW_LORA = 15 * GROUP
COL_ML_GATES = 15 * GROUP + RWKV_LORA_COLS
N_IN_PAD = 16 * GROUP

VMEM_LIMIT = 56 * 1024 * 1024
VMEM_LIMIT_MAX = 62 * 1024 * 1024


def _cparams(n_axes):
    return pltpu.CompilerParams(dimension_semantics=("arbitrary",) * n_axes,
                                vmem_limit_bytes=VMEM_LIMIT)


def _dot(a, b):
    return jnp.dot(a.astype(BF16), b.astype(BF16), preferred_element_type=F32)


def _dot_nt(a, b):
    return lax.dot_general(a.astype(BF16), b.astype(BF16), (((1,), (1,)), ((), ())),
                           preferred_element_type=F32)


def _dot_tn(a, b):
    return lax.dot_general(a.astype(BF16), b.astype(BF16), (((0,), (0,)), ((), ())),
                           preferred_element_type=F32)


def _split3(x):
    hi = x.astype(BF16)
    r1 = x - hi.astype(F32)
    mid = r1.astype(BF16)
    lo = (r1 - mid.astype(F32)).astype(BF16)
    return hi, mid, lo


def _dot_x3(x, sel):
    hi, mid, lo = _split3(x)
    d = lambda t: jnp.dot(t, sel, preferred_element_type=F32)
    return d(hi) + d(mid) + d(lo)


def _dot_3x(sel, x):
    hi, mid, lo = _split3(x)
    d = lambda t: jnp.dot(sel, t, preferred_element_type=F32)
    return d(hi) + d(mid) + d(lo)


def _bsum(x, ones_bd):
    return jnp.dot(x.astype(BF16), ones_bd, preferred_element_type=F32)


def _sigmoid(x):
    return 1.0 / (1.0 + jnp.exp(-x))


def _silu(x):
    return x * _sigmoid(x)


def _log_sigmoid(x):
    return jnp.minimum(x, 0.0) - jnp.log1p(jnp.exp(-jnp.abs(x)))


def _rms_rows(x, w):
    return x * lax.rsqrt(jnp.mean(x * x, axis=-1, keepdims=True) + NORM_EPS) * w


def _shift_rows(x, prev8, k):
    rolled = pltpu.roll(x, k, axis=0)
    row8 = lax.broadcasted_iota(jnp.int32, (8, x.shape[1]), 0)
    top = jnp.where(row8 < k, pltpu.roll(prev8, k, axis=0), rolled[0:8])
    return jnp.concatenate([top, rolled[8:]], axis=0)


def _stack2(x, lo):
    return jnp.concatenate([jnp.where(lo, x, 0.0), jnp.where(lo, 0.0, x)], axis=0)


def _rms_kernel(x_ref, w_ref, u_ref):
    u_ref[...] = _rms_rows(x_ref[...], w_ref[...]).astype(BF16)


def _prenorm(x2d, w, tm=512):
    t, d = x2d.shape
    return pl.pallas_call(
        _rms_kernel,
        grid=(t // tm,),
        in_specs=[pl.BlockSpec((tm, d), lambda i: (i, 0)),
                  pl.BlockSpec((1, d), lambda i: (0, 0))],
        out_specs=pl.BlockSpec((tm, d), lambda i: (i, 0)),
        out_shape=jax.ShapeDtypeStruct((t, d), BF16),
        compiler_params=_cparams(1),
        name="prenorm",
    )(x2d, w.reshape(1, d))


def _mm_kernel(u_ref, w_ref, o_ref):
    o_ref[...] = jnp.dot(u_ref[...], w_ref[...], preferred_element_type=F32)


def _in_proj(u, w, layer, tm=2048, tn=1024):
    t, d = u.shape
    n = w.shape[2]
    tm = min(tm, t)
    return pl.pallas_call(
        _mm_kernel,
        grid=(t // tm, n // tn),
        in_specs=[pl.BlockSpec((tm, d), lambda i, j: (i, 0)),
                  pl.BlockSpec((None, d, tn), lambda i, j: (layer, 0, j))],
        out_specs=pl.BlockSpec((tm, tn), lambda i, j: (i, j)),
        out_shape=jax.ShapeDtypeStruct((t, n), F32),
        compiler_params=_cparams(2),
        name="in_proj",
    )(u, w)


def _outproj_kernel(o0_ref, o1_ref, o2_ref, o3_ref, w_ref, h_ref, wpost_ref, wpre_ref,
                    hout_ref, u_ref):
    acc = jnp.dot(o0_ref[...], w_ref[0 * GROUP:1 * GROUP, :], preferred_element_type=F32)
    acc += jnp.dot(o1_ref[...], w_ref[1 * GROUP:2 * GROUP, :], preferred_element_type=F32)
    acc += jnp.dot(o2_ref[...], w_ref[2 * GROUP:3 * GROUP, :], preferred_element_type=F32)
    acc += jnp.dot(o3_ref[...], w_ref[3 * GROUP:4 * GROUP, :], preferred_element_type=F32)
    h = h_ref[...] + _rms_rows(acc, wpost_ref[...])
    hout_ref[...] = h
    u_ref[...] = _rms_rows(h, wpre_ref[...]).astype(BF16)


def _out_proj(outs, w, layer, h, w_post, w_pre, tm=512):
    t, d = h.shape
    tm = min(tm, t)
    row = lambda i: (i, 0)
    fixed = lambda i: (0, 0)
    return pl.pallas_call(
        _outproj_kernel,
        grid=(t // tm,),
        in_specs=[pl.BlockSpec((tm, GROUP), row)] * 4 + [
            pl.BlockSpec((None, d, d), lambda i: (layer, 0, 0)),
            pl.BlockSpec((tm, d), row),
            pl.BlockSpec((1, d), fixed),
            pl.BlockSpec((1, d), fixed)],
        out_specs=[pl.BlockSpec((tm, d), row), pl.BlockSpec((tm, d), row)],
        out_shape=[jax.ShapeDtypeStruct((t, d), F32), jax.ShapeDtypeStruct((t, d), BF16)],
        compiler_params=_cparams(1),
        name="out_proj",
    )(*outs, w, h, w_post.reshape(1, d), w_pre.reshape(1, d))


def _ffn_up_kernel(u_ref, wg_ref, wu_ref, a_ref):
    u = u_ref[...]
    tf = a_ref.shape[1]
    for c0 in range(0, tf, tf // 2):
        cols = slice(c0, c0 + tf // 2)
        g = jnp.dot(u, wg_ref[:, cols].astype(BF16), preferred_element_type=F32)
        up = jnp.dot(u, wu_ref[:, cols].astype(BF16), preferred_element_type=F32)
        a_ref[:, cols] = (_silu(g) * up).astype(BF16)


def _ffn_down_kernel(a_ref, wd_ref, h_ref, wpost_ref, wpre_ref, hout_ref, u_next_ref):
    kk = pl.program_id(1)

    @pl.when(kk == 0)
    def _():
        hout_ref[...] = jnp.dot(a_ref[...], wd_ref[...], preferred_element_type=F32)

    @pl.when(kk > 0)
    def _():
        hout_ref[...] += jnp.dot(a_ref[...], wd_ref[...], preferred_element_type=F32)

    @pl.when(kk == pl.num_programs(1) - 1)
    def _():
        h = h_ref[...] + _rms_rows(hout_ref[...], wpost_ref[...])
        hout_ref[...] = h
        u_next_ref[...] = _rms_rows(h, wpre_ref[...]).astype(BF16)


def _ffn(u, h, layer, wg, wu, wd, w_post, w_pre_next, tm_up=2048, tf=512, tm_down=1024, tk=1408):
    t, d = h.shape
    f = wg.shape[2]
    tm_up, tm_down = min(tm_up, t), min(tm_down, t)
    act = pl.pallas_call(
        _ffn_up_kernel,
        grid=(t // tm_up, f // tf),
        in_specs=[pl.BlockSpec((tm_up, d), lambda i, j: (i, 0)),
                  pl.BlockSpec((None, d, tf), lambda i, j: (layer, 0, j)),
                  pl.BlockSpec((None, d, tf), lambda i, j: (layer, 0, j))],
        out_specs=pl.BlockSpec((tm_up, tf), lambda i, j: (i, j)),
        out_shape=jax.ShapeDtypeStruct((t, f), BF16),
        compiler_params=_cparams(2),
        name="ffn_up",
    )(u, wg, wu)
    row = lambda i, k: (i, 0)
    fixed = lambda i, k: (0, 0)
    return pl.pallas_call(
        _ffn_down_kernel,
        grid=(t // tm_down, f // tk),
        in_specs=[pl.BlockSpec((tm_down, tk), lambda i, k: (i, k)),
                  pl.BlockSpec((None, tk, d), lambda i, k: (layer, k, 0)),
                  pl.BlockSpec((tm_down, d), row),
                  pl.BlockSpec((1, d), fixed),
                  pl.BlockSpec((1, d), fixed)],
        out_specs=[pl.BlockSpec((tm_down, d), row), pl.BlockSpec((tm_down, d), row)],
        out_shape=[jax.ShapeDtypeStruct((t, d), F32), jax.ShapeDtypeStruct((t, d), BF16)],
        compiler_params=pltpu.CompilerParams(dimension_semantics=("arbitrary",) * 2,
                                             vmem_limit_bytes=VMEM_LIMIT_MAX),
        name="ffn_down",
    )(act, wd, h, w_post.reshape(1, d), w_pre_next.reshape(1, d))


def _block_ones():
    i = np.arange(LANES)
    return (i[:, None] // HEAD_DIM == i[None, :] // HEAD_DIM).astype(np.float32)


def _tri_incl(c, block=None):
    i = np.arange(c)
    m = i[None, :] <= i[:, None]
    if block is not None:
        m = m & (i[None, :] // block == i[:, None] // block)
    return m.astype(np.float32)


def _rope_tables(seq):
    half = HEAD_DIM // 2
    lane = np.arange(LANES)
    inv_freq = ROPE_BASE ** (-jnp.arange(half, dtype=F32) / half)
    ang = jnp.arange(seq, dtype=F32)[:, None] * inv_freq[None, :]
    cos, sin = jnp.cos(ang), jnp.sin(ang)
    fidx = (lane % HEAD_DIM) % half
    sign = np.where((lane % HEAD_DIM) < half, -1.0, 1.0).astype(np.float32)
    return cos[:, fidx], sin[:, fidx] * sign


def _retention_tables():
    c = MIX_TS
    log_gamma = jnp.log1p(-jnp.exp2(-5.0 - jnp.arange(N_HEADS, dtype=F32)))
    idx = jnp.arange(c, dtype=F32)
    rel = idx[:, None] - idx[None, :]
    intra = jnp.where(rel >= 0, jnp.exp(log_gamma[:, None, None] * jnp.maximum(rel, 0.0)), 0.0)
    q_dec = jnp.exp(log_gamma[:, None] * (idx + 1.0))
    k_dec = jnp.exp(log_gamma[:, None] * (c - 1.0 - idx))
    chunk_dec = jnp.exp(log_gamma * c)
    per_lane = lambda t: jnp.repeat(t.T, HEAD_DIM, axis=1)
    bd = jnp.asarray(_block_ones())
    lane_head = np.arange(LANES) // HEAD_DIM
    sdec = jnp.stack([bd * chunk_dec[2 * j + lane_head][None, :] for j in range(N_PAIRS)])
    intra2 = intra.reshape(N_PAIRS, 2 * c, c)
    return intra2, per_lane(q_dec), per_lane(k_dec), sdec


def _mlstm_tables():
    c = MIX_TS
    esel = np.zeros((LANES, N_HEADS * c), np.float32)
    epf = np.zeros((LANES, GROUP), np.float32)
    epi = np.zeros((LANES, GROUP), np.float32)
    for h in range(N_HEADS):
        esel[N_HEADS + h, h * c:(h + 1) * c] = 1.0
        epf[N_HEADS + h, h * HEAD_DIM:(h + 1) * HEAD_DIM] = 1.0
        epi[h, h * HEAD_DIM:(h + 1) * HEAD_DIM] = 1.0
    return (jnp.asarray(_tri_incl(c), BF16), jnp.asarray(esel, BF16), jnp.asarray(epf, BF16),
            jnp.asarray(epi, BF16))


def _fixed(ndim):
    return lambda i, t: (0,) * ndim


def _full(a):
    return pl.BlockSpec(a.shape, _fixed(a.ndim))


def _seq_spec(width, col_block=0):
    return pl.BlockSpec((MIX_G, MIX_TS, width), lambda i, t: (i, t, col_block))


def _mix_scratch(shape, dtype):
    return pltpu.VMEM((MIX_G,) + tuple(shape), dtype)


def _mixer_kernel(bodies, n_in, n_seq, n_out, n_scr, delays, *refs):
    ins, outs, scrs = [], [], []
    pos = 0
    for group, counts in ((ins, n_in), (outs, n_out), (scrs, n_scr)):
        for n in counts:
            group.append(refs[pos:pos + n])
            pos += n

    @pl.when(pl.program_id(1) == 0)
    def _():
        for scr in scrs:
            for ref in scr:
                ref[...] = jnp.zeros_like(ref)

    active, waits = [], []
    for body, i, o, s, nseq, delay in zip(bodies, ins, outs, scrs, n_seq, delays):
        for g in range(MIX_G):
            row = lambda ref: ref.at[pl.ds(g, 1)]
            args = ([row(r) for r in i[:nseq]] + list(i[nseq:]) + [row(r) for r in o]
                    + [r.at[g] for r in s])
            active.append(body(*args))
            waits.append(delay)
    while active:
        for gen in list(active):
            k = active.index(gen)
            if waits[k] > 0:
                waits[k] -= 1
            elif next(gen, "done") == "done":
                active.pop(k)
                waits.pop(k)


def _run_mixers(specs, b, s, delays=None):
    kern = functools.partial(_mixer_kernel, tuple(sp["body"] for sp in specs),
                             tuple(len(sp["args"]) for sp in specs),
                             tuple(sp["n_seq"] for sp in specs),
                             tuple(len(sp["out_shape"]) for sp in specs),
                             tuple(len(sp["scratch"]) for sp in specs),
                             tuple(delays) if delays else (0,) * len(specs))
    flat = lambda key: [x for sp in specs for x in sp[key]]
    res = pl.pallas_call(
        kern,
        grid=(b // MIX_G, s // MIX_TS),
        in_specs=flat("in_specs"),
        out_specs=flat("out_specs"),
        out_shape=flat("out_shape"),
        scratch_shapes=flat("scratch"),
        compiler_params=_cparams(2),
        name="mix_" + "_".join(sp["name"] for sp in specs),
    )(*flat("args"))
    outs, pos = {}, 0
    for sp in specs:
        n = len(sp["out_shape"])
        outs[sp["name"]] = res[pos:pos + n]
        pos += n
    return outs


def _out_bf16(b, s):
    return jax.ShapeDtypeStruct((b, s, GROUP), BF16)


def _ret_body(p_ref, cos_ref, sin_ref, intra_ref, qdec_ref, kdec_ref, sdec_ref, bd_ref,
              ones_ref, o_ref, st_ref):
    c = MIX_TS
    cos = cos_ref[...]
    sin = sin_ref[...]
    lane = lax.broadcasted_iota(jnp.int32, (c, LANES), 1)
    first_half = (lane & (HEAD_DIM - 1)) < HEAD_DIM // 2
    lo = lane < HEAD_DIM
    ones_bd = ones_ref[...]
    bd = bd_ref[...]

    def rot(t):
        swapped = jnp.where(first_half, pltpu.roll(t, LANES - HEAD_DIM // 2, axis=1),
                            pltpu.roll(t, HEAD_DIM // 2, axis=1))
        return t * cos + swapped * sin

    pairs = range(N_PAIRS)
    sls = [slice(j * LANES, (j + 1) * LANES) for j in pairs]
    qs = [rot(p_ref[0, :, j * LANES:(j + 1) * LANES]) for j in pairs]
    ks = [rot(p_ref[0, :, GROUP + j * LANES:GROUP + (j + 1) * LANES]) * HEAD_DIM ** -0.5
          for j in pairs]
    vbs = [p_ref[0, :, 2 * GROUP + j * LANES:2 * GROUP + (j + 1) * LANES].astype(BF16)
           for j in pairs]
    sts = [st_ref[j] for j in pairs]
    yield
    scores = [_dot_nt(_stack2(qs[j], lo), ks[j]) * intra_ref[j] for j in pairs]
    pvs = [_dot(scores[j], vbs[j]) for j in pairs]
    inters = [_dot(qs[j] * qdec_ref[:, sls[j]], sts[j]) for j in pairs]
    outs = [jnp.where(lo, pvs[j][0:c], pvs[j][c:2 * c]) + inters[j] for j in pairs]
    yield
    for j in pairs:
        st_ref[j] = sdec_ref[j] * sts[j] + bd * _dot_tn(ks[j] * kdec_ref[:, sls[j]], vbs[j])
    mss = [_bsum(outs[j] * outs[j], ones_bd) * (1.0 / HEAD_DIM) for j in pairs]
    for j in pairs:
        g = p_ref[0, :, 3 * GROUP + j * LANES:3 * GROUP + (j + 1) * LANES]
        o_ref[0, :, sls[j]] = (outs[j] * lax.rsqrt(mss[j] + NORM_EPS) * _silu(g)).astype(BF16)
    yield


def _ret_spec(p3, tables):
    b, s, _ = p3.shape
    c = MIX_TS
    cos, sin, intra, qdec, kdec, sdec, bd, ones_bd = tables
    tab_spec = pl.BlockSpec((c, LANES), lambda i, t: (t, 0))
    return dict(
        name="ret", body=_ret_body, n_seq=1,
        args=[p3, cos, sin, intra, qdec, kdec, sdec, bd, ones_bd],
        in_specs=[_seq_spec(4 * GROUP, COL_RET // (4 * GROUP)), tab_spec, tab_spec,
                  _full(intra), _full(qdec), _full(kdec), _full(sdec), _full(bd), _full(ones_bd)],
        out_specs=[_seq_spec(GROUP)], out_shape=[_out_bf16(b, s)],
        scratch=[_mix_scratch((N_PAIRS, LANES, LANES), F32)])


def _level_ref(cum, h):
    ts, n = cum.shape
    if h >= 8:
        return jnp.concatenate(
            [jnp.broadcast_to(cum[b0 + h - 1:b0 + h, :], (2 * h, n)) for b0 in range(0, ts, 2 * h)],
            axis=0)
    cum3 = cum.reshape(ts // 8, 8, n)
    brow = lambda i: jnp.broadcast_to(cum3[:, i:i + 1, :], cum3.shape).reshape(ts, n)
    sub = lax.broadcasted_iota(jnp.int32, (ts, n), 0) & 7
    if h == 4:
        return brow(3)
    if h == 2:
        return jnp.where(sub < 4, brow(1), brow(5))
    return jnp.where(sub < 2, brow(0), jnp.where(sub < 4, brow(2), jnp.where(sub < 6, brow(4), brow(6))))


def _hgrn_body(p_ref, loglb_ref, log1mlb_ref, onemlb_ref, normw_ref, tri_ref, bd_ref, ones_ref,
               o_ref, st_ref):
    ts = MIX_TS
    pf = p_ref[0, :, GROUP:2 * GROUP]
    q = _silu(p_ref[0, :, 0:GROUP]) * HEAD_DIM ** -0.5
    e_f = jnp.exp(-jnp.abs(pf))
    k = onemlb_ref[...] * (jnp.where(pf >= 0.0, e_f, 1.0) / (1.0 + e_f))
    a = loglb_ref[...]
    bb = log1mlb_ref[...] + (jnp.minimum(pf, 0.0) - jnp.log1p(e_f))
    log_f = jnp.maximum(a, bb) + jnp.log1p(jnp.exp(-jnp.abs(a - bb)))
    cum = _dot_3x(tri_ref[...], log_f)
    yield

    ones_bd = ones_ref[...]
    bd = bd_ref[...]
    lo = lax.broadcasted_iota(jnp.int32, (ts, LANES), 1) < HEAD_DIM
    row2 = lax.broadcasted_iota(jnp.int32, (2 * ts, ts), 0) & (ts - 1)
    col2 = lax.broadcasted_iota(jnp.int32, (2 * ts, ts), 1)
    diff2 = jnp.where(col2 < row2, row2 ^ col2, 0)

    lo4 = (lax.broadcasted_iota(jnp.int32, (ts, GROUP), 1) & HEAD_DIM) == 0
    q_a = jnp.where(lo4, q, 0.0)
    q_b = (q - q_a).astype(BF16)
    q_a = q_a.astype(BF16)
    k_bf = k.astype(BF16)
    scores = [None] * N_PAIRS
    h = ts // 2
    while h >= 1:
        e = jnp.exp(-jnp.abs(cum - _level_ref(cum, h))).astype(BF16)
        qa_l, qb_l, k_l = q_a * e, q_b * e, k_bf * e
        level = (diff2 >> (h.bit_length() - 1)) == 1
        for j in range(N_PAIRS):
            sl = slice(j * LANES, (j + 1) * LANES)
            sc = _dot_nt(jnp.concatenate([qa_l[:, sl], qb_l[:, sl]], axis=0), k_l[:, sl])
            scores[j] = jnp.where(level, sc, 0.0 if scores[j] is None else scores[j])
        h //= 2
        yield

    q_in = q * jnp.exp(cum)
    c_end = cum[ts - 1:ts, :]
    k_out = k * jnp.exp(c_end - cum)
    diag = q * k
    for j in range(N_PAIRS):
        sl = slice(j * LANES, (j + 1) * LANES)
        vj = p_ref[0, :, 2 * GROUP + j * LANES:2 * GROUP + (j + 1) * LANES]
        gj = p_ref[0, :, 3 * GROUP + j * LANES:3 * GROUP + (j + 1) * LANES]
        pv = _dot(scores[j], vj)
        st_t = st_ref[j]
        out = (jnp.where(lo, pv[0:ts], pv[ts:2 * ts]) + _bsum(diag[:, sl], ones_bd) * vj
               + _dot_nt(q_in[:, sl], st_t))
        st_ref[j] = st_t * jnp.exp(c_end[:, sl]) + bd * _dot_tn(vj, k_out[:, sl])
        ms = _bsum(out * out, ones_bd) * (1.0 / HEAD_DIM)
        y = out * lax.rsqrt(ms + NORM_EPS) * normw_ref[:, sl]
        o_ref[0, :, sl] = (y * _sigmoid(gj)).astype(BF16)
        yield


def _hgrn_spec(p3, lb, norm_w, consts):
    b, s, _ = p3.shape
    tri, bd, ones_bd = consts
    row = lambda v: v.reshape(1, GROUP)
    args = [p3, row(jnp.log(lb)), row(jnp.log1p(-lb)), row(1.0 - lb), row(norm_w), tri, bd, ones_bd]
    return dict(
        name="hgrn", body=_hgrn_body, n_seq=1, args=args,
        in_specs=[_seq_spec(4 * GROUP, COL_HGRN // (4 * GROUP))] + [_full(a) for a in args[1:]],
        out_specs=[_seq_spec(GROUP)], out_shape=[_out_bf16(b, s)],
        scratch=[_mix_scratch((N_PAIRS, LANES, LANES), F32)])


def _mlstm_body(pm_ref, pg_ref, convw_ref, convb_ref, gbias_ref, normw_ref, tri_ref, esel_ref,
                epf_ref, epi_ref, bd_ref, ones_ref, o_ref,
                prev_ref, cm_ref, n_ref, m_ref, mgl_ref):
    c = MIX_TS
    ones_bd = ones_ref[...]
    bd = bd_ref[...]

    x = pm_ref[0, :, 0:2 * GROUP]
    prev = prev_ref[...]
    acc = convb_ref[...] + x * convw_ref[CONV_WIDTH - 1:CONV_WIDTH, :]
    for sh in range(1, CONV_WIDTH):
        acc = acc + _shift_rows(x, prev, sh) * convw_ref[CONV_WIDTH - 1 - sh:CONV_WIDTH - sh, :]
    prev_ref[...] = x[c - 8:c]
    qk = _silu(acc)
    q_all = qk[:, 0:GROUP]
    k_all = qk[:, GROUP:2 * GROUP] * HEAD_DIM ** -0.5
    yield

    lane = lax.broadcasted_iota(jnp.int32, (c, LANES), 1)
    gb = pg_ref[0] + gbias_ref[...]
    gl = jnp.where(lane < N_HEADS, gb, jnp.where(lane < 2 * N_HEADS, _log_sigmoid(gb), 0.0))
    cum = _dot_3x(tri_ref[...], gl)
    mgl = mgl_ref[...]
    xm = cum + mgl
    colb = _dot_x3(xm, esel_ref[...])
    xm_t = xm.T
    gl_t = gl.T

    row = lax.broadcasted_iota(jnp.int32, (c, c), 0)
    col = lax.broadcasted_iota(jnp.int32, (c, c), 1)
    causal = col <= row
    dmat, w_inter, e_neg_m = [], [], []
    for h in range(N_HEADS):
        log_inter = colb[:, h * c:(h + 1) * c]
        logd = jnp.where(
            causal, log_inter + (gl_t[h:h + 1, :] - xm_t[N_HEADS + h:N_HEADS + h + 1, :]), NEG_BIG)
        m_t = jnp.maximum(jnp.max(logd, axis=1, keepdims=True), log_inter)
        dmat.append(jnp.exp(logd - m_t))
        w_inter.append(jnp.exp(log_inter - m_t))
        e_neg_m.append(jnp.exp(-m_t))
    yield

    cumcol = _dot_x3(cum, epf_ref[...])
    icol = _dot_x3(gl, epi_ref[...])
    c_end = cumcol[c - 1:c, :]
    m_old = m_ref[...]
    log_w = c_end - cumcol + icol
    m_new = jnp.maximum(c_end + m_old, jnp.max(log_w, axis=0, keepdims=True))
    decay = jnp.exp(c_end + m_old - m_new)
    kw_all = k_all * jnp.exp(log_w - m_new)
    n_old = n_ref[...]
    n_ref[...] = decay * n_old + jnp.sum(kw_all, axis=0, keepdims=True)
    m_ref[...] = m_new
    c_end_gl = cum[c - 1:c, :]
    log_w_gl = c_end_gl - cum + pltpu.roll(gl, N_HEADS, axis=1)
    m_new_gl = jnp.maximum(c_end_gl + mgl, jnp.max(log_w_gl, axis=0, keepdims=True))
    lane1 = lax.broadcasted_iota(jnp.int32, (1, LANES), 1)
    mgl_ref[...] = jnp.where((lane1 >= N_HEADS) & (lane1 < 2 * N_HEADS), m_new_gl, 0.0)
    yield

    lo = lane < HEAD_DIM
    pairs = range(N_PAIRS)
    sls = [slice(j * LANES, (j + 1) * LANES) for j in pairs]
    ones_full = jnp.ones((c, LANES), BF16)
    v1s = [jnp.concatenate(
        [pm_ref[0, :, 2 * GROUP + j * LANES:2 * GROUP + (j + 1) * LANES].astype(BF16), ones_full],
        axis=1) for j in pairs]
    cms = [cm_ref[j] for j in pairs]
    scores = [_dot_nt(_stack2(q_all[:, sls[j]], lo), k_all[:, sls[j]])
              * jnp.concatenate([dmat[2 * j], dmat[2 * j + 1]], axis=0) for j in pairs]
    pvr = [_dot(scores[j], v1s[j]) for j in pairs]
    qcs = [_dot(q_all[:, sls[j]], cms[j]) for j in pairs]
    qns = [_bsum(q_all[:, sls[j]] * n_old[:, sls[j]], ones_bd) for j in pairs]
    yield
    for j in pairs:
        cm_ref[j] = cms[j] * decay[:, sls[j]] + bd * _dot_tn(kw_all[:, sls[j]], v1s[j][:, 0:LANES])
    hhs = []
    for j in pairs:
        wi = jnp.where(lo, w_inter[2 * j], w_inter[2 * j + 1])
        num = jnp.where(lo, pvr[j][0:c, 0:LANES], pvr[j][c:2 * c, 0:LANES]) + wi * qcs[j]
        den = (jnp.where(lo, pvr[j][0:c, LANES:2 * LANES], pvr[j][c:2 * c, LANES:2 * LANES])
               + wi * qns[j])
        floor = jnp.where(lo, e_neg_m[2 * j], e_neg_m[2 * j + 1])
        hhs.append(num / jnp.maximum(jnp.abs(den), floor))
    mss = [_bsum(hhs[j] * hhs[j], ones_bd) * (1.0 / HEAD_DIM) for j in pairs]
    for j in pairs:
        oj = pm_ref[0, :, 3 * GROUP + j * LANES:3 * GROUP + (j + 1) * LANES]
        y = hhs[j] * lax.rsqrt(mss[j] + NORM_EPS) * normw_ref[:, sls[j]]
        o_ref[0, :, sls[j]] = (_sigmoid(oj) * y).astype(BF16)
    yield


def _mlstm_spec(p3, conv_w, conv_b, i_bias, f_bias, norm_w, consts):
    b, s, _ = p3.shape
    tri, esel, epf, epi, bd, ones_bd = consts
    gbias = jnp.zeros((1, LANES), F32).at[0, 0:N_HEADS].set(i_bias).at[0, N_HEADS:2 * N_HEADS].set(f_bias)
    args = [p3, p3, conv_w, conv_b.reshape(1, -1), gbias, norm_w.reshape(1, GROUP), tri, esel, epf,
            epi, bd, ones_bd]
    return dict(
        name="mlstm", body=_mlstm_body, n_seq=2, args=args,
        in_specs=[_seq_spec(4 * GROUP, COL_ML // (4 * GROUP)), _seq_spec(LANES, COL_ML_GATES // LANES)]
        + [_full(a) for a in args[2:]],
        out_specs=[_seq_spec(GROUP)], out_shape=[_out_bf16(b, s)],
        scratch=[_mix_scratch((8, 2 * GROUP), F32),
                 _mix_scratch((N_PAIRS, LANES, LANES), F32),
                 _mix_scratch((1, GROUP), F32),
                 _mix_scratch((1, GROUP), F32),
                 _mix_scratch((1, LANES), F32)])


def _rwkv_body(first_layer, *refs):
    ts, c = MIX_TS, RWKV_C
    n_ch = ts // c
    if first_layer:
        (prkv_ref, plora_ref, mu_rkv_ref, mu_lora_ref, w0_ref, wup_ref, a0_ref, aup_ref, gup_ref,
         kk_ref, ka_ref, rk_ref, lnw_ref, lnb_ref, tri_ref, ones_ref,
         o_ref, vfirst_out_ref, prev_rkv_ref, prev_lora_ref, st_ref) = refs
    else:
        (prkv_ref, plora_ref, vfirst_ref, mu_rkv_ref, mu_lora_ref, w0_ref, wup_ref, a0_ref, aup_ref,
         gup_ref, kk_ref, ka_ref, rk_ref, lnw_ref, lnb_ref, v0_ref, vdown_ref, vup_ref, tri_ref,
         ones_ref, o_ref, prev_rkv_ref, prev_lora_ref, st_ref) = refs

    ones_bd = ones_ref[...]

    def token_mix(x_ref, prev_ref, mu_ref):
        x = x_ref[0]
        shifted = _shift_rows(x, prev_ref[...], 1)
        prev_ref[...] = x[ts - 8:ts]
        return x + (shifted - x) * mu_ref[...]

    rkv = token_mix(prkv_ref, prev_rkv_ref, mu_rkv_ref)
    lora = token_mix(plora_ref, prev_lora_ref, mu_lora_ref)
    r = rkv[:, 0:GROUP]
    k = rkv[:, GROUP:2 * GROUP]
    v = rkv[:, 2 * GROUP:3 * GROUP]
    wa = lora[:, 0:LANES]
    log_w = -RWKV_DECAY_SCALE * _sigmoid(w0_ref[...] + _dot(jnp.tanh(wa), wup_ref[...]))
    a = _sigmoid(a0_ref[...] + _dot(wa, aup_ref[...]))
    g = _dot(_sigmoid(lora[:, LANES:2 * LANES]), gup_ref[...])
    if first_layer:
        vfirst_out_ref[0] = v
    else:
        mix = _sigmoid(v0_ref[...] + _dot(_dot(v, vdown_ref[...]), vup_ref[...]))
        v = v + (vfirst_ref[0] - v) * mix

    kk_raw = k * kk_ref[...]
    k2 = k * (1.0 + (a - 1.0) * ka_ref[...])
    cum = _dot_3x(tri_ref[...], log_w)
    c_end = [cum[(ci + 1) * c - 1:(ci + 1) * c, :] for ci in range(n_ch)]
    c_end_rows = jnp.concatenate([jnp.broadcast_to(e, (c, GROUP)) for e in c_end], axis=0)
    r_g = r * jnp.exp(cum)
    e_prev = jnp.exp(cum - log_w)
    e_neg = jnp.exp(-cum)
    e_end = jnp.exp(c_end_rows - cum)
    rk_term = r * k2 * rk_ref[...]
    yield

    lane = lax.broadcasted_iota(jnp.int32, (c, LANES), 1)
    lo = lane < HEAD_DIM
    row = lax.broadcasted_iota(jnp.int32, (2 * c, 2 * c), 0)
    col = lax.broadcasted_iota(jnp.int32, (2 * c, 2 * c), 1)
    strict = col < row
    incl = col <= row
    eye = (col == row).astype(F32)

    kk_all, b_all = [], []
    for j in range(N_PAIRS):
        sl = slice(j * LANES, (j + 1) * LANES)
        kkr = kk_raw[:, sl]
        kk = kkr / jnp.maximum(jnp.sqrt(_bsum(kkr * kkr, ones_bd)), 1e-12)
        kk_all.append(kk)
        b_all.append(a[:, sl] * kk)

    insts = [(ci, j) for ci in range(n_ch) for j in range(N_PAIRS)]
    st2 = {}
    for ci, j in insts:
        rs = slice(ci * c, (ci + 1) * c)
        sl = slice(j * LANES, (j + 1) * LANES)
        st2[ci, j] = dict(
            r_t=_stack2(r_g[rs, sl].astype(BF16), lo),
            kk_t=_stack2((kk_all[j][rs] * e_prev[rs, sl]).astype(BF16), lo),
            k_h=_stack2((k2[rs, sl] * e_neg[rs, sl]).astype(BF16), lo),
            b_h=_stack2((b_all[j][rs] * e_neg[rs, sl]).astype(BF16), lo),
            k_e=_stack2((k2[rs, sl] * e_end[rs, sl]).astype(BF16), lo),
            b_e=_stack2((b_all[j][rs] * e_end[rs, sl]).astype(BF16), lo),
            v2=_stack2(v[rs, sl].astype(BF16), lo))
    yield

    def chunk_stages(wave):
        for key in wave:
            d = st2[key]
            prod = _dot_nt(jnp.concatenate([d["kk_t"], d["r_t"]], axis=0),
                           jnp.concatenate([d["b_h"], d["k_h"]], axis=0)).astype(BF16)
            d["lmat"] = jnp.where(strict, prod[0:2 * c, 0:2 * c], 0.0)
            d["a_kk"] = jnp.where(strict, prod[0:2 * c, 2 * c:4 * c], 0.0)
            d["a_rb"] = jnp.where(incl, prod[2 * c:4 * c, 0:2 * c], 0.0)
            d["a_rk"] = jnp.where(incl, prod[2 * c:4 * c, 2 * c:4 * c], 0.0)
        yield
        for key in wave:
            d = st2[key]
            d["s"] = eye - d["lmat"]
            d["p"] = _dot(d["lmat"], d["lmat"]).astype(BF16)
        yield
        for _ in range(4):
            for key in wave:
                d = st2[key]
                both = _dot(jnp.concatenate([d["p"], d["s"].astype(BF16)], axis=0), d["p"])
                d["p"] = both[0:2 * c].astype(BF16)
                d["s"] = d["s"] + both[2 * c:4 * c]
            yield
        for key in wave:
            d = st2[key]
            d["s"] = d["s"] + _dot(d["s"], d["p"])
            d["akv"] = _dot(d["a_kk"], d["v2"]).astype(BF16)
        yield
        for key in wave:
            d = st2[key]
            tw = -_dot(d["s"], jnp.concatenate([d["kk_t"], d["akv"]], axis=1))
            d["w"] = tw[:, 0:LANES].astype(BF16)
            d["u0"] = tw[:, LANES:2 * LANES].astype(BF16)
        yield
        for key in wave:
            d = st2[key]
            zero = jnp.zeros((2 * c, LANES), BF16)
            rhs = jnp.concatenate([jnp.concatenate([d["v2"], zero], axis=1),
                                   jnp.concatenate([d["u0"], d["w"]], axis=1)], axis=0)
            yq = _dot(jnp.concatenate([d["a_rk"], d["a_rb"]], axis=1), rhs)
            d["y0"] = yq[:, 0:LANES]
            d["q"] = d["r_t"] + yq[:, LANES:2 * LANES]
            d["m"] = _dot_tn(d["b_e"], d["w"])
            d["n_t"] = _dot_tn(jnp.concatenate([d["v2"], d["u0"]], axis=0),
                               jnp.concatenate([d["k_e"], d["b_e"]], axis=0))
        yield

    for w0 in range(0, len(insts), RWKV_WAVE):
        yield from chunk_stages(insts[w0:w0 + RWKV_WAVE])

    hts = [st_ref[j] for j in range(N_PAIRS)]
    ys = [[] for _ in range(N_PAIRS)]
    for ci in range(n_ch):
        for j in range(N_PAIRS):
            sl = slice(j * LANES, (j + 1) * LANES)
            d = st2[ci, j]
            ht = hts[j]
            y2 = _dot_nt(d["q"], ht) + d["y0"]
            hts[j] = ht * jnp.exp(c_end[ci][:, sl]) + _dot_nt(ht, d["m"]) + d["n_t"]
            ys[j].append(y2[0:c] + y2[c:2 * c])
    for j in range(N_PAIRS):
        st_ref[j] = hts[j]
    yield
    y_pairs = [jnp.concatenate(ys[j], axis=0) for j in range(N_PAIRS)]
    means = [_dot(y, ones_bd) * (1.0 / HEAD_DIM) for y in y_pairs]
    ycs = [y - m for y, m in zip(y_pairs, means)]
    vars_ = [_dot(yc * yc, ones_bd) * (1.0 / HEAD_DIM) for yc in ycs]
    for j in range(N_PAIRS):
        sl = slice(j * LANES, (j + 1) * LANES)
        yn = ycs[j] * lax.rsqrt(vars_[j] + RWKV_LN_EPS) * lnw_ref[:, sl] + lnb_ref[:, sl]
        bonus = _bsum(rk_term[:, sl], ones_bd) * v[:, sl]
        o_ref[0, :, sl] = ((yn + bonus) * g[:, sl]).astype(BF16)
    yield


def _rwkv_spec(p3, v_first, prm, consts):
    b, s, _ = p3.shape
    tri, ones_bd = consts
    first_layer = v_first is None
    args = [p3, p3]
    in_specs = [_seq_spec(3 * GROUP, COL_RW // (3 * GROUP)),
                _seq_spec(RWKV_LORA_COLS, COL_RW_LORA // RWKV_LORA_COLS)]
    if not first_layer:
        args.append(v_first)
        in_specs.append(_seq_spec(GROUP))
    consts_in = [prm[n] for n in ("mu_rkv", "mu_lora", "w0", "w_up", "a0", "a_up", "g_up", "k_k",
                                  "k_a", "r_k", "ln_w", "ln_b")]
    if not first_layer:
        consts_in += [prm["v0"], prm["v_down"], prm["v_up"]]
    consts_in += [tri, ones_bd]
    args += consts_in
    in_specs += [_full(a) for a in consts_in]
    out_specs = [_seq_spec(GROUP)]
    out_shape = [_out_bf16(b, s)]
    if first_layer:
        out_specs.append(_seq_spec(GROUP))
        out_shape.append(jax.ShapeDtypeStruct((b, s, GROUP), F32))
    return dict(
        name="rwkv", body=functools.partial(_rwkv_body, first_layer),
        n_seq=2 if first_layer else 3, args=args,
        in_specs=in_specs, out_specs=out_specs, out_shape=out_shape,
        scratch=[_mix_scratch((8, 3 * GROUP), F32),
                 _mix_scratch((8, RWKV_LORA_COLS), F32),
                 _mix_scratch((N_PAIRS, LANES, LANES), F32)])


def _pad_cols(w, n):
    return jnp.pad(w, ((0, 0), (0, n - w.shape[1])))


def _pad_rows(w, n):
    return jnp.pad(w, ((0, n - w.shape[0]), (0, 0)))


def _layout_w_in(w):
    g = GROUP
    ml_gates = w[..., 12 * g:12 * g + 2 * N_HEADS]
    pad = jnp.zeros(w.shape[:-1] + (N_IN_PAD - w.shape[-1],), w.dtype)
    out = jnp.concatenate([w[..., 0:12 * g], w[..., 12 * g + 2 * N_HEADS:], ml_gates, pad], axis=-1)
    return out.astype(BF16)


def _rwkv_params(l, rwkv_mu, rwkv_w0, rwkv_w_up, rwkv_a0, rwkv_a_up, rwkv_g_up, rwkv_k_k, rwkv_k_a,
                 rwkv_r_k, rwkv_ln_w, rwkv_ln_b, rwkv_v0, rwkv_v_down, rwkv_v_up):
    g = GROUP
    mu = rwkv_mu[l]
    prm = {
        "mu_rkv": mu[0:3 * g].reshape(1, -1), "mu_lora": mu[3 * g:].reshape(1, -1),
        "w0": rwkv_w0[l].reshape(1, g), "a0": rwkv_a0[l].reshape(1, g),
        "w_up": _pad_rows(rwkv_w_up[l], LANES).astype(BF16),
        "a_up": jnp.concatenate([jnp.zeros_like(rwkv_w_up[l]), rwkv_a_up[l]], axis=0).astype(BF16),
        "g_up": rwkv_g_up[l].astype(BF16),
        "k_k": rwkv_k_k[l].reshape(1, g), "k_a": rwkv_k_a[l].reshape(1, g),
        "r_k": rwkv_r_k[l].reshape(1, g), "ln_w": rwkv_ln_w[l].reshape(1, g),
        "ln_b": rwkv_ln_b[l].reshape(1, g),
    }
    if l > 0:
        prm["v0"] = rwkv_v0[l - 1].reshape(1, g)
        prm["v_down"] = _pad_cols(rwkv_v_down[l - 1], LANES).astype(BF16)
        prm["v_up"] = _pad_rows(rwkv_v_up[l - 1], LANES).astype(BF16)
    return prm


def kernel(x, w_in, w_out, norm_pre_mix, norm_post_mix, norm_pre_ffn, norm_post_ffn, w_ffn_gate, w_ffn_up, w_ffn_down, hgrn_lb_logits, hgrn_norm_w, mlstm_conv_w, mlstm_conv_b, mlstm_i_bias, mlstm_f_bias, mlstm_norm_w, rwkv_mu, rwkv_w0, rwkv_w_up, rwkv_a0, rwkv_a_up, rwkv_g_up, rwkv_k_k, rwkv_k_a, rwkv_r_k, rwkv_ln_w, rwkv_ln_b, rwkv_v0, rwkv_v_down, rwkv_v_up):
    b, s, d = x.shape
    depth = w_in.shape[0]
    t = b * s
    g = GROUP

    lb_cum = jnp.cumsum(jax.nn.softmax(hgrn_lb_logits.astype(F32), axis=0), axis=0)
    lower_bounds = lb_cum - lb_cum[0]

    bd_f32 = jnp.asarray(_block_ones(), F32)
    ones_bd = jnp.asarray(_block_ones(), BF16)
    cos, sin = _rope_tables(s)
    ret_tables = (cos, sin) + _retention_tables() + (bd_f32, ones_bd)
    hgrn_consts = (jnp.asarray(_tri_incl(MIX_TS), BF16), bd_f32, ones_bd)
    mlstm_consts = _mlstm_tables() + (bd_f32, ones_bd)
    rwkv_consts = (jnp.asarray(_tri_incl(MIX_TS, RWKV_C), BF16), ones_bd)

    w_in_bf = _layout_w_in(w_in)
    w_out_bf = w_out.astype(BF16)
    w_down_bf = w_ffn_down.astype(BF16)

    h = x.reshape(t, d)
    u = _prenorm(h, norm_pre_mix[0])
    v_first = None
    for l in range(depth):
        p3 = _in_proj(u, w_in_bf, l).reshape(b, s, N_IN_PAD)

        prm = _rwkv_params(l, rwkv_mu, rwkv_w0, rwkv_w_up, rwkv_a0, rwkv_a_up, rwkv_g_up, rwkv_k_k,
                           rwkv_k_a, rwkv_r_k, rwkv_ln_w, rwkv_ln_b, rwkv_v0, rwkv_v_down, rwkv_v_up)
        specs = {
            "ret": _ret_spec(p3, ret_tables),
            "hgrn": _hgrn_spec(p3, lower_bounds[l], hgrn_norm_w[l], hgrn_consts),
            "mlstm": _mlstm_spec(p3, mlstm_conv_w[l], mlstm_conv_b[l], mlstm_i_bias[l],
                                 mlstm_f_bias[l], mlstm_norm_w[l], mlstm_consts),
            "rwkv": _rwkv_spec(p3, v_first, prm, rwkv_consts),
        }
        mixed = {}
        for group in MIXER_GROUPS:
            mixed.update(_run_mixers([specs[name] for name, _ in group], b, s,
                                     [delay for _, delay in group]))
        if v_first is None:
            v_first = mixed["rwkv"][1]

        outs = [mixed[name][0].reshape(t, g) for name in ("ret", "hgrn", "mlstm", "rwkv")]
        h, u = _out_proj(outs, w_out_bf, l, h, norm_post_mix[l], norm_pre_ffn[l])
        w_pre_next = norm_pre_mix[l + 1] if l + 1 < depth else norm_pre_mix[l]
        h, u = _ffn(u, h, l, w_ffn_gate, w_ffn_up, w_down_bf, norm_post_ffn[l], w_pre_next)
    return h.reshape(b, s, d)
```

```python
import functools

import numpy as np
import jax
import jax.numpy as jnp
from jax import lax
from jax.experimental import pallas as pl
from jax.experimental.pallas import tpu as pltpu

F32 = jnp.float32
BF16 = jnp.bfloat16

HEAD_DIM = 64
N_HEADS = 8
GROUP = N_HEADS * HEAD_DIM
LANES = 128
N_PAIRS = GROUP // LANES
NORM_EPS = 1e-6
ROPE_BASE = 10000.0
RWKV_DECAY_SCALE = 0.6065306597126334
RWKV_LN_EPS = 64e-5
RWKV_LORA_COLS = 256
CONV_WIDTH = 4
NEG_BIG = -1e30

MIX_TS = 128
MIX_G = 4
RWKV_C = 64
RWKV_WAVE = 8
MIXER_GROUPS = ((("rwkv", 0), ("hgrn", 0)), (("mlstm", 0), ("ret", 0)))

COL_RET = 0
COL_HGRN = 4 * GROUP
COL_ML = 8 * GROUP
COL_RW = 12 * GROUP
COL_RW_LORA = 15 * GROUP
COL_ML_GATES = 15 * GROUP + RWKV_LORA_COLS
N_IN_PAD = 16 * GROUP

VMEM_LIMIT = 56 * 1024 * 1024
VMEM_LIMIT_MAX = 62 * 1024 * 1024


def _cparams(n_axes):
    return pltpu.CompilerParams(dimension_semantics=("arbitrary",) * n_axes,
                                vmem_limit_bytes=VMEM_LIMIT)


def _dot(a, b):
    return jnp.dot(a.astype(BF16), b.astype(BF16), preferred_element_type=F32)


def _dot_nt(a, b):
    return lax.dot_general(a.astype(BF16), b.astype(BF16), (((1,), (1,)), ((), ())),
                           preferred_element_type=F32)


def _dot_tn(a, b):
    return lax.dot_general(a.astype(BF16), b.astype(BF16), (((0,), (0,)), ((), ())),
                           preferred_element_type=F32)


def _split3(x):
    hi = x.astype(BF16)
    r1 = x - hi.astype(F32)
    mid = r1.astype(BF16)
    lo = (r1 - mid.astype(F32)).astype(BF16)
    return hi, mid, lo


def _dot_x3(x, sel):
    hi, mid, lo = _split3(x)
    d = lambda t: jnp.dot(t, sel, preferred_element_type=F32)
    return d(hi) + d(mid) + d(lo)


def _dot_3x(sel, x):
    hi, mid, lo = _split3(x)
    d = lambda t: jnp.dot(sel, t, preferred_element_type=F32)
    return d(hi) + d(mid) + d(lo)


def _bsum(x, ones_bd):
    return jnp.dot(x.astype(BF16), ones_bd, preferred_element_type=F32)


def _sigmoid(x):
    return 1.0 / (1.0 + jnp.exp(-x))


def _silu(x):
    return x * _sigmoid(x)


def _log_sigmoid(x):
    return jnp.minimum(x, 0.0) - jnp.log1p(jnp.exp(-jnp.abs(x)))


def _rms_rows(x, w):
    return x * lax.rsqrt(jnp.mean(x * x, axis=-1, keepdims=True) + NORM_EPS) * w


def _shift_rows(x, prev8, k):
    rolled = pltpu.roll(x, k, axis=0)
    row8 = lax.broadcasted_iota(jnp.int32, (8, x.shape[1]), 0)
    top = jnp.where(row8 < k, pltpu.roll(prev8, k, axis=0), rolled[0:8])
    return jnp.concatenate([top, rolled[8:]], axis=0)


def _stack2(x, lo):
    return jnp.concatenate([jnp.where(lo, x, 0.0), jnp.where(lo, 0.0, x)], axis=0)


def _rms_kernel(x_ref, w_ref, u_ref):
    u_ref[...] = _rms_rows(x_ref[...], w_ref[...]).astype(BF16)


def _prenorm(x2d, w, tm=512):
    t, d = x2d.shape
    return pl.pallas_call(
        _rms_kernel,
        grid=(t // tm,),
        in_specs=[pl.BlockSpec((tm, d), lambda i: (i, 0)),
                  pl.BlockSpec((1, d), lambda i: (0, 0))],
        out_specs=pl.BlockSpec((tm, d), lambda i: (i, 0)),
        out_shape=jax.ShapeDtypeStruct((t, d), BF16),
        compiler_params=_cparams(1),
        name="prenorm",
    )(x2d, w.reshape(1, d))


def _mm_kernel(u_ref, w_ref, o_ref):
    o_ref[...] = jnp.dot(u_ref[...], w_ref[...], preferred_element_type=F32)


def _in_proj(u, w, layer, tm=2048, tn=1024):
    t, d = u.shape
    n = w.shape[2]
    tm = min(tm, t)
    return pl.pallas_call(
        _mm_kernel,
        grid=(t // tm, n // tn),
        in_specs=[pl.BlockSpec((tm, d), lambda i, j: (i, 0)),
                  pl.BlockSpec((None, d, tn), lambda i, j: (layer, 0, j))],
        out_specs=pl.BlockSpec((tm, tn), lambda i, j: (i, j)),
        out_shape=jax.ShapeDtypeStruct((t, n), F32),
        compiler_params=_cparams(2),
        name="in_proj",
    )(u, w)


def _outproj_kernel(o0_ref, o1_ref, o2_ref, o3_ref, w_ref, h_ref, wpost_ref, wpre_ref,
                    hout_ref, u_ref):
    acc = jnp.dot(o0_ref[...], w_ref[0 * GROUP:1 * GROUP, :], preferred_element_type=F32)
    acc += jnp.dot(o1_ref[...], w_ref[1 * GROUP:2 * GROUP, :], preferred_element_type=F32)
    acc += jnp.dot(o2_ref[...], w_ref[2 * GROUP:3 * GROUP, :], preferred_element_type=F32)
    acc += jnp.dot(o3_ref[...], w_ref[3 * GROUP:4 * GROUP, :], preferred_element_type=F32)
    h = h_ref[...] + _rms_rows(acc, wpost_ref[...])
    hout_ref[...] = h
    u_ref[...] = _rms_rows(h, wpre_ref[...]).astype(BF16)


def _out_proj(outs, w, layer, h, w_post, w_pre, tm=512):
    t, d = h.shape
    tm = min(tm, t)
    row = lambda i: (i, 0)
    fixed = lambda i: (0, 0)
    return pl.pallas_call(
        _outproj_kernel,
        grid=(t // tm,),
        in_specs=[pl.BlockSpec((tm, GROUP), row)] * 4 + [
            pl.BlockSpec((None, d, d), lambda i: (layer, 0, 0)),
            pl.BlockSpec((tm, d), row),
            pl.BlockSpec((1, d), fixed),
            pl.BlockSpec((1, d), fixed)],
        out_specs=[pl.BlockSpec((tm, d), row), pl.BlockSpec((tm, d), row)],
        out_shape=[jax.ShapeDtypeStruct((t, d), F32), jax.ShapeDtypeStruct((t, d), BF16)],
        compiler_params=_cparams(1),
        name="out_proj",
    )(*outs, w, h, w_post.reshape(1, d), w_pre.reshape(1, d))


def _ffn_up_kernel(u_ref, wg_ref, wu_ref, a_ref):
    u = u_ref[...]
    tf = a_ref.shape[1]
    for c0 in range(0, tf, tf // 2):
        cols = slice(c0, c0 + tf // 2)
        g = jnp.dot(u, wg_ref[:, cols].astype(BF16), preferred_element_type=F32)
        up = jnp.dot(u, wu_ref[:, cols].astype(BF16), preferred_element_type=F32)
        a_ref[:, cols] = (_silu(g) * up).astype(BF16)


def _ffn_down_kernel(a_ref, wd_ref, h_ref, wpost_ref, wpre_ref, hout_ref, u_next_ref):
    kk = pl.program_id(1)

    @pl.when(kk == 0)
    def _():
        hout_ref[...] = jnp.dot(a_ref[...], wd_ref[...], preferred_element_type=F32)

    @pl.when(kk > 0)
    def _():
        hout_ref[...] += jnp.dot(a_ref[...], wd_ref[...], preferred_element_type=F32)

    @pl.when(kk == pl.num_programs(1) - 1)
    def _():
        h = h_ref[...] + _rms_rows(hout_ref[...], wpost_ref[...])
        hout_ref[...] = h
        u_next_ref[...] = _rms_rows(h, wpre_ref[...]).astype(BF16)


def _ffn(u, h, layer, wg, wu, wd, w_post, w_pre_next, tm_up=2048, tf=512, tm_down=1024, tk=1408):
    t, d = h.shape
    f = wg.shape[2]
    tm_up, tm_down = min(tm_up, t), min(tm_down, t)
    act = pl.pallas_call(
        _ffn_up_kernel,
        grid=(t // tm_up, f // tf),
        in_specs=[pl.BlockSpec((tm_up, d), lambda i, j: (i, 0)),
                  pl.BlockSpec((None, d, tf), lambda i, j: (layer, 0, j)),
                  pl.BlockSpec((None, d, tf), lambda i, j: (layer, 0, j))],
        out_specs=pl.BlockSpec((tm_up, tf), lambda i, j: (i, j)),
        out_shape=jax.ShapeDtypeStruct((t, f), BF16),
        compiler_params=_cparams(2),
        name="ffn_up",
    )(u, wg, wu)
    row = lambda i, k: (i, 0)
    fixed = lambda i, k: (0, 0)
    return pl.pallas_call(
        _ffn_down_kernel,
        grid=(t // tm_down, f // tk),
        in_specs=[pl.BlockSpec((tm_down, tk), lambda i, k: (i, k)),
                  pl.BlockSpec((None, tk, d), lambda i, k: (layer, k, 0)),
                  pl.BlockSpec((tm_down, d), row),
                  pl.BlockSpec((1, d), fixed),
                  pl.BlockSpec((1, d), fixed)],
        out_specs=[pl.BlockSpec((tm_down, d), row), pl.BlockSpec((tm_down, d), row)],
        out_shape=[jax.ShapeDtypeStruct((t, d), F32), jax.ShapeDtypeStruct((t, d), BF16)],
        compiler_params=pltpu.CompilerParams(dimension_semantics=("arbitrary",) * 2,
                                             vmem_limit_bytes=VMEM_LIMIT_MAX),
        name="ffn_down",
    )(act, wd, h, w_post.reshape(1, d), w_pre_next.reshape(1, d))


def _block_ones():
    i = np.arange(LANES)
    return (i[:, None] // HEAD_DIM == i[None, :] // HEAD_DIM).astype(np.float32)


def _tri_incl(c, block=None):
    i = np.arange(c)
    m = i[None, :] <= i[:, None]
    if block is not None:
        m = m & (i[None, :] // block == i[:, None] // block)
    return m.astype(np.float32)


def _rope_tables(seq):
    half = HEAD_DIM // 2
    lane = np.arange(LANES)
    inv_freq = ROPE_BASE ** (-jnp.arange(half, dtype=F32) / half)
    ang = jnp.arange(seq, dtype=F32)[:, None] * inv_freq[None, :]
    cos, sin = jnp.cos(ang), jnp.sin(ang)
    fidx = (lane % HEAD_DIM) % half
    sign = np.where((lane % HEAD_DIM) < half, -1.0, 1.0).astype(np.float32)
    return cos[:, fidx], sin[:, fidx] * sign


def _retention_tables():
    c = MIX_TS
    log_gamma = jnp.log1p(-jnp.exp2(-5.0 - jnp.arange(N_HEADS, dtype=F32)))
    idx = jnp.arange(c, dtype=F32)
    rel = idx[:, None] - idx[None, :]
    intra = jnp.where(rel >= 0, jnp.exp(log_gamma[:, None, None] * jnp.maximum(rel, 0.0)), 0.0)
    q_dec = jnp.exp(log_gamma[:, None] * (idx + 1.0))
    k_dec = jnp.exp(log_gamma[:, None] * (c - 1.0 - idx))
    chunk_dec = jnp.exp(log_gamma * c)
    per_lane = lambda t: jnp.repeat(t.T, HEAD_DIM, axis=1)
    bd = jnp.asarray(_block_ones())
    lane_head = np.arange(LANES) // HEAD_DIM
    sdec = jnp.stack([bd * chunk_dec[2 * j + lane_head][None, :] for j in range(N_PAIRS)])
    intra2 = intra.reshape(N_PAIRS, 2 * c, c)
    return intra2, per_lane(q_dec), per_lane(k_dec), sdec


def _mlstm_tables():
    c = MIX_TS
    esel = np.zeros((LANES, N_HEADS * c), np.float32)
    epf = np.zeros((LANES, GROUP), np.float32)
    epi = np.zeros((LANES, GROUP), np.float32)
    for h in range(N_HEADS):
        esel[N_HEADS + h, h * c:(h + 1) * c] = 1.0
        epf[N_HEADS + h, h * HEAD_DIM:(h + 1) * HEAD_DIM] = 1.0
        epi[h, h * HEAD_DIM:(h + 1) * HEAD_DIM] = 1.0
    return (jnp.asarray(_tri_incl(c), BF16), jnp.asarray(esel, BF16), jnp.asarray(epf, BF16),
            jnp.asarray(epi, BF16))


def _fixed(ndim):
    return lambda i, t: (0,) * ndim


def _full(a):
    return pl.BlockSpec(a.shape, _fixed(a.ndim))


def _seq_spec(width, col_block=0):
    return pl.BlockSpec((MIX_G, MIX_TS, width), lambda i, t: (i, t, col_block))


def _mix_scratch(shape, dtype):
    return pltpu.VMEM((MIX_G,) + tuple(shape), dtype)


def _mixer_kernel(bodies, n_in, n_seq, n_out, n_scr, delays, *refs):
    ins, outs, scrs = [], [], []
    pos = 0
    for group, counts in ((ins, n_in), (outs, n_out), (scrs, n_scr)):
        for n in counts:
            group.append(refs[pos:pos + n])
            pos += n

    @pl.when(pl.program_id(1) == 0)
    def _():
        for scr in scrs:
            for ref in scr:
                ref[...] = jnp.zeros_like(ref)

    active, waits = [], []
    for body, i, o, s, nseq, delay in zip(bodies, ins, outs, scrs, n_seq, delays):
        for g in range(MIX_G):
            row = lambda ref: ref.at[pl.ds(g, 1)]
            args = ([row(r) for r in i[:nseq]] + list(i[nseq:]) + [row(r) for r in o]
                    + [r.at[g] for r in s])
            active.append(body(*args))
            waits.append(delay)
    while active:
        for gen in list(active):
            k = active.index(gen)
            if waits[k] > 0:
                waits[k] -= 1
            elif next(gen, "done") == "done":
                active.pop(k)
                waits.pop(k)


def _run_mixers(specs, b, s, delays=None):
    kern = functools.partial(_mixer_kernel, tuple(sp["body"] for sp in specs),
                             tuple(len(sp["args"]) for sp in specs),
                             tuple(sp["n_seq"] for sp in specs),
                             tuple(len(sp["out_shape"]) for sp in specs),
                             tuple(len(sp["scratch"]) for sp in specs),
                             tuple(delays) if delays else (0,) * len(specs))
    flat = lambda key: [x for sp in specs for x in sp[key]]
    res = pl.pallas_call(
        kern,
        grid=(b // MIX_G, s // MIX_TS),
        in_specs=flat("in_specs"),
        out_specs=flat("out_specs"),
        out_shape=flat("out_shape"),
        scratch_shapes=flat("scratch"),
        compiler_params=_cparams(2),
        name="mix_" + "_".join(sp["name"] for sp in specs),
    )(*flat("args"))
    outs, pos = {}, 0
    for sp in specs:
        n = len(sp["out_shape"])
        outs[sp["name"]] = res[pos:pos + n]
        pos += n
    return outs


def _out_bf16(b, s):
    return jax.ShapeDtypeStruct((b, s, GROUP), BF16)


def _ret_body(p_ref, cos_ref, sin_ref, intra_ref, qdec_ref, kdec_ref, sdec_ref, bd_ref,
              ones_ref, o_ref, st_ref):
    c = MIX_TS
    cos = cos_ref[...]
    sin = sin_ref[...]
    lane = lax.broadcasted_iota(jnp.int32, (c, LANES), 1)
    first_half = (lane & (HEAD_DIM - 1)) < HEAD_DIM // 2
    lo = lane < HEAD_DIM
    ones_bd = ones_ref[...]
    bd = bd_ref[...]

    def rot(t):
        swapped = jnp.where(first_half, pltpu.roll(t, LANES - HEAD_DIM // 2, axis=1),
                            pltpu.roll(t, HEAD_DIM // 2, axis=1))
        return t * cos + swapped * sin

    pairs = range(N_PAIRS)
    sls = [slice(j * LANES, (j + 1) * LANES) for j in pairs]
    qs = [rot(p_ref[0, :, j * LANES:(j + 1) * LANES]) for j in pairs]
    ks = [rot(p_ref[0, :, GROUP + j * LANES:GROUP + (j + 1) * LANES]) * HEAD_DIM ** -0.5
          for j in pairs]
    vbs = [p_ref[0, :, 2 * GROUP + j * LANES:2 * GROUP + (j + 1) * LANES].astype(BF16)
           for j in pairs]
    sts = [st_ref[j] for j in pairs]
    yield
    scores = [_dot_nt(_stack2(qs[j], lo), ks[j]) * intra_ref[j] for j in pairs]
    pvs = [_dot(scores[j], vbs[j]) for j in pairs]
    inters = [_dot(qs[j] * qdec_ref[:, sls[j]], sts[j]) for j in pairs]
    outs = [jnp.where(lo, pvs[j][0:c], pvs[j][c:2 * c]) + inters[j] for j in pairs]
    yield
    for j in pairs:
        st_ref[j] = sdec_ref[j] * sts[j] + bd * _dot_tn(ks[j] * kdec_ref[:, sls[j]], vbs[j])
    mss = [_bsum(outs[j] * outs[j], ones_bd) * (1.0 / HEAD_DIM) for j in pairs]
    for j in pairs:
        g = p_ref[0, :, 3 * GROUP + j * LANES:3 * GROUP + (j + 1) * LANES]
        o_ref[0, :, sls[j]] = (outs[j] * lax.rsqrt(mss[j] + NORM_EPS) * _silu(g)).astype(BF16)
    yield


def _ret_spec(p3, tables):
    b, s, _ = p3.shape
    c = MIX_TS
    cos, sin, intra, qdec, kdec, sdec, bd, ones_bd = tables
    tab_spec = pl.BlockSpec((c, LANES), lambda i, t: (t, 0))
    return dict(
        name="ret", body=_ret_body, n_seq=1,
        args=[p3, cos, sin, intra, qdec, kdec, sdec, bd, ones_bd],
        in_specs=[_seq_spec(4 * GROUP, COL_RET // (4 * GROUP)), tab_spec, tab_spec,
                  _full(intra), _full(qdec), _full(kdec), _full(sdec), _full(bd), _full(ones_bd)],
        out_specs=[_seq_spec(GROUP)], out_shape=[_out_bf16(b, s)],
        scratch=[_mix_scratch((N_PAIRS, LANES, LANES), F32)])


def _level_ref(cum, h):
    ts, n = cum.shape
    if h >= 8:
        return jnp.concatenate(
            [jnp.broadcast_to(cum[b0 + h - 1:b0 + h, :], (2 * h, n)) for b0 in range(0, ts, 2 * h)],
            axis=0)
    cum3 = cum.reshape(ts // 8, 8, n)
    brow = lambda i: jnp.broadcast_to(cum3[:, i:i + 1, :], cum3.shape).reshape(ts, n)
    sub = lax.broadcasted_iota(jnp.int32, (ts, n), 0) & 7
    if h == 4:
        return brow(3)
    if h == 2:
        return jnp.where(sub < 4, brow(1), brow(5))
    return jnp.where(sub < 2, brow(0), jnp.where(sub < 4, brow(2), jnp.where(sub < 6, brow(4), brow(6))))


def _hgrn_body(p_ref, loglb_ref, log1mlb_ref, onemlb_ref, normw_ref, tri_ref, bd_ref, ones_ref,
               o_ref, st_ref):
    ts = MIX_TS
    pf = p_ref[0, :, GROUP:2 * GROUP]
    q = _silu(p_ref[0, :, 0:GROUP]) * HEAD_DIM ** -0.5
    e_f = jnp.exp(-jnp.abs(pf))
    k = onemlb_ref[...] * (jnp.where(pf >= 0.0, e_f, 1.0) / (1.0 + e_f))
    a = loglb_ref[...]
    bb = log1mlb_ref[...] + (jnp.minimum(pf, 0.0) - jnp.log1p(e_f))
    log_f = jnp.maximum(a, bb) + jnp.log1p(jnp.exp(-jnp.abs(a - bb)))
    cum = _dot_3x(tri_ref[...], log_f)
    yield

    ones_bd = ones_ref[...]
    bd = bd_ref[...]
    lo = lax.broadcasted_iota(jnp.int32, (ts, LANES), 1) < HEAD_DIM
    row2 = lax.broadcasted_iota(jnp.int32, (2 * ts, ts), 0) & (ts - 1)
    col2 = lax.broadcasted_iota(jnp.int32, (2 * ts, ts), 1)
    diff2 = jnp.where(col2 < row2, row2 ^ col2, 0)

    lo4 = (lax.broadcasted_iota(jnp.int32, (ts, GROUP), 1) & HEAD_DIM) == 0
    q_a = jnp.where(lo4, q, 0.0)
    q_b = (q - q_a).astype(BF16)
    q_a = q_a.astype(BF16)
    k_bf = k.astype(BF16)
    scores = [None] * N_PAIRS
    h = ts // 2
    while h >= 1:
        e = jnp.exp(-jnp.abs(cum - _level_ref(cum, h))).astype(BF16)
        qa_l, qb_l, k_l = q_a * e, q_b * e, k_bf * e
        level = (diff2 >> (h.bit_length() - 1)) == 1
        for j in range(N_PAIRS):
            sl = slice(j * LANES, (j + 1) * LANES)
            sc = _dot_nt(jnp.concatenate([qa_l[:, sl], qb_l[:, sl]], axis=0), k_l[:, sl])
            scores[j] = jnp.where(level, sc, 0.0 if scores[j] is None else scores[j])
        h //= 2
        yield

    q_in = q * jnp.exp(cum)
    c_end = cum[ts - 1:ts, :]
    k_out = k * jnp.exp(c_end - cum)
    diag = q * k
    for j in range(N_PAIRS):
        sl = slice(j * LANES, (j + 1) * LANES)
        vj = p_ref[0, :, 2 * GROUP + j * LANES:2 * GROUP + (j + 1) * LANES]
        gj = p_ref[0, :, 3 * GROUP + j * LANES:3 * GROUP + (j + 1) * LANES]
        pv = _dot(scores[j], vj)
        st_t = st_ref[j]
        out = (jnp.where(lo, pv[0:ts], pv[ts:2 * ts]) + _bsum(diag[:, sl], ones_bd) * vj
               + _dot_nt(q_in[:, sl], st_t))
        st_ref[j] = st_t * jnp.exp(c_end[:, sl]) + bd * _dot_tn(vj, k_out[:, sl])
        ms = _bsum(out * out, ones_bd) * (1.0 / HEAD_DIM)
        y = out * lax.rsqrt(ms + NORM_EPS) * normw_ref[:, sl]
        o_ref[0, :, sl] = (y * _sigmoid(gj)).astype(BF16)
        yield


def _hgrn_spec(p3, lb, norm_w, consts):
    b, s, _ = p3.shape
    tri, bd, ones_bd = consts
    row = lambda v: v.reshape(1, GROUP)
    args = [p3, row(jnp.log(lb)), row(jnp.log1p(-lb)), row(1.0 - lb), row(norm_w), tri, bd, ones_bd]
    return dict(
        name="hgrn", body=_hgrn_body, n_seq=1, args=args,
        in_specs=[_seq_spec(4 * GROUP, COL_HGRN // (4 * GROUP))] + [_full(a) for a in args[1:]],
        out_specs=[_seq_spec(GROUP)], out_shape=[_out_bf16(b, s)],
        scratch=[_mix_scratch((N_PAIRS, LANES, LANES), F32)])


def _mlstm_body(pm_ref, pg_ref, convw_ref, convb_ref, gbias_ref, normw_ref, tri_ref, esel_ref,
                epf_ref, epi_ref, bd_ref, ones_ref, o_ref,
                prev_ref, cm_ref, n_ref, m_ref, mgl_ref):
    c = MIX_TS
    ones_bd = ones_ref[...]
    bd = bd_ref[...]

    x = pm_ref[0, :, 0:2 * GROUP]
    prev = prev_ref[...]
    acc = convb_ref[...] + x * convw_ref[CONV_WIDTH - 1:CONV_WIDTH, :]
    for sh in range(1, CONV_WIDTH):
        acc = acc + _shift_rows(x, prev, sh) * convw_ref[CONV_WIDTH - 1 - sh:CONV_WIDTH - sh, :]
    prev_ref[...] = x[c - 8:c]
    qk = _silu(acc)
    q_all = qk[:, 0:GROUP]
    k_all = qk[:, GROUP:2 * GROUP] * HEAD_DIM ** -0.5
    yield

    lane = lax.broadcasted_iota(jnp.int32, (c, LANES), 1)
    gb = pg_ref[0] + gbias_ref[...]
    gl = jnp.where(lane < N_HEADS, gb, jnp.where(lane < 2 * N_HEADS, _log_sigmoid(gb), 0.0))
    cum = _dot_3x(tri_ref[...], gl)
    mgl = mgl_ref[...]
    xm = cum + mgl
    colb = _dot_x3(xm, esel_ref[...])
    xm_t = xm.T
    gl_t = gl.T

    row = lax.broadcasted_iota(jnp.int32, (c, c), 0)
    col = lax.broadcasted_iota(jnp.int32, (c, c), 1)
    causal = col <= row
    dmat, w_inter, e_neg_m = [], [], []
    for h in range(N_HEADS):
        log_inter = colb[:, h * c:(h + 1) * c]
        logd = jnp.where(
            causal, log_inter + (gl_t[h:h + 1, :] - xm_t[N_HEADS + h:N_HEADS + h + 1, :]), NEG_BIG)
        m_t = jnp.maximum(jnp.max(logd, axis=1, keepdims=True), log_inter)
        dmat.append(jnp.exp(logd - m_t))
        w_inter.append(jnp.exp(log_inter - m_t))
        e_neg_m.append(jnp.exp(-m_t))
    yield

    cumcol = _dot_x3(cum, epf_ref[...])
    icol = _dot_x3(gl, epi_ref[...])
    c_end = cumcol[c - 1:c, :]
    m_old = m_ref[...]
    log_w = c_end - cumcol + icol
    m_new = jnp.maximum(c_end + m_old, jnp.max(log_w, axis=0, keepdims=True))
    decay = jnp.exp(c_end + m_old - m_new)
    kw_all = k_all * jnp.exp(log_w - m_new)
    n_old = n_ref[...]
    n_ref[...] = decay * n_old + jnp.sum(kw_all, axis=0, keepdims=True)
    m_ref[...] = m_new
    c_end_gl = cum[c - 1:c, :]
    log_w_gl = c_end_gl - cum + pltpu.roll(gl, N_HEADS, axis=1)
    m_new_gl = jnp.maximum(c_end_gl + mgl, jnp.max(log_w_gl, axis=0, keepdims=True))
    lane1 = lax.broadcasted_iota(jnp.int32, (1, LANES), 1)
    mgl_ref[...] = jnp.where((lane1 >= N_HEADS) & (lane1 < 2 * N_HEADS), m_new_gl, 0.0)
    yield

    lo = lane < HEAD_DIM
    pairs = range(N_PAIRS)
    sls = [slice(j * LANES, (j + 1) * LANES) for j in pairs]
    ones_full = jnp.ones((c, LANES), BF16)
    v1s = [jnp.concatenate(
        [pm_ref[0, :, 2 * GROUP + j * LANES:2 * GROUP + (j + 1) * LANES].astype(BF16), ones_full],
        axis=1) for j in pairs]
    cms = [cm_ref[j] for j in pairs]
    scores = [_dot_nt(_stack2(q_all[:, sls[j]], lo), k_all[:, sls[j]])
              * jnp.concatenate([dmat[2 * j], dmat[2 * j + 1]], axis=0) for j in pairs]
    pvr = [_dot(scores[j], v1s[j]) for j in pairs]
    qcs = [_dot(q_all[:, sls[j]], cms[j]) for j in pairs]
    qns = [_bsum(q_all[:, sls[j]] * n_old[:, sls[j]], ones_bd) for j in pairs]
    yield
    for j in pairs:
        cm_ref[j] = cms[j] * decay[:, sls[j]] + bd * _dot_tn(kw_all[:, sls[j]], v1s[j][:, 0:LANES])
    hhs = []
    for j in pairs:
        wi = jnp.where(lo, w_inter[2 * j], w_inter[2 * j + 1])
        num = jnp.where(lo, pvr[j][0:c, 0:LANES], pvr[j][c:2 * c, 0:LANES]) + wi * qcs[j]
        den = (jnp.where(lo, pvr[j][0:c, LANES:2 * LANES], pvr[j][c:2 * c, LANES:2 * LANES])
               + wi * qns[j])
        floor = jnp.where(lo, e_neg_m[2 * j], e_neg_m[2 * j + 1])
        hhs.append(num / jnp.maximum(jnp.abs(den), floor))
    mss = [_bsum(hhs[j] * hhs[j], ones_bd) * (1.0 / HEAD_DIM) for j in pairs]
    for j in pairs:
        oj = pm_ref[0, :, 3 * GROUP + j * LANES:3 * GROUP + (j + 1) * LANES]
        y = hhs[j] * lax.rsqrt(mss[j] + NORM_EPS) * normw_ref[:, sls[j]]
        o_ref[0, :, sls[j]] = (_sigmoid(oj) * y).astype(BF16)
    yield


def _mlstm_spec(p3, conv_w, conv_b, i_bias, f_bias, norm_w, consts):
    b, s, _ = p3.shape
    tri, esel, epf, epi, bd, ones_bd = consts
    gbias = jnp.zeros((1, LANES), F32).at[0, 0:N_HEADS].set(i_bias).at[0, N_HEADS:2 * N_HEADS].set(f_bias)
    args = [p3, p3, conv_w, conv_b.reshape(1, -1), gbias, norm_w.reshape(1, GROUP), tri, esel, epf,
            epi, bd, ones_bd]
    return dict(
        name="mlstm", body=_mlstm_body, n_seq=2, args=args,
        in_specs=[_seq_spec(4 * GROUP, COL_ML // (4 * GROUP)), _seq_spec(LANES, COL_ML_GATES // LANES)]
        + [_full(a) for a in args[2:]],
        out_specs=[_seq_spec(GROUP)], out_shape=[_out_bf16(b, s)],
        scratch=[_mix_scratch((8, 2 * GROUP), F32),
                 _mix_scratch((N_PAIRS, LANES, LANES), F32),
                 _mix_scratch((1, GROUP), F32),
                 _mix_scratch((1, GROUP), F32),
                 _mix_scratch((1, LANES), F32)])


def _rwkv_body(first_layer, *refs):
    ts, c = MIX_TS, RWKV_C
    n_ch = ts // c
    if first_layer:
        (prkv_ref, plora_ref, mu_rkv_ref, mu_lora_ref, w0_ref, wup_ref, a0_ref, aup_ref, gup_ref,
         kk_ref, ka_ref, rk_ref, lnw_ref, lnb_ref, tri_ref, ones_ref,
         o_ref, vfirst_out_ref, prev_rkv_ref, prev_lora_ref, st_ref) = refs
    else:
        (prkv_ref, plora_ref, vfirst_ref, mu_rkv_ref, mu_lora_ref, w0_ref, wup_ref, a0_ref, aup_ref,
         gup_ref, kk_ref, ka_ref, rk_ref, lnw_ref, lnb_ref, v0_ref, vdown_ref, vup_ref, tri_ref,
         ones_ref, o_ref, prev_rkv_ref, prev_lora_ref, st_ref) = refs

    ones_bd = ones_ref[...]

    def token_mix(x_ref, prev_ref, mu_ref):
        x = x_ref[0]
        shifted = _shift_rows(x, prev_ref[...], 1)
        prev_ref[...] = x[ts - 8:ts]
        return x + (shifted - x) * mu_ref[...]

    rkv = token_mix(prkv_ref, prev_rkv_ref, mu_rkv_ref)
    lora = token_mix(plora_ref, prev_lora_ref, mu_lora_ref)
    r = rkv[:, 0:GROUP]
    k = rkv[:, GROUP:2 * GROUP]
    v = rkv[:, 2 * GROUP:3 * GROUP]
    wa = lora[:, 0:LANES]
    log_w = -RWKV_DECAY_SCALE * _sigmoid(w0_ref[...] + _dot(jnp.tanh(wa), wup_ref[...]))
    a = _sigmoid(a0_ref[...] + _dot(wa, aup_ref[...]))
    g = _dot(_sigmoid(lora[:, LANES:2 * LANES]), gup_ref[...])
    if first_layer:
        vfirst_out_ref[0] = v
    else:
        mix = _sigmoid(v0_ref[...] + _dot(_dot(v, vdown_ref[...]), vup_ref[...]))
        v = v + (vfirst_ref[0] - v) * mix

    kk_raw = k * kk_ref[...]
    k2 = k * (1.0 + (a - 1.0) * ka_ref[...])
    cum = _dot_3x(tri_ref[...], log_w)
    c_end = [cum[(ci + 1) * c - 1:(ci + 1) * c, :] for ci in range(n_ch)]
    c_end_rows = jnp.concatenate([jnp.broadcast_to(e, (c, GROUP)) for e in c_end], axis=0)
    r_g = r * jnp.exp(cum)
    e_prev = jnp.exp(cum - log_w)
    e_neg = jnp.exp(-cum)
    e_end = jnp.exp(c_end_rows - cum)
    rk_term = r * k2 * rk_ref[...]
    yield

    lane = lax.broadcasted_iota(jnp.int32, (c, LANES), 1)
    lo = lane < HEAD_DIM
    row = lax.broadcasted_iota(jnp.int32, (2 * c, 2 * c), 0)
    col = lax.broadcasted_iota(jnp.int32, (2 * c, 2 * c), 1)
    strict = col < row
    incl = col <= row
    eye = (col == row).astype(F32)

    kk_all, b_all = [], []
    for j in range(N_PAIRS):
        sl = slice(j * LANES, (j + 1) * LANES)
        kkr = kk_raw[:, sl]
        kk = kkr / jnp.maximum(jnp.sqrt(_bsum(kkr * kkr, ones_bd)), 1e-12)
        kk_all.append(kk)
        b_all.append(a[:, sl] * kk)

    insts = [(ci, j) for ci in range(n_ch) for j in range(N_PAIRS)]
    st2 = {}
    for ci, j in insts:
        rs = slice(ci * c, (ci + 1) * c)
        sl = slice(j * LANES, (j + 1) * LANES)
        st2[ci, j] = dict(
            r_t=_stack2(r_g[rs, sl].astype(BF16), lo),
            kk_t=_stack2((kk_all[j][rs] * e_prev[rs, sl]).astype(BF16), lo),
            k_h=_stack2((k2[rs, sl] * e_neg[rs, sl]).astype(BF16), lo),
            b_h=_stack2((b_all[j][rs] * e_neg[rs, sl]).astype(BF16), lo),
            k_e=_stack2((k2[rs, sl] * e_end[rs, sl]).astype(BF16), lo),
            b_e=_stack2((b_all[j][rs] * e_end[rs, sl]).astype(BF16), lo),
            v2=_stack2(v[rs, sl].astype(BF16), lo))
    yield

    def chunk_stages(wave):
        for key in wave:
            d = st2[key]
            prod = _dot_nt(jnp.concatenate([d["kk_t"], d["r_t"]], axis=0),
                           jnp.concatenate([d["b_h"], d["k_h"]], axis=0)).astype(BF16)
            d["lmat"] = jnp.where(strict, prod[0:2 * c, 0:2 * c], 0.0)
            d["a_kk"] = jnp.where(strict, prod[0:2 * c, 2 * c:4 * c], 0.0)
            d["a_rb"] = jnp.where(incl, prod[2 * c:4 * c, 0:2 * c], 0.0)
            d["a_rk"] = jnp.where(incl, prod[2 * c:4 * c, 2 * c:4 * c], 0.0)
        yield
        for key in wave:
            d = st2[key]
            d["s"] = eye - d["lmat"]
            d["p"] = _dot(d["lmat"], d["lmat"]).astype(BF16)
        yield
        for _ in range(4):
            for key in wave:
                d = st2[key]
                both = _dot(jnp.concatenate([d["p"], d["s"].astype(BF16)], axis=0), d["p"])
                d["p"] = both[0:2 * c].astype(BF16)
                d["s"] = d["s"] + both[2 * c:4 * c]
            yield
        for key in wave:
            d = st2[key]
            d["s"] = d["s"] + _dot(d["s"], d["p"])
            d["akv"] = _dot(d["a_kk"], d["v2"]).astype(BF16)
        yield
        for key in wave:
            d = st2[key]
            tw = -_dot(d["s"], jnp.concatenate([d["kk_t"], d["akv"]], axis=1))
            d["w"] = tw[:, 0:LANES].astype(BF16)
            d["u0"] = tw[:, LANES:2 * LANES].astype(BF16)
        yield
        for key in wave:
            d = st2[key]
            zero = jnp.zeros((2 * c, LANES), BF16)
            rhs = jnp.concatenate([jnp.concatenate([d["v2"], zero], axis=1),
                                   jnp.concatenate([d["u0"], d["w"]], axis=1)], axis=0)
            yq = _dot(jnp.concatenate([d["a_rk"], d["a_rb"]], axis=1), rhs)
            d["y0"] = yq[:, 0:LANES]
            d["q"] = d["r_t"] + yq[:, LANES:2 * LANES]
            d["m"] = _dot_tn(d["b_e"], d["w"])
            d["n_t"] = _dot_tn(jnp.concatenate([d["v2"], d["u0"]], axis=0),
                               jnp.concatenate([d["k_e"], d["b_e"]], axis=0))
        yield

    for w0 in range(0, len(insts), RWKV_WAVE):
        yield from chunk_stages(insts[w0:w0 + RWKV_WAVE])

    hts = [st_ref[j] for j in range(N_PAIRS)]
    ys = [[] for _ in range(N_PAIRS)]
    for ci in range(n_ch):
        for j in range(N_PAIRS):
            sl = slice(j * LANES, (j + 1) * LANES)
            d = st2[ci, j]
            ht = hts[j]
            y2 = _dot_nt(d["q"], ht) + d["y0"]
            hts[j] = ht * jnp.exp(c_end[ci][:, sl]) + _dot_nt(ht, d["m"]) + d["n_t"]
            ys[j].append(y2[0:c] + y2[c:2 * c])
    for j in range(N_PAIRS):
        st_ref[j] = hts[j]
    yield
    y_pairs = [jnp.concatenate(ys[j], axis=0) for j in range(N_PAIRS)]
    means = [_dot(y, ones_bd) * (1.0 / HEAD_DIM) for y in y_pairs]
    ycs = [y - m for y, m in zip(y_pairs, means)]
    vars_ = [_dot(yc * yc, ones_bd) * (1.0 / HEAD_DIM) for yc in ycs]
    for j in range(N_PAIRS):
        sl = slice(j * LANES, (j + 1) * LANES)
        yn = ycs[j] * lax.rsqrt(vars_[j] + RWKV_LN_EPS) * lnw_ref[:, sl] + lnb_ref[:, sl]
        bonus = _bsum(rk_term[:, sl], ones_bd) * v[:, sl]
        o_ref[0, :, sl] = ((yn + bonus) * g[:, sl]).astype(BF16)
    yield


def _rwkv_spec(p3, v_first, prm, consts):
    b, s, _ = p3.shape
    tri, ones_bd = consts
    first_layer = v_first is None
    args = [p3, p3]
    in_specs = [_seq_spec(3 * GROUP, COL_RW // (3 * GROUP)),
                _seq_spec(RWKV_LORA_COLS, COL_RW_LORA // RWKV_LORA_COLS)]
    if not first_layer:
        args.append(v_first)
        in_specs.append(_seq_spec(GROUP))
    consts_in = [prm[n] for n in ("mu_rkv", "mu_lora", "w0", "w_up", "a0", "a_up", "g_up", "k_k",
                                  "k_a", "r_k", "ln_w", "ln_b")]
    if not first_layer:
        consts_in += [prm["v0"], prm["v_down"], prm["v_up"]]
    consts_in += [tri, ones_bd]
    args += consts_in
    in_specs += [_full(a) for a in consts_in]
    out_specs = [_seq_spec(GROUP)]
    out_shape = [_out_bf16(b, s)]
    if first_layer:
        out_specs.append(_seq_spec(GROUP))
        out_shape.append(jax.ShapeDtypeStruct((b, s, GROUP), F32))
    return dict(
        name="rwkv", body=functools.partial(_rwkv_body, first_layer),
        n_seq=2 if first_layer else 3, args=args,
        in_specs=in_specs, out_specs=out_specs, out_shape=out_shape,
        scratch=[_mix_scratch((8, 3 * GROUP), F32),
                 _mix_scratch((8, RWKV_LORA_COLS), F32),
                 _mix_scratch((N_PAIRS, LANES, LANES), F32)])


def _pad_cols(w, n):
    return jnp.pad(w, ((0, 0), (0, n - w.shape[1])))


def _pad_rows(w, n):
    return jnp.pad(w, ((0, n - w.shape[0]), (0, 0)))


def _layout_w_in(w):
    g = GROUP
    w = w.astype(BF16)
    ml_gates = w[..., 12 * g:12 * g + 2 * N_HEADS]
    pad = jnp.zeros(w.shape[:-1] + (N_IN_PAD - w.shape[-1],), w.dtype)
    return jnp.concatenate([w[..., 0:12 * g], w[..., 12 * g + 2 * N_HEADS:], ml_gates, pad], axis=-1)


def _rwkv_params(l, rwkv_mu, rwkv_w0, rwkv_w_up, rwkv_a0, rwkv_a_up, rwkv_g_up, rwkv_k_k, rwkv_k_a,
                 rwkv_r_k, rwkv_ln_w, rwkv_ln_b, rwkv_v0, rwkv_v_down, rwkv_v_up):
    g = GROUP
    mu = rwkv_mu[l]
    prm = {
        "mu_rkv": mu[0:3 * g].reshape(1, -1), "mu_lora": mu[3 * g:].reshape(1, -1),
        "w0": rwkv_w0[l].reshape(1, g), "a0": rwkv_a0[l].reshape(1, g),
        "w_up": _pad_rows(rwkv_w_up[l], LANES).astype(BF16),
        "a_up": jnp.concatenate([jnp.zeros_like(rwkv_w_up[l]), rwkv_a_up[l]], axis=0).astype(BF16),
        "g_up": rwkv_g_up[l].astype(BF16),
        "k_k": rwkv_k_k[l].reshape(1, g), "k_a": rwkv_k_a[l].reshape(1, g),
        "r_k": rwkv_r_k[l].reshape(1, g), "ln_w": rwkv_ln_w[l].reshape(1, g),
        "ln_b": rwkv_ln_b[l].reshape(1, g),
    }
    if l > 0:
        prm["v0"] = rwkv_v0[l - 1].reshape(1, g)
        prm["v_down"] = _pad_cols(rwkv_v_down[l - 1], LANES).astype(BF16)
        prm["v_up"] = _pad_rows(rwkv_v_up[l - 1], LANES).astype(BF16)
    return prm


def kernel(x, w_in, w_out, norm_pre_mix, norm_post_mix, norm_pre_ffn, norm_post_ffn, w_ffn_gate, w_ffn_up, w_ffn_down, hgrn_lb_logits, hgrn_norm_w, mlstm_conv_w, mlstm_conv_b, mlstm_i_bias, mlstm_f_bias, mlstm_norm_w, rwkv_mu, rwkv_w0, rwkv_w_up, rwkv_a0, rwkv_a_up, rwkv_g_up, rwkv_k_k, rwkv_k_a, rwkv_r_k, rwkv_ln_w, rwkv_ln_b, rwkv_v0, rwkv_v_down, rwkv_v_up):
    b, s, d = x.shape
    depth = w_in.shape[0]
    t = b * s
    g = GROUP

    lb_cum = jnp.cumsum(jax.nn.softmax(hgrn_lb_logits.astype(F32), axis=0), axis=0)
    lower_bounds = lb_cum - lb_cum[0]

    bd_f32 = jnp.asarray(_block_ones(), F32)
    ones_bd = jnp.asarray(_block_ones(), BF16)
    cos, sin = _rope_tables(s)
    ret_tables = (cos, sin) + _retention_tables() + (bd_f32, ones_bd)
    hgrn_consts = (jnp.asarray(_tri_incl(MIX_TS), BF16), bd_f32, ones_bd)
    mlstm_consts = _mlstm_tables() + (bd_f32, ones_bd)
    rwkv_consts = (jnp.asarray(_tri_incl(MIX_TS, RWKV_C), BF16), ones_bd)

    w_in_bf = _layout_w_in(w_in)
    w_out_bf = w_out.astype(BF16)
    w_down_bf = w_ffn_down.astype(BF16)

    h = x.reshape(t, d)
    u = _prenorm(h, norm_pre_mix[0])
    v_first = None
    for l in range(depth):
        p3 = _in_proj(u, w_in_bf, l).reshape(b, s, N_IN_PAD)

        prm = _rwkv_params(l, rwkv_mu, rwkv_w0, rwkv_w_up, rwkv_a0, rwkv_a_up, rwkv_g_up, rwkv_k_k,
                           rwkv_k_a, rwkv_r_k, rwkv_ln_w, rwkv_ln_b, rwkv_v0, rwkv_v_down, rwkv_v_up)
        specs = {
            "ret": _ret_spec(p3, ret_tables),
            "hgrn": _hgrn_spec(p3, lower_bounds[l], hgrn_norm_w[l], hgrn_consts),
            "mlstm": _mlstm_spec(p3, mlstm_conv_w[l], mlstm_conv_b[l], mlstm_i_bias[l],
                                 mlstm_f_bias[l], mlstm_norm_w[l], mlstm_consts),
            "rwkv": _rwkv_spec(p3, v_first, prm, rwkv_consts),
        }
        mixed = {}
        for group in MIXER_GROUPS:
            mixed.update(_run_mixers([specs[name] for name, _ in group], b, s,
                                     [delay for _, delay in group]))
        if v_first is None:
            v_first = mixed["rwkv"][1]

        outs = [mixed[name][0].reshape(t, g) for name in ("ret", "hgrn", "mlstm", "rwkv")]
        h, u = _out_proj(outs, w_out_bf, l, h, norm_post_mix[l], norm_pre_ffn[l])
        w_pre_next = norm_pre_mix[l + 1] if l + 1 < depth else norm_pre_mix[l]
        h, u = _ffn(u, h, l, w_ffn_gate, w_ffn_up, w_down_bf, norm_post_ffn[l], w_pre_next)
    return h.reshape(b, s, d)
```
